```python
import math
import jax, jax.numpy as jnp
from jax import lax
import numpy as np

D_MODEL = 2048
BATCH = 4
SEQ = 2048
DEPTH = 2
DEC_BATCH = 128
DEC_SEQ = 4
PAST_LEN = 16384
PAGE_SIZE = 128

N_MIXERS = 2
N_RET_LAYERS = (DEPTH + 1) // 2
N_HG_LAYERS = DEPTH // 2
RET_HEADS = 8
RET_DK = D_MODEL // RET_HEADS
RET_DV = 2 * RET_DK
RET_QK_WIDTH = RET_HEADS * RET_DK
RET_V_WIDTH = RET_HEADS * RET_DV
ROPE_BASE = 10000.0
HG_EXPAND = 128
HG_HEADS = D_MODEL // HG_EXPAND
HG_DK = HG_EXPAND
HG_DV = D_MODEL // HG_HEADS
N_EXPERTS = 16
N_GROUPS = 4
EXPERTS_PER_GROUP = N_EXPERTS // N_GROUPS
TOP_K = 2
GROUP_SCORE_TOPK = 2
D_FF_EXPERT = D_MODEL // 2
CHUNK = 64
EPS = 1e-6

kernel_name = "hybrid_retention_hgrn2_grouped_moe_step"


def rms_norm(x, gain=None):
    xf = x.astype(jnp.float32)
    y = xf * lax.rsqrt(jnp.mean(xf * xf, axis=-1, keepdims=True) + EPS)
    if gain is not None:
        y = y * gain.astype(jnp.float32)
    return y.astype(x.dtype)


def rotary(x, pos):
    half = x.shape[-1] // 2
    theta = 1.0 / (ROPE_BASE ** jnp.linspace(0.0, 1.0, half, dtype=jnp.float32))
    ang = pos[:, None] * theta[None, :]
    cos, sin = jnp.cos(ang), jnp.sin(ang)
    xr = x.reshape(x.shape[:-1] + (half, 2))
    x1, x2 = xr[..., 0], xr[..., 1]
    return jnp.stack([x1 * cos - x2 * sin, x1 * sin + x2 * cos], axis=-1).reshape(x.shape)


def gated_linear_scan(q, k, v, log_f, s0):
    B, H, T, dk = q.shape
    C = CHUNK if T % CHUNK == 0 else T
    n = T // C
    f32 = jnp.float32

    def split(a):
        return jnp.moveaxis(a.astype(f32).reshape(B, H, n, C, a.shape[-1]), 2, 0)

    causal = jnp.tril(jnp.ones((C, C), dtype=bool))[:, :, None]
    scalar_decay = log_f.shape[-1] == 1

    def step(S, blk):
        qc, kc, vc, lc = blk
        b = jnp.cumsum(lc, axis=2)
        diff = b[:, :, :, None, :] - b[:, :, None, :, :]
        decay = jnp.where(causal, jnp.exp(jnp.where(causal, diff, 0.0)), 0.0)
        if scalar_decay:
            scores = jnp.einsum('bhtk,bhsk->bhts', qc, kc) * decay[..., 0]
        else:
            scores = jnp.einsum('bhtk,bhsk,bhtsk->bhts', qc, kc, decay)
        o = jnp.einsum('bhts,bhsv->bhtv', scores, vc) + jnp.einsum('bhtk,bhkv->bhtv', qc * jnp.exp(b), S)
        b_last = b[:, :, -1:, :]
        S_new = jnp.exp(b_last[:, :, 0, :])[..., None] * S + jnp.einsum('bhsk,bhsv->bhkv', kc * jnp.exp(b_last - b), vc)
        return S_new, o

    S, o = lax.scan(step, s0.astype(f32), (split(q), split(k), split(v), split(log_f)))
    o = jnp.moveaxis(o, 0, 2).reshape(B, H, T, v.shape[-1])
    return o, S.astype(s0.dtype)


def retention_mixer(h, w_in, w_out, s0, pos0):
    B, T, _ = h.shape
    q, k, v, g = jnp.split(h @ w_in, [RET_QK_WIDTH, 2 * RET_QK_WIDTH, 2 * RET_QK_WIDTH + RET_V_WIDTH], axis=-1)
    heads = lambda a, d: a.reshape(B, T, RET_HEADS, d).transpose(0, 2, 1, 3).astype(jnp.float32)
    pos = (pos0 + jnp.arange(T)).astype(jnp.float32)
    qh = rotary(heads(q, RET_DK), pos)
    kh = rotary(heads(k, RET_DK), pos) * (RET_DK ** -0.5)
    vh = heads(v, RET_DV)
    log_gamma = jnp.log(1.0 - jnp.exp2(-5.0 - jnp.arange(RET_HEADS, dtype=jnp.float32)))
    log_f = jnp.broadcast_to(log_gamma[None, :, None, None], (B, RET_HEADS, T, 1))
    o, S = gated_linear_scan(qh, kh, vh, log_f, s0)
    o = rms_norm(o)
    o = o.transpose(0, 2, 1, 3).reshape(B, T, RET_V_WIDTH).astype(h.dtype) * jax.nn.silu(g)
    return o @ w_out, S


def hgrn2_mixer(h, w_in, w_out, lower_bound, out_norm, s0):
    B, T, D = h.shape
    q, f, i, g = jnp.split(h @ w_in, 4, axis=-1)
    heads = lambda a, d: a.reshape(B, T, HG_HEADS, d).transpose(0, 2, 1, 3).astype(jnp.float32)
    lb = lower_bound.astype(jnp.float32)
    forget = lb + (1.0 - lb) * jax.nn.sigmoid(f.astype(jnp.float32))
    qh = heads(jax.nn.silu(q), HG_DK)
    kh = heads(1.0 - forget, HG_DK)
    log_fh = heads(jnp.log(forget), HG_DK)
    vh = heads(i, HG_DV)
    o, S = gated_linear_scan(qh, kh, vh, log_fh, s0)
    o = rms_norm(o, out_norm.reshape(HG_HEADS, 1, HG_DV))
    o = o.transpose(0, 2, 1, 3).reshape(B, T, D).astype(h.dtype) * jax.nn.silu(g)
    return o @ w_out, S


def grouped_moe(h, router_w, router_b, w_gate, w_up, w_down):
    B, T, D = h.shape
    x = h.reshape(B * T, D)
    scores = jax.nn.sigmoid(jnp.einsum('nd,de->ne', x.astype(jnp.float32), router_w.astype(jnp.float32)))
    biased = scores + router_b.astype(jnp.float32)
    group_top = lax.top_k(biased.reshape(-1, N_GROUPS, EXPERTS_PER_GROUP), GROUP_SCORE_TOPK)[0]
    sel_group = jnp.argmax(jnp.sum(group_top, axis=-1), axis=-1)
    in_group = jnp.repeat(jax.nn.one_hot(sel_group, N_GROUPS) > 0.5, EXPERTS_PER_GROUP, axis=-1)
    _, idx = lax.top_k(jnp.where(in_group, biased, -jnp.inf), TOP_K)
    w = jnp.take_along_axis(scores, idx, axis=-1)
    w = w / jnp.sum(w, axis=-1, keepdims=True)
    gates = jnp.sum(jax.nn.one_hot(idx, N_EXPERTS, dtype=jnp.float32) * w[..., None], axis=1).astype(h.dtype)
    a = jnp.einsum('nd,edf->nef', x, w_gate)
    u = jnp.einsum('nd,edf->nef', x, w_up)
    hid = jax.nn.silu(a) * u * gates[..., None]
    return jnp.einsum('nef,efd->nd', hid, w_down).reshape(B, T, D)


def trunk(x, c, s_ret, s_hg, pos0, ada_w, ada_b, norm_gains, ret_w_in, ret_w_out, hg_w_in, hg_w_out,
          hg_lower_bound, hg_out_norm, router_w, router_b, exp_w_gate, exp_w_up, exp_w_down):
    sm = jax.nn.softmax(hg_lower_bound.astype(jnp.float32), axis=0)
    lb_all = jnp.cumsum(sm, axis=0) - sm[0]
    new_ret, new_hg = [], []
    for l in range(DEPTH):
        mod = jax.nn.silu(c) @ ada_w[l] + ada_b[l]
        sh_m, sc_m, g_m, sh_f, sc_f, g_f = [m[:, None, :] for m in jnp.split(mod, 6, axis=-1)]
        h = rms_norm(x, norm_gains[l, 0]) * (1.0 + sc_m) + sh_m
        j = l // N_MIXERS
        if l % N_MIXERS == 0:
            y, s = retention_mixer(h, ret_w_in[j], ret_w_out[j], s_ret[j], pos0)
            new_ret.append(s)
        else:
            y, s = hgrn2_mixer(h, hg_w_in[j], hg_w_out[j], lb_all[l], hg_out_norm[j], s_hg[j])
            new_hg.append(s)
        x = x + g_m * rms_norm(y, norm_gains[l, 1])
        h = rms_norm(x, norm_gains[l, 2]) * (1.0 + sc_f) + sh_f
        y = grouped_moe(h, router_w, router_b, exp_w_gate[l], exp_w_up[l], exp_w_down[l])
        x = x + g_f * rms_norm(y, norm_gains[l, 3])
    return x, jnp.stack(new_ret), jnp.stack(new_hg)


def setup_inputs(seed: int = 0) -> dict:
    key = jax.random.key(seed)
    ks = jax.random.split(key, 24)
    nrm = lambda k, shape, scale: jax.random.normal(k, shape, jnp.float32) * scale
    D = D_MODEL
    return {
        "x_prompt": nrm(ks[0], (BATCH, SEQ, D), 1.0),
        "x_sample": nrm(ks[1], (DEC_BATCH, DEC_SEQ, D), 1.0),
        "c_prompt": nrm(ks[2], (BATCH, D), 1.0),
        "c_sample": nrm(ks[3], (DEC_BATCH, D), 1.0),
        "state_ret": nrm(ks[4], (N_RET_LAYERS, DEC_BATCH, RET_HEADS, RET_DK, RET_DV), 1.0),
        "state_hgrn": nrm(ks[5], (N_HG_LAYERS, DEC_BATCH, HG_HEADS, HG_DK, HG_DV), 1.0),
        "ada_w": nrm(ks[6], (DEPTH, D, 6 * D), 0.3 * D ** -0.5),
        "ada_b": nrm(ks[7], (DEPTH, 6 * D), 0.01),
        "norm_gains": 1.0 + nrm(ks[8], (DEPTH, 4, D), 0.05),
        "ret_w_in": nrm(ks[9], (N_RET_LAYERS, D, 2 * RET_QK_WIDTH + 2 * RET_V_WIDTH), D ** -0.5),
        "ret_w_out": nrm(ks[10], (N_RET_LAYERS, RET_V_WIDTH, D), RET_V_WIDTH ** -0.5),
        "hg_w_in": nrm(ks[11], (N_HG_LAYERS, D, 4 * D), D ** -0.5),
        "hg_w_out": nrm(ks[12], (N_HG_LAYERS, D, D), D ** -0.5),
        "hg_lower_bound": nrm(ks[13], (DEPTH, D), 0.1),
        "hg_out_norm": 1.0 + nrm(ks[14], (N_HG_LAYERS, D), 0.05),
        "router_w": nrm(ks[15], (D, N_EXPERTS), D ** -0.5),
        "router_b": nrm(ks[16], (N_EXPERTS,), 0.01),
        "exp_w_gate": nrm(ks[17], (DEPTH, N_EXPERTS, D, D_FF_EXPERT), D ** -0.5),
        "exp_w_up": nrm(ks[18], (DEPTH, N_EXPERTS, D, D_FF_EXPERT), D ** -0.5),
        "exp_w_down": nrm(ks[19], (DEPTH, N_EXPERTS, D_FF_EXPERT, D), D_FF_EXPERT ** -0.5),
    }


def reference(x_prompt, x_sample, c_prompt, c_sample, state_ret, state_hgrn, ada_w, ada_b, norm_gains,
              ret_w_in, ret_w_out, hg_w_in, hg_w_out, hg_lower_bound, hg_out_norm, router_w, router_b,
              exp_w_gate, exp_w_up, exp_w_down):
    s_ret0 = jnp.zeros((N_RET_LAYERS, x_prompt.shape[0], RET_HEADS, RET_DK, RET_DV), state_ret.dtype)
    s_hg0 = jnp.zeros((N_HG_LAYERS, x_prompt.shape[0], HG_HEADS, HG_DK, HG_DV), state_hgrn.dtype)
    y_prompt, ret_prompt, hg_prompt = trunk(
        x_prompt, c_prompt, s_ret0, s_hg0, 0, ada_w, ada_b, norm_gains, ret_w_in, ret_w_out,
        hg_w_in, hg_w_out, hg_lower_bound, hg_out_norm, router_w, router_b, exp_w_gate, exp_w_up, exp_w_down)
    y_sample, ret_sample, hg_sample = trunk(
        x_sample, c_sample, state_ret, state_hgrn, PAST_LEN, ada_w, ada_b, norm_gains, ret_w_in, ret_w_out,
        hg_w_in, hg_w_out, hg_lower_bound, hg_out_norm, router_w, router_b, exp_w_gate, exp_w_up, exp_w_down)
    return (y_prompt, y_sample, ret_prompt, hg_prompt, ret_sample, hg_sample)
```

```python
import functools
import math

import numpy as np
import jax
import jax.numpy as jnp
from jax import lax
from jax.experimental import pallas as pl
from jax.experimental.pallas import tpu as pltpu

F32, BF16, I32 = jnp.float32, jnp.bfloat16, jnp.int32

EPS = 1e-6
ROPE_BASE = 10000.0
PAST_LEN = 16384
RET_DK = 256
RET_DV = 512
HG_DH = 128
N_EXPERTS = 16
GROUP_SHIFT = 2
N_GROUPS = N_EXPERTS >> GROUP_SHIFT

TOKEN_TILE = 256
PROJ_TILE = 512
PROJ_COLS = 1024
EXPERT_TILE = 512
RET_CHUNK = 256
HG_CHUNK = 128
HG_LEAF = 16
HG_LEVELS = (64, 32, 16)
SAMPLE_ROWS = 16
VMEM_LIMIT = 56 * 1024 * 1024


def _params(*sem):
    return pltpu.CompilerParams(dimension_semantics=sem, vmem_limit_bytes=VMEM_LIMIT)


def _rms(x):
    return x * lax.rsqrt(jnp.mean(x * x, axis=-1, keepdims=True) + EPS)


def _silu(x):
    return x * jax.nn.sigmoid(x)


def _dot(a, b):
    return jnp.dot(a, b, preferred_element_type=F32)


def _dot_nt(a, b):
    return lax.dot_general(a, b, (((1,), (1,)), ((), ())), preferred_element_type=F32)


def _dot_tn(a, b):
    return lax.dot_general(a, b, (((0,), (0,)), ((), ())), preferred_element_type=F32)


def _split2(x):
    hi = x.astype(BF16)
    lo = (x - hi.astype(F32)).astype(BF16)
    return hi, lo


def _split3(x):
    hi = x.astype(BF16)
    r = x - hi.astype(F32)
    mid = r.astype(BF16)
    lo = (r - mid.astype(F32)).astype(BF16)
    return hi, mid, lo


def _ada_kernel(c_ref, w_ref, b_ref, o_ref):
    s = _silu(c_ref[...]).astype(BF16)
    o_ref[...] = _dot(s, w_ref[...].astype(BF16)) + b_ref[...]


def _ada(c_all, ada_w, ada_b):
    depth, d, d6 = ada_w.shape
    m = c_all.shape[0]
    tn = min(1024, d6)
    return pl.pallas_call(
        _ada_kernel,
        grid=(depth, d6 // tn),
        in_specs=[pl.BlockSpec((m, d), lambda l, j: (0, 0)),
                  pl.BlockSpec((None, d, tn), lambda l, j: (l, 0, j)),
                  pl.BlockSpec((None, 1, tn), lambda l, j: (l, 0, j))],
        out_specs=pl.BlockSpec((None, m, tn), lambda l, j: (l, 0, j)),
        out_shape=jax.ShapeDtypeStruct((depth, m, d6), F32),
        compiler_params=_params("arbitrary", "arbitrary"),
        name="ada_mod",
    )(c_all, ada_w, ada_b.reshape(depth, 1, d6))


class _Tokens:
    def __init__(self, n_prompt_batch, seq, n_sample, d):
        self.d = d
        self.n_prompt = n_prompt_batch * seq
        self.n = self.n_prompt + n_sample
        self.prompt_tiles = self.n_prompt // TOKEN_TILE
        self.tiles_per_batch = seq // TOKEN_TILE
        self.n_batch = n_prompt_batch
        self.tiles = self.n // TOKEN_TILE

    def mod_specs(self, comp):
        d, tpb, nb, npt = self.d, self.tiles_per_batch, self.n_batch, self.prompt_tiles
        return [pl.BlockSpec((None, None, 1, d), lambda i: (jnp.minimum(i // tpb, nb - 1), comp, 0, 0)),
                pl.BlockSpec((TOKEN_TILE, d), lambda i: (jnp.maximum(i - npt, 0), comp))]

    def row_spec(self, width):
        return pl.BlockSpec((TOKEN_TILE, width), lambda i: (i, 0))


def _const_spec(shape):
    return pl.BlockSpec(shape, lambda i: (0,) * len(shape))


def _pick(is_sample, p_ref, s_ref):
    return jnp.where(is_sample, s_ref[...], p_ref[...])


def _prenorm_kernel(prompt_tiles, x_ref, g_ref, sh_p, sh_s, sc_p, sc_s, h_ref):
    is_sample = pl.program_id(0) >= prompt_tiles
    h = _rms(x_ref[...]) * g_ref[...] * (1.0 + _pick(is_sample, sc_p, sc_s)) + _pick(is_sample, sh_p, sh_s)
    h_ref[...] = h.astype(BF16)


def _prenorm(tok, x, gain, mod_p, mod_s):
    d = tok.d
    return pl.pallas_call(
        functools.partial(_prenorm_kernel, tok.prompt_tiles),
        grid=(tok.tiles,),
        in_specs=[tok.row_spec(d), _const_spec((1, d))] + tok.mod_specs(0) + tok.mod_specs(1),
        out_specs=tok.row_spec(d),
        out_shape=jax.ShapeDtypeStruct((tok.n, d), BF16),
        compiler_params=_params("arbitrary"),
        name="prenorm",
    )(x, gain, mod_p, mod_s, mod_p, mod_s)


def _proj_kernel(h_ref, w_ref, o_ref, wb_ref):
    @pl.when(pl.program_id(1) == 0)
    def _():
        wb_ref[...] = w_ref[...].astype(BF16)

    o_ref[...] = _dot(h_ref[...], wb_ref[...]).astype(BF16)


def _proj(h, w):
    n, d = h.shape
    p = w.shape[1]
    tm = PROJ_TILE if n % PROJ_TILE == 0 else TOKEN_TILE
    tn = min(PROJ_COLS, p)
    return pl.pallas_call(
        _proj_kernel,
        grid=(p // tn, n // tm),
        in_specs=[pl.BlockSpec((tm, d), lambda j, i: (i, 0)),
                  pl.BlockSpec((d, tn), lambda j, i: (0, j))],
        out_specs=pl.BlockSpec((tm, tn), lambda j, i: (i, j)),
        out_shape=jax.ShapeDtypeStruct((n, p), BF16),
        scratch_shapes=[pltpu.VMEM((d, tn), BF16)],
        compiler_params=_params("arbitrary", "arbitrary"),
        name="in_proj",
    )(h, w)


def _rope_tables(pos):
    half = RET_DK // 2
    theta = 1.0 / (ROPE_BASE ** jnp.linspace(0.0, 1.0, half, dtype=F32))
    ang = pos.astype(F32)[:, None] * theta[None, :]
    cos, sin = jnp.cos(ang), jnp.sin(ang)
    return (jnp.repeat(cos, 2, axis=1),
            jnp.stack([-sin, sin], axis=-1).reshape(pos.shape[0], RET_DK))


def _rot(x, cos, sin_signed):
    lane = lax.broadcasted_iota(I32, x.shape, 1)
    width = x.shape[1]
    nbr = jnp.where((lane & 1) == 0, pltpu.roll(x, width - 1, 1), pltpu.roll(x, 1, 1))
    return x * cos + nbr * sin_signed


def _gate_out(o, g_ref_val):
    return (_rms(o) * _silu(g_ref_val.astype(F32))).astype(BF16)


def _ret_prompt_kernel(q_ref, k_ref, v_ref, g_ref, cos_ref, sin_ref, lg_ref, o_ref, s_out_ref, s_ref):
    c = RET_CHUNK
    seq = q_ref.shape[0]
    lg = lg_ref[0:1, 0:1]
    ti = lax.broadcasted_iota(I32, (c, c), 0)
    si = lax.broadcasted_iota(I32, (c, c), 1)
    decay = jnp.where(ti >= si, jnp.exp((ti - si).astype(F32) * lg), 0.0)
    tcol = lax.broadcasted_iota(I32, (c, 1), 0).astype(F32)
    dq = jnp.exp((tcol + 1.0) * lg)
    dk = jnp.exp((float(c - 1) - tcol) * lg)
    dchunk = jnp.exp(float(c) * lg)
    s_ref[...] = jnp.zeros_like(s_ref)

    def body(ci, carry):
        r0 = pl.multiple_of(ci * c, c)
        rows = pl.ds(r0, c)
        cos, sin = cos_ref[rows, :], sin_ref[rows, :]
        q = _rot(q_ref[rows, :].astype(F32), cos, sin)
        k = _rot(k_ref[rows, :].astype(F32), cos, sin) * (RET_DK ** -0.5)
        v = v_ref[rows, :]
        s = s_ref[...]
        scores = _dot_nt(q.astype(BF16), k.astype(BF16)) * decay
        o = _dot(scores.astype(BF16), v) + _dot((q * dq).astype(BF16), s.astype(BF16))
        s_ref[...] = dchunk * s + _dot_tn((k * dk).astype(BF16), v)
        o_ref[rows, :] = _gate_out(o, g_ref[rows, :])
        return carry

    lax.fori_loop(0, seq // c, body, 0)
    s_out_ref[...] = s_ref[...]


def _ret_prompt(proj, n_batch, seq, heads, cos, sin, log_gamma):
    qk_blocks = heads
    return pl.pallas_call(
        _ret_prompt_kernel,
        grid=(n_batch, heads),
        in_specs=[pl.BlockSpec((seq, RET_DK), lambda b, h: (b, h)),
                  pl.BlockSpec((seq, RET_DK), lambda b, h: (b, qk_blocks + h)),
                  pl.BlockSpec((seq, RET_DV), lambda b, h: (b, heads + h)),
                  pl.BlockSpec((seq, RET_DV), lambda b, h: (b, 2 * heads + h)),
                  pl.BlockSpec((seq, RET_DK), lambda b, h: (0, 0)),
                  pl.BlockSpec((seq, RET_DK), lambda b, h: (0, 0)),
                  pl.BlockSpec((None, 1, 128), lambda b, h: (h, 0, 0))],
        out_specs=[pl.BlockSpec((seq, RET_DV), lambda b, h: (b, h)),
                   pl.BlockSpec((None, None, RET_DK, RET_DV), lambda b, h: (b, h, 0, 0))],
        out_shape=[jax.ShapeDtypeStruct((n_batch * seq, heads * RET_DV), BF16),
                   jax.ShapeDtypeStruct((n_batch, heads, RET_DK, RET_DV), F32)],
        scratch_shapes=[pltpu.VMEM((RET_DK, RET_DV), F32)],
        compiler_params=_params("arbitrary", "arbitrary"),
        name="ret_prompt",
    )(proj, proj, proj, proj, cos, sin, log_gamma)


def _ret_sample_kernel(steps, q_ref, k_ref, v_ref, g_ref, cos_ref, sin_ref, lg_ref, s_in_ref, o_ref, s_out_ref):
    rows = q_ref.shape[0]
    shift = int(math.log2(steps))
    lg = lg_ref[0:1, 0:1]
    cos, sin = cos_ref[...], sin_ref[...]
    q = _rot(q_ref[...].astype(F32), cos, sin)
    k = _rot(k_ref[...].astype(F32), cos, sin) * (RET_DK ** -0.5)
    v = v_ref[...]
    ri = lax.broadcasted_iota(I32, (rows, rows), 0)
    ci = lax.broadcasted_iota(I32, (rows, rows), 1)
    pair = ((ri >> shift) == (ci >> shift)) & (ri >= ci)
    decay = jnp.where(pair, jnp.exp((ri - ci).astype(F32) * lg), 0.0)
    o = _dot((_dot_nt(q.astype(BF16), k.astype(BF16)) * decay).astype(BF16), v)
    rid = lax.broadcasted_iota(I32, (rows, 1), 0)
    step = (rid & (steps - 1)).astype(F32)
    row_batch = rid >> shift
    qd = (q * jnp.exp((step + 1.0) * lg)).astype(BF16)
    kd = k * jnp.exp((float(steps - 1) - step) * lg)
    dall = jnp.exp(float(steps) * lg)
    for j in range(rows // steps):
        s = s_in_ref[j]
        o = o + jnp.where(row_batch == j, _dot(qd, s.astype(BF16)), 0.0)
        kj = jnp.where(row_batch == j, kd, 0.0).astype(BF16)
        s_out_ref[j] = dall * s + _dot_tn(kj, v)
    o_ref[...] = _gate_out(o, g_ref[...])


def _ret_sample(proj, state, n_prompt, steps, heads, cos, sin, log_gamma):
    n_sample_batch = state.shape[0]
    per = SAMPLE_ROWS // steps
    base = n_prompt // SAMPLE_ROWS
    return pl.pallas_call(
        functools.partial(_ret_sample_kernel, steps),
        grid=(n_sample_batch // per, heads),
        in_specs=[pl.BlockSpec((SAMPLE_ROWS, RET_DK), lambda b, h: (base + b, h)),
                  pl.BlockSpec((SAMPLE_ROWS, RET_DK), lambda b, h: (base + b, heads + h)),
                  pl.BlockSpec((SAMPLE_ROWS, RET_DV), lambda b, h: (base + b, heads + h)),
                  pl.BlockSpec((SAMPLE_ROWS, RET_DV), lambda b, h: (base + b, 2 * heads + h)),
                  pl.BlockSpec((SAMPLE_ROWS, RET_DK), lambda b, h: (0, 0)),
                  pl.BlockSpec((SAMPLE_ROWS, RET_DK), lambda b, h: (0, 0)),
                  pl.BlockSpec((None, 1, 128), lambda b, h: (h, 0, 0)),
                  pl.BlockSpec((per, None, RET_DK, RET_DV), lambda b, h: (b, h, 0, 0))],
        out_specs=[pl.BlockSpec((SAMPLE_ROWS, RET_DV), lambda b, h: (b, h)),
                   pl.BlockSpec((per, None, RET_DK, RET_DV), lambda b, h: (b, h, 0, 0))],
        out_shape=[jax.ShapeDtypeStruct((n_sample_batch * steps, heads * RET_DV), BF16),
                   jax.ShapeDtypeStruct(state.shape, F32)],
        compiler_params=_params("arbitrary", "arbitrary"),
        name="ret_sample",
    )(proj, proj, proj, proj, cos, sin, log_gamma, state)


def _hg_prefix_matrix():
    c = HG_CHUNK
    t = np.arange(c)[:, None]
    s = np.arange(c)[None, :]
    le = (s <= t).astype(np.float32)
    mats = [le, le * ((s // HG_LEAF) == (t // HG_LEAF))]
    for half in HG_LEVELS:
        mid = (t // (2 * half)) * (2 * half) + half - 1
        mats.append(le - (s <= mid).astype(np.float32))
    return jnp.asarray(np.concatenate(mats, axis=0), dtype=BF16)


def _hg_gates(q, f, lb):
    forget = lb + (1.0 - lb) * jax.nn.sigmoid(f)
    return _silu(q), 1.0 - forget, jnp.log(forget)


def _hg_prompt_kernel(q_ref, f_ref, i_ref, g_ref, lb_ref, gn_ref, pm_ref, o_ref, s_out_ref, s_ref):
    c = HG_CHUNK
    seq = q_ref.shape[0]
    lb = lb_ref[...]
    gain = gn_ref[...]
    ti = lax.broadcasted_iota(I32, (c, c), 0)
    si = lax.broadcasted_iota(I32, (c, c), 1)
    leaf_shift = int(math.log2(HG_LEAF))
    mask_leaf = (si <= ti) & ((ti >> leaf_shift) == (si >> leaf_shift))
    level_masks = []
    for half in HG_LEVELS:
        sh = int(math.log2(2 * half))
        level_masks.append(((ti >> sh) == (si >> sh)) & ((ti & (2 * half - 1)) >= half) & ((si & (2 * half - 1)) < half))
    eye = ti == si
    s_ref[...] = jnp.zeros_like(s_ref)

    def body(ci, carry):
        r0 = pl.multiple_of(ci * c, c)
        rows = pl.ds(r0, c)
        qh, kk, lf = _hg_gates(q_ref[rows, :].astype(F32), f_ref[rows, :].astype(F32), lb)
        v = i_ref[rows, :]
        hi, lo = _split2(lf)
        sums = _dot(pm_ref[...], jnp.concatenate([hi, lo], axis=1))
        sums = sums[:, :HG_DH] + sums[:, HG_DH:]
        b = sums[0:c]
        d_leaf = sums[c:2 * c]
        a = jnp.where(mask_leaf,
                      _dot_nt((qh * jnp.exp(d_leaf)).astype(BF16), (kk * jnp.exp(-d_leaf)).astype(BF16)), 0.0)
        for lvl, mask in enumerate(level_masks):
            w = jnp.exp(-jnp.abs(sums[(2 + lvl) * c:(3 + lvl) * c]))
            a = a + jnp.where(mask, _dot_nt((qh * w).astype(BF16), (kk * w).astype(BF16)), 0.0)
        s = s_ref[...]
        o = _dot(a.astype(BF16), v) + _dot((qh * jnp.exp(b)).astype(BF16), s.astype(BF16))
        b_last = b[c - 1:c, :]
        col = jnp.sum(jnp.where(eye, jnp.exp(b_last), 0.0), axis=1, keepdims=True)
        s_ref[...] = col * s + _dot_tn((kk * jnp.exp(b_last - b)).astype(BF16), v)
        o_ref[rows, :] = (_rms(o) * gain * _silu(g_ref[rows, :].astype(F32))).astype(BF16)
        return carry

    lax.fori_loop(0, seq // c, body, 0)
    s_out_ref[...] = s_ref[...]


def _hg_prompt(proj, n_batch, seq, heads, lb, out_norm):
    pm = _hg_prefix_matrix()
    col = lambda part: (lambda b, h: (b, part * heads + h))
    return pl.pallas_call(
        _hg_prompt_kernel,
        grid=(n_batch, heads),
        in_specs=[pl.BlockSpec((seq, HG_DH), col(0)),
                  pl.BlockSpec((seq, HG_DH), col(1)),
                  pl.BlockSpec((seq, HG_DH), col(2)),
                  pl.BlockSpec((seq, HG_DH), col(3)),
                  pl.BlockSpec((1, HG_DH), lambda b, h: (0, h)),
                  pl.BlockSpec((1, HG_DH), lambda b, h: (0, h)),
                  pl.BlockSpec(pm.shape, lambda b, h: (0, 0))],
        out_specs=[pl.BlockSpec((seq, HG_DH), lambda b, h: (b, h)),
                   pl.BlockSpec((None, None, HG_DH, HG_DH), lambda b, h: (b, h, 0, 0))],
        out_shape=[jax.ShapeDtypeStruct((n_batch * seq, heads * HG_DH), BF16),
                   jax.ShapeDtypeStruct((n_batch, heads, HG_DH, HG_DH), F32)],
        scratch_shapes=[pltpu.VMEM((HG_DH, HG_DH), F32)],
        compiler_params=_params("arbitrary", "arbitrary"),
        name="hg_prompt",
    )(proj, proj, proj, proj, lb, out_norm, pm)


def _hg_sample_kernel(steps, heads, q_ref, f_ref, i_ref, g_ref, lb_ref, gn_ref, s_in_ref, o_ref, s_out_ref):
    rows = q_ref.shape[0]
    shift = int(math.log2(steps))
    ri = lax.broadcasted_iota(I32, (rows, rows), 0)
    ci = lax.broadcasted_iota(I32, (rows, rows), 1)
    prefix = (((ri >> shift) == (ci >> shift)) & (ci <= ri)).astype(BF16)
    rid = lax.broadcasted_iota(I32, (rows, 1), 0)
    step = rid & (steps - 1)
    row_batch = rid >> shift
    ki = lax.broadcasted_iota(I32, (HG_DH, HG_DH), 0)
    vi = lax.broadcasted_iota(I32, (HG_DH, HG_DH), 1)
    eye = ki == vi

    def head(h, carry):
        cols = pl.ds(pl.multiple_of(h * HG_DH, HG_DH), HG_DH)
        qh, kk, lf = _hg_gates(q_ref[:, cols].astype(F32), f_ref[:, cols].astype(F32), lb_ref[:, cols])
        v = i_ref[:, cols]
        vf = v.astype(F32)
        hi, lo = _split2(lf)
        b = _dot(prefix, jnp.concatenate([hi, lo], axis=1))
        b = b[:, :HG_DH] + b[:, HG_DH:]
        o = jnp.zeros((rows, HG_DH), F32)
        for dist in range(steps):
            if dist == 0:
                k_s, b_s, v_s = kk, b, vf
            else:
                k_s, b_s, v_s = (pltpu.roll(kk, dist, 0), pltpu.roll(b, dist, 0), pltpu.roll(vf, dist, 0))
            w = jnp.sum(qh * k_s * jnp.exp(jnp.minimum(b - b_s, 0.0)), axis=1, keepdims=True)
            o = o + jnp.where(step >= dist, w, 0.0) * v_s
        qd = (qh * jnp.exp(b)).astype(BF16)
        for j in range(rows // steps):
            s = s_in_ref[j, h]
            o = o + jnp.where(row_batch == j, _dot(qd, s.astype(BF16)), 0.0)
            b_last = b[(j + 1) * steps - 1:(j + 1) * steps, :]
            col = jnp.sum(jnp.where(eye, jnp.exp(b_last), 0.0), axis=1, keepdims=True)
            kj = jnp.where(row_batch == j, kk * jnp.exp(jnp.minimum(b_last - b, 0.0)), 0.0).astype(BF16)
            s_out_ref[j, h] = col * s + _dot_tn(kj, v)
        o_ref[:, cols] = (_rms(o) * gn_ref[:, cols] * _silu(g_ref[:, cols].astype(F32))).astype(BF16)
        return carry

    lax.fori_loop(0, heads, head, 0)


def _hg_sample(proj, state, n_prompt, steps, heads, lb, out_norm):
    n_sample_batch = state.shape[0]
    d = heads * HG_DH
    per = SAMPLE_ROWS // steps
    base = n_prompt // SAMPLE_ROWS
    col = lambda part: (lambda b: (base + b, part))
    return pl.pallas_call(
        functools.partial(_hg_sample_kernel, steps, heads),
        grid=(n_sample_batch // per,),
        in_specs=[pl.BlockSpec((SAMPLE_ROWS, d), col(0)),
                  pl.BlockSpec((SAMPLE_ROWS, d), col(1)),
                  pl.BlockSpec((SAMPLE_ROWS, d), col(2)),
                  pl.BlockSpec((SAMPLE_ROWS, d), col(3)),
                  pl.BlockSpec((1, d), lambda b: (0, 0)),
                  pl.BlockSpec((1, d), lambda b: (0, 0)),
                  pl.BlockSpec((per, heads, HG_DH, HG_DH), lambda b: (b, 0, 0, 0))],
        out_specs=[pl.BlockSpec((SAMPLE_ROWS, d), lambda b: (b, 0)),
                   pl.BlockSpec((per, heads, HG_DH, HG_DH), lambda b: (b, 0, 0, 0))],
        out_shape=[jax.ShapeDtypeStruct((n_sample_batch * steps, d), BF16),
                   jax.ShapeDtypeStruct(state.shape, F32)],
        compiler_params=_params("arbitrary"),
        name="hg_sample",
    )(proj, proj, proj, proj, lb, out_norm, state)


def _first_max(vals, lane, width):
    m = jnp.max(vals, axis=1, keepdims=True)
    idx = jnp.min(jnp.where(vals == m, lane, width), axis=1, keepdims=True)
    return m, idx


def _route(logits, bias):
    neg = -jnp.inf
    lane = lax.broadcasted_iota(I32, logits.shape, 1)
    group = lane >> GROUP_SHIFT
    scores = jax.nn.sigmoid(logits)
    biased = scores + bias
    best = sel = None
    for gi in range(N_GROUPS):
        vals = jnp.where(group == gi, biased, neg)
        m1, i1 = _first_max(vals, lane, N_EXPERTS)
        m2 = jnp.max(jnp.where(lane == i1, neg, vals), axis=1, keepdims=True)
        total = m1 + m2
        if gi == 0:
            best, sel = total, jnp.zeros_like(i1)
        else:
            better = total > best
            sel = jnp.where(better, gi, sel)
            best = jnp.where(better, total, best)
    vals = jnp.where(group == sel, biased, neg)
    _, e1 = _first_max(vals, lane, N_EXPERTS)
    _, e2 = _first_max(jnp.where(lane == e1, neg, vals), lane, N_EXPERTS)
    w1 = jnp.sum(jnp.where(lane == e1, scores, 0.0), axis=1, keepdims=True)
    w2 = jnp.sum(jnp.where(lane == e2, scores, 0.0), axis=1, keepdims=True)
    tot = w1 + w2
    return e1, e2, w1 / tot, w2 / tot


def _mixout_kernel(prompt_tiles, o_p, o_s, w_ref, x_ref, g1_ref, g2_ref, gm_p, gm_s, sh_p, sh_s, sc_p, sc_s,
                   rw_ref, rb_ref, xo_ref, h_ref, ridx_ref, rwt_ref, cnt_ref, carry_ref):
    i = pl.program_id(0)
    is_sample = i >= prompt_tiles
    tm = x_ref.shape[0]

    @pl.when(i == 0)
    def _():
        carry_ref[...] = jnp.zeros_like(carry_ref)

    y = _dot(_pick(is_sample, o_p, o_s), w_ref[...])
    x = x_ref[...] + _pick(is_sample, gm_p, gm_s) * (_rms(y) * g1_ref[...])
    xo_ref[...] = x
    h = _rms(x) * g2_ref[...] * (1.0 + _pick(is_sample, sc_p, sc_s)) + _pick(is_sample, sh_p, sh_s)
    h_ref[...] = h

    h1, h2, h3 = _split3(h)
    w1, w2, w3 = _split3(rw_ref[...])
    logits = (_dot(h1, w1) + (_dot(h1, w2) + _dot(h2, w1))
              + (_dot(h2, w2) + _dot(h1, w3) + _dot(h3, w1)))
    e1, e2, p1, p2 = _route(logits, rb_ref[...])

    lane = lax.broadcasted_iota(I32, (tm, N_EXPERTS), 1)
    hot1, hot2 = lane == e1, lane == e2
    onehot = (hot1 | hot2).astype(BF16)
    ti = lax.broadcasted_iota(I32, (tm, tm), 0)
    si = lax.broadcasted_iota(I32, (tm, tm), 1)
    before = _dot((si < ti).astype(BF16), onehot) + carry_ref[...]
    r1 = jnp.sum(jnp.where(hot1, before, 0.0), axis=1, keepdims=True).astype(I32)
    r2 = jnp.sum(jnp.where(hot2, before, 0.0), axis=1, keepdims=True).astype(I32)
    carry = carry_ref[...] + jnp.sum(onehot.astype(F32), axis=0, keepdims=True)
    carry_ref[...] = carry

    wide = lax.broadcasted_iota(I32, (tm, 128), 1)
    ridx_ref[...] = jnp.where(wide == 0, e1, jnp.where(wide == 1, e2, jnp.where(wide == 2, r1, r2)))
    rwt_ref[...] = jnp.where(wide == 0, p1, p2)
    cnt_ref[...] = jnp.zeros_like(cnt_ref)
    cnt_ref[0:1, 0:N_EXPERTS] = carry


def _mixout(tok, o_prompt, o_sample, w_out, x, g1, g2, mod_p, mod_s, router_w, router_b):
    d = tok.d
    v = o_prompt.shape[1]
    npt = tok.prompt_tiles
    return pl.pallas_call(
        functools.partial(_mixout_kernel, npt),
        grid=(tok.tiles,),
        in_specs=[pl.BlockSpec((TOKEN_TILE, v), lambda i: (jnp.minimum(i, npt - 1), 0)),
                  pl.BlockSpec((TOKEN_TILE, v), lambda i: (jnp.maximum(i - npt, 0), 0)),
                  pl.BlockSpec((v, d), lambda i: (0, 0), pipeline_mode=pl.Buffered(1)),
                  tok.row_spec(d), _const_spec((1, d)), _const_spec((1, d))]
                 + tok.mod_specs(2) + tok.mod_specs(3) + tok.mod_specs(4)
                 + [_const_spec((d, N_EXPERTS)), _const_spec((1, N_EXPERTS))],
        out_specs=[tok.row_spec(d), tok.row_spec(d), tok.row_spec(128), tok.row_spec(128),
                   _const_spec((8, 128))],
        out_shape=[jax.ShapeDtypeStruct((tok.n, d), F32),
                   jax.ShapeDtypeStruct((tok.n, d), F32),
                   jax.ShapeDtypeStruct((tok.n, 128), I32),
                   jax.ShapeDtypeStruct((tok.n, 128), F32),
                   jax.ShapeDtypeStruct((8, 128), F32)],
        scratch_shapes=[pltpu.VMEM((1, N_EXPERTS), F32)],
        compiler_params=_params("arbitrary"),
        name="mix_out",
    )(o_prompt, o_sample, w_out, x, g1, g2, mod_p, mod_s, mod_p, mod_s, mod_p, mod_s, router_w, router_b)


def _expert_plan(ridx, counts, n_tiles):
    n = ridx.shape[0]
    expert = ridx[:, 0:2]
    rank = ridx[:, 2:4]
    cnt = counts[0, :N_EXPERTS].astype(I32)
    padded = ((cnt + EXPERT_TILE - 1) // EXPERT_TILE) * EXPERT_TILE
    ends = jnp.cumsum(padded)
    starts = ends - padded
    pos = (starts[expert] + rank).reshape(-1)
    slots = jnp.arange(2 * n, dtype=I32)
    row_tok = jnp.zeros((n_tiles * EXPERT_TILE,), I32).at[pos].set(slots >> 1)
    row_dst = jnp.zeros((n_tiles * EXPERT_TILE,), I32).at[pos].set(slots)
    tile_start = jnp.arange(n_tiles, dtype=I32) * EXPERT_TILE
    used = tile_start < ends[-1]
    tile_expert = jnp.sum((tile_start[:, None] >= ends[None, :]).astype(I32), axis=1)
    last_used = jnp.maximum(ends[-1] // EXPERT_TILE - 1, 0)
    tile_expert = jnp.minimum(jnp.where(used, tile_expert, tile_expert[last_used]), N_EXPERTS - 1)
    valid = jnp.clip(cnt[tile_expert] - (tile_start - starts[tile_expert]), 0, EXPERT_TILE)
    valid = jnp.where(used, valid, 0)
    return (tile_expert, valid, row_tok.reshape(n_tiles, 1, EXPERT_TILE), row_dst.reshape(n_tiles, 1, EXPERT_TILE))


def _expert_kernel(te_ref, nv_ref, tok_ref, dst_ref, h_hbm, wg_ref, wu_ref, wd_ref, y_hbm,
                   xbuf, ybuf, sem):
    i = pl.program_id(0)
    n = nv_ref[i]

    def row_in(r):
        return pltpu.make_async_copy(h_hbm.at[pl.ds(tok_ref[0, 0, r], 1)], xbuf.at[pl.ds(r, 1)], sem.at[0])

    def row_out(r):
        return pltpu.make_async_copy(ybuf.at[pl.ds(r, 1)], y_hbm.at[pl.ds(dst_ref[0, 0, r], 1)], sem.at[1])

    def each_row(fn):
        def body(r, carry):
            fn(r)
            return carry
        lax.fori_loop(0, n, body, 0)

    @pl.when(i == 0)
    def _():
        xbuf[...] = jnp.zeros_like(xbuf)

    @pl.when(n > 0)
    def _():
        each_row(lambda r: row_in(r).start())
        each_row(lambda r: row_in(r).wait())
        x = xbuf[...].astype(BF16)
        a = _dot(x, wg_ref[...])
        u = _dot(x, wu_ref[...])
        ybuf[...] = _dot((_silu(a) * u).astype(BF16), wd_ref[...])
        each_row(lambda r: row_out(r).start())
        each_row(lambda r: row_out(r).wait())


def _experts(h, plan, wg, wu, wd):
    n, d = h.shape
    f = wg.shape[2]
    tile_expert, valid, row_tok, row_dst = plan
    n_tiles = row_tok.shape[0]
    smem_rows = pl.BlockSpec((1, 1, EXPERT_TILE), lambda i, te, nv: (i, 0, 0), memory_space=pltpu.SMEM)
    grid_spec = pltpu.PrefetchScalarGridSpec(
        num_scalar_prefetch=2,
        grid=(n_tiles,),
        in_specs=[smem_rows, smem_rows,
                  pl.BlockSpec(memory_space=pl.ANY),
                  pl.BlockSpec((None, d, f), lambda i, te, nv: (te[i], 0, 0)),
                  pl.BlockSpec((None, d, f), lambda i, te, nv: (te[i], 0, 0)),
                  pl.BlockSpec((None, f, d), lambda i, te, nv: (te[i], 0, 0))],
        out_specs=pl.BlockSpec(memory_space=pl.ANY),
        scratch_shapes=[pltpu.VMEM((EXPERT_TILE, d), F32),
                        pltpu.VMEM((EXPERT_TILE, d), F32),
                        pltpu.SemaphoreType.DMA((2,))],
    )
    return pl.pallas_call(
        _expert_kernel,
        grid_spec=grid_spec,
        out_shape=jax.ShapeDtypeStruct((2 * n, d), F32),
        compiler_params=_params("arbitrary"),
        name="experts",
    )(tile_expert, valid, row_tok, row_dst, h, wg, wu, wd)


def _combine(y_ref, rwt_ref, d):
    rw = rwt_ref[...]
    return rw[:, 0:1] * y_ref[:, 0:d] + rw[:, 1:2] * y_ref[:, d:2 * d]


def _moeout_next_kernel(prompt_tiles, x_ref, y_ref, rwt_ref, g3_ref, gf_p, gf_s, gn_ref, sh_p, sh_s, sc_p, sc_s,
                        xo_ref, h_ref):
    is_sample = pl.program_id(0) >= prompt_tiles
    d = x_ref.shape[1]
    y = _combine(y_ref, rwt_ref, d)
    x = x_ref[...] + _pick(is_sample, gf_p, gf_s) * (_rms(y) * g3_ref[...])
    xo_ref[...] = x
    h = _rms(x) * gn_ref[...] * (1.0 + _pick(is_sample, sc_p, sc_s)) + _pick(is_sample, sh_p, sh_s)
    h_ref[...] = h.astype(BF16)


def _moeout_last_kernel(prompt_tiles, x_ref, y_ref, rwt_ref, g3_ref, gf_p, gf_s, xo_ref):
    is_sample = pl.program_id(0) >= prompt_tiles
    d = x_ref.shape[1]
    y = _combine(y_ref, rwt_ref, d)
    xo_ref[...] = x_ref[...] + _pick(is_sample, gf_p, gf_s) * (_rms(y) * g3_ref[...])


def _moeout(tok, x, y2, rwt, g3, mod_p, mod_s, nxt=None):
    d = tok.d
    ins = [tok.row_spec(d), tok.row_spec(2 * d), tok.row_spec(128), _const_spec((1, d))] + tok.mod_specs(5)
    args = [x, y2.reshape(tok.n, 2 * d), rwt, g3, mod_p, mod_s]
    if nxt is None:
        return pl.pallas_call(
            functools.partial(_moeout_last_kernel, tok.prompt_tiles),
            grid=(tok.tiles,), in_specs=ins, out_specs=tok.row_spec(d),
            out_shape=jax.ShapeDtypeStruct((tok.n, d), F32),
            compiler_params=_params("arbitrary"), name="moe_out_last",
        )(*args)
    gain_n, (tok_n, mod_pn, mod_sn) = nxt
    ins = ins + [_const_spec((1, d))] + tok_n.mod_specs(0) + tok_n.mod_specs(1)
    args = args + [gain_n, mod_pn, mod_sn, mod_pn, mod_sn]
    return pl.pallas_call(
        functools.partial(_moeout_next_kernel, tok.prompt_tiles),
        grid=(tok.tiles,), in_specs=ins, out_specs=[tok.row_spec(d), tok.row_spec(d)],
        out_shape=[jax.ShapeDtypeStruct((tok.n, d), F32), jax.ShapeDtypeStruct((tok.n, d), BF16)],
        compiler_params=_params("arbitrary"), name="moe_out_next",
    )(*args)


def kernel(x_prompt, x_sample, c_prompt, c_sample, state_ret, state_hgrn, ada_w, ada_b, norm_gains,
           ret_w_in, ret_w_out, hg_w_in, hg_w_out, hg_lower_bound, hg_out_norm, router_w, router_b,
           exp_w_gate, exp_w_up, exp_w_down):
    n_batch, seq, d = x_prompt.shape
    n_dec, steps, _ = x_sample.shape
    depth = ada_w.shape[0]
    assert depth == 2 and d % RET_DK == 0 and d % HG_DH == 0
    assert seq % RET_CHUNK == 0 and (n_dec * steps) % TOKEN_TILE == 0 and SAMPLE_ROWS % steps == 0
    ret_heads = d // RET_DK
    hg_heads = d // HG_DH
    n_prompt = n_batch * seq
    n_sample = n_dec * steps
    tok = _Tokens(n_batch, seq, n_sample, d)
    n = tok.n

    n_cond = n_batch + n_dec
    pad = (-n_cond) % 8
    c_all = jnp.concatenate([c_prompt, c_sample, jnp.zeros((pad, d), F32)], axis=0)
    mod = _ada(c_all, ada_w, ada_b)
    mods = []
    for l in range(depth):
        mod_p = mod[l, :n_batch].reshape(n_batch, 6, 1, d)
        mod_s = jnp.repeat(mod[l, n_batch:n_cond], steps, axis=0)
        mods.append((mod_p, mod_s))
    gain = lambda l, k: norm_gains[l, k].reshape(1, d)

    x = jnp.concatenate([x_prompt.reshape(n_prompt, d), x_sample.reshape(n_sample, d)], axis=0)
    n_tiles = (2 * n + N_EXPERTS * (EXPERT_TILE - 1)) // EXPERT_TILE + 1
    rw = router_w.astype(F32)
    rb = router_b.astype(F32).reshape(1, N_EXPERTS)

    def channel_mixer(l, x, h, ridx, rwt, counts, nxt):
        plan = _expert_plan(ridx, counts, n_tiles)
        y2 = _experts(h, plan, exp_w_gate[l].astype(BF16), exp_w_up[l].astype(BF16), exp_w_down[l].astype(BF16))
        return _moeout(tok, x, y2, rwt, gain(l, 3), *mods[l], nxt=nxt)

    h = _prenorm(tok, x, gain(0, 0), *mods[0])
    proj = _proj(h, ret_w_in[0])
    log_gamma = jnp.log(1.0 - jnp.exp2(-5.0 - jnp.arange(ret_heads, dtype=F32)))
    log_gamma = jnp.broadcast_to(log_gamma[:, None, None], (ret_heads, 1, 128))
    cos_p, sin_p = _rope_tables(jnp.arange(seq))
    cos_s, sin_s = _rope_tables(PAST_LEN + jnp.arange(steps))
    reps = SAMPLE_ROWS // steps
    cos_s, sin_s = jnp.tile(cos_s, (reps, 1)), jnp.tile(sin_s, (reps, 1))
    o_p, ret_prompt = _ret_prompt(proj, n_batch, seq, ret_heads, cos_p, sin_p, log_gamma)
    o_s, ret_sample = _ret_sample(proj, state_ret[0], n_prompt, steps, ret_heads, cos_s, sin_s, log_gamma)
    x, h, ridx, rwt, counts = _mixout(tok, o_p, o_s, ret_w_out[0].astype(BF16), x, gain(0, 1), gain(0, 2),
                                      *mods[0], rw, rb)
    x, h = channel_mixer(0, x, h, ridx, rwt, counts, (gain(1, 0), (tok,) + mods[1]))

    sm = jax.nn.softmax(hg_lower_bound.astype(F32), axis=0)
    lb = (jnp.cumsum(sm, axis=0) - sm[0])[1].reshape(1, d)
    proj = _proj(h, hg_w_in[0])
    out_norm = hg_out_norm[0].reshape(1, d)
    o_p, hg_prompt = _hg_prompt(proj, n_batch, seq, hg_heads, lb, out_norm)
    o_s, hg_sample = _hg_sample(proj, state_hgrn[0], n_prompt, steps, hg_heads, lb, out_norm)
    x, h, ridx, rwt, counts = _mixout(tok, o_p, o_s, hg_w_out[0].astype(BF16), x, gain(1, 1), gain(1, 2),
                                      *mods[1], rw, rb)
    x = channel_mixer(1, x, h, ridx, rwt, counts, None)

    return (x[:n_prompt].reshape(n_batch, seq, d), x[n_prompt:].reshape(n_dec, steps, d),
            ret_prompt[None], hg_prompt[None], ret_sample[None], hg_sample[None])
```

```python
import functools
import math

import numpy as np
import jax
import jax.numpy as jnp
from jax import lax
from jax.experimental import pallas as pl
from jax.experimental.pallas import tpu as pltpu

F32, BF16, I32 = jnp.float32, jnp.bfloat16, jnp.int32

EPS = 1e-6
ROPE_BASE = 10000.0
PAST_LEN = 16384
RET_DK = 256
RET_DV = 512
HG_DH = 128
N_EXPERTS = 16
GROUP_SHIFT = 2
N_GROUPS = N_EXPERTS >> GROUP_SHIFT

TOKEN_TILE = 256
PROJ_TILE = 512
PROJ_COLS = 1024
EXPERT_TILE = 384
RET_CHUNK = 256
HG_CHUNK = 128
HG_LEAF = 16
HG_LEVELS = (64, 32, 16)
HG_HEADS_PER_STEP = 4
SAMPLE_ROWS = 16
VMEM_LIMIT = 56 * 1024 * 1024


def _params(*sem):
    return pltpu.CompilerParams(dimension_semantics=sem, vmem_limit_bytes=VMEM_LIMIT)


def _rms(x):
    return x * lax.rsqrt(jnp.mean(x * x, axis=-1, keepdims=True) + EPS)


def _silu(x):
    return x * jax.nn.sigmoid(x)


def _dot(a, b):
    return jnp.dot(a, b, preferred_element_type=F32)


def _dot_nt(a, b):
    return lax.dot_general(a, b, (((1,), (1,)), ((), ())), preferred_element_type=F32)


def _dot_tn(a, b):
    return lax.dot_general(a, b, (((0,), (0,)), ((), ())), preferred_element_type=F32)


def _split2(x):
    hi = x.astype(BF16)
    lo = (x - hi.astype(F32)).astype(BF16)
    return hi, lo


def _split3(x):
    hi = x.astype(BF16)
    r = x - hi.astype(F32)
    mid = r.astype(BF16)
    lo = (r - mid.astype(F32)).astype(BF16)
    return hi, mid, lo


def _ada_kernel(c_ref, w_ref, b_ref, o_ref):
    s = _silu(c_ref[...]).astype(BF16)
    o_ref[...] = _dot(s, w_ref[...].astype(BF16)) + b_ref[...]


def _ada(c_all, ada_w, ada_b):
    depth, d, d6 = ada_w.shape
    m = c_all.shape[0]
    tn = min(1024, d6)
    return pl.pallas_call(
        _ada_kernel,
        grid=(depth, d6 // tn),
        in_specs=[pl.BlockSpec((m, d), lambda l, j: (0, 0)),
                  pl.BlockSpec((None, d, tn), lambda l, j: (l, 0, j)),
                  pl.BlockSpec((None, 1, tn), lambda l, j: (l, 0, j))],
        out_specs=pl.BlockSpec((None, m, tn), lambda l, j: (l, 0, j)),
        out_shape=jax.ShapeDtypeStruct((depth, m, d6), F32),
        compiler_params=_params("arbitrary", "arbitrary"),
        name="ada_mod",
    )(c_all, ada_w, ada_b.reshape(depth, 1, d6))


class _Tokens:
    def __init__(self, n_prompt_batch, seq, n_sample, d):
        self.d = d
        self.n_prompt = n_prompt_batch * seq
        self.n = self.n_prompt + n_sample
        self.prompt_tiles = self.n_prompt // TOKEN_TILE
        self.tiles_per_batch = seq // TOKEN_TILE
        self.n_batch = n_prompt_batch
        self.tiles = self.n // TOKEN_TILE

    def mod_specs(self, comp):
        d, tpb, nb, npt = self.d, self.tiles_per_batch, self.n_batch, self.prompt_tiles
        return [pl.BlockSpec((None, None, 1, d), lambda i: (jnp.minimum(i // tpb, nb - 1), comp, 0, 0)),
                pl.BlockSpec((TOKEN_TILE, d), lambda i: (jnp.maximum(i - npt, 0), comp),
                             pipeline_mode=pl.Buffered(1))]

    def row_spec(self, width):
        return pl.BlockSpec((TOKEN_TILE, width), lambda i: (i, 0))

    def split_specs(self, width, out=False):
        npt = self.prompt_tiles
        mode = {} if out else dict(pipeline_mode=pl.Buffered(1))
        return [pl.BlockSpec((TOKEN_TILE, width), lambda i: (jnp.minimum(i, npt - 1), 0)),
                pl.BlockSpec((TOKEN_TILE, width), lambda i: (jnp.maximum(i - npt, 0), 0), **mode)]

    def split_shapes(self, width, dtype):
        return [jax.ShapeDtypeStruct((self.n_prompt, width), dtype),
                jax.ShapeDtypeStruct((self.n - self.n_prompt, width), dtype)]


def _const_spec(shape):
    return pl.BlockSpec(shape, lambda i: (0,) * len(shape))


def _pick(is_sample, p_ref, s_ref):
    return jnp.where(is_sample, s_ref[...], p_ref[...])


def _store_split(is_sample, p_ref, s_ref, val):
    @pl.when(is_sample)
    def _():
        s_ref[...] = val

    @pl.when(jnp.logical_not(is_sample))
    def _():
        p_ref[...] = val


def _prenorm_kernel(prompt_tiles, x_p, x_s, g_ref, sh_p, sh_s, sc_p, sc_s, h_ref):
    is_sample = pl.program_id(0) >= prompt_tiles
    x = _pick(is_sample, x_p, x_s)
    h = _rms(x) * g_ref[...] * (1.0 + _pick(is_sample, sc_p, sc_s)) + _pick(is_sample, sh_p, sh_s)
    h_ref[...] = h.astype(BF16)


def _prenorm(tok, x_p, x_s, gain, mod_p, mod_s):
    d = tok.d
    return pl.pallas_call(
        functools.partial(_prenorm_kernel, tok.prompt_tiles),
        grid=(tok.tiles,),
        in_specs=tok.split_specs(d) + [_const_spec((1, d))] + tok.mod_specs(0) + tok.mod_specs(1),
        out_specs=tok.row_spec(d),
        out_shape=jax.ShapeDtypeStruct((tok.n, d), BF16),
        compiler_params=_params("arbitrary"),
        name="prenorm",
    )(x_p, x_s, gain, mod_p, mod_s, mod_p, mod_s)


def _proj_kernel(h_ref, w_ref, o_ref, wb_ref):
    @pl.when(pl.program_id(1) == 0)
    def _():
        wb_ref[...] = w_ref[...].astype(BF16)

    o_ref[...] = _dot(h_ref[...], wb_ref[...]).astype(BF16)


def _proj(h, w):
    n, d = h.shape
    p = w.shape[1]
    tm = PROJ_TILE if n % PROJ_TILE == 0 else TOKEN_TILE
    tn = min(PROJ_COLS, p)
    return pl.pallas_call(
        _proj_kernel,
        grid=(p // tn, n // tm),
        in_specs=[pl.BlockSpec((tm, d), lambda j, i: (i, 0)),
                  pl.BlockSpec((d, tn), lambda j, i: (0, j))],
        out_specs=pl.BlockSpec((tm, tn), lambda j, i: (i, j)),
        out_shape=jax.ShapeDtypeStruct((n, p), BF16),
        scratch_shapes=[pltpu.VMEM((d, tn), BF16)],
        compiler_params=_params("arbitrary", "arbitrary"),
        name="in_proj",
    )(h, w)


def _rope_tables(pos):
    half = RET_DK // 2
    theta = 1.0 / (ROPE_BASE ** jnp.linspace(0.0, 1.0, half, dtype=F32))
    ang = pos.astype(F32)[:, None] * theta[None, :]
    cos, sin = jnp.cos(ang), jnp.sin(ang)
    return (jnp.repeat(cos, 2, axis=1),
            jnp.stack([-sin, sin], axis=-1).reshape(pos.shape[0], RET_DK))


def _rot(x, cos, sin_signed):
    lane = lax.broadcasted_iota(I32, x.shape, 1)
    width = x.shape[1]
    nbr = jnp.where((lane & 1) == 0, pltpu.roll(x, width - 1, 1), pltpu.roll(x, 1, 1))
    return x * cos + nbr * sin_signed


def _gate_out(o, g_ref_val):
    return (_rms(o) * _silu(g_ref_val.astype(F32))).astype(BF16)


def _ret_prompt_kernel(q_ref, k_ref, v_ref, g_ref, cos_ref, sin_ref, lg_ref, o_ref, s_out_ref, s_ref):
    c = RET_CHUNK
    seq = q_ref.shape[0]
    lg = lg_ref[0:1, 0:1]
    ti = lax.broadcasted_iota(I32, (c, c), 0)
    si = lax.broadcasted_iota(I32, (c, c), 1)
    decay = jnp.where(ti >= si, jnp.exp((ti - si).astype(F32) * lg), 0.0)
    tcol = lax.broadcasted_iota(I32, (c, 1), 0).astype(F32)
    dq = jnp.exp((tcol + 1.0) * lg)
    dk = jnp.exp((float(c - 1) - tcol) * lg)
    dchunk = jnp.exp(float(c) * lg)
    s_ref[...] = jnp.zeros_like(s_ref)

    def body(ci, carry):
        r0 = pl.multiple_of(ci * c, c)
        rows = pl.ds(r0, c)
        cos, sin = cos_ref[rows, :], sin_ref[rows, :]
        q = _rot(q_ref[rows, :].astype(F32), cos, sin)
        k = _rot(k_ref[rows, :].astype(F32), cos, sin) * (RET_DK ** -0.5)
        v = v_ref[rows, :]
        s = s_ref[...]
        scores = _dot_nt(q.astype(BF16), k.astype(BF16)) * decay
        o = _dot(scores.astype(BF16), v) + _dot((q * dq).astype(BF16), s.astype(BF16))
        s_ref[...] = dchunk * s + _dot_tn((k * dk).astype(BF16), v)
        o_ref[rows, :] = _gate_out(o, g_ref[rows, :])
        return carry

    lax.fori_loop(0, seq // c, body, 0)
    s_out_ref[...] = s_ref[...]


def _ret_prompt(proj, n_batch, seq, heads, cos, sin, log_gamma):
    qk_blocks = heads
    return pl.pallas_call(
        _ret_prompt_kernel,
        grid=(n_batch, heads),
        in_specs=[pl.BlockSpec((seq, RET_DK), lambda b, h: (b, h)),
                  pl.BlockSpec((seq, RET_DK), lambda b, h: (b, qk_blocks + h)),
                  pl.BlockSpec((seq, RET_DV), lambda b, h: (b, heads + h)),
                  pl.BlockSpec((seq, RET_DV), lambda b, h: (b, 2 * heads + h)),
                  pl.BlockSpec((seq, RET_DK), lambda b, h: (0, 0)),
                  pl.BlockSpec((seq, RET_DK), lambda b, h: (0, 0)),
                  pl.BlockSpec((None, 1, 128), lambda b, h: (h, 0, 0))],
        out_specs=[pl.BlockSpec((seq, RET_DV), lambda b, h: (b, h)),
                   pl.BlockSpec((None, None, RET_DK, RET_DV), lambda b, h: (b, h, 0, 0))],
        out_shape=[jax.ShapeDtypeStruct((n_batch * seq, heads * RET_DV), BF16),
                   jax.ShapeDtypeStruct((n_batch, heads, RET_DK, RET_DV), F32)],
        scratch_shapes=[pltpu.VMEM((RET_DK, RET_DV), F32)],
        compiler_params=_params("arbitrary", "arbitrary"),
        name="ret_prompt",
    )(proj, proj, proj, proj, cos, sin, log_gamma)


def _ret_sample_kernel(steps, q_ref, k_ref, v_ref, g_ref, cos_ref, sin_ref, lg_ref, s_in_ref, o_ref, s_out_ref):
    rows = q_ref.shape[0]
    shift = int(math.log2(steps))
    lg = lg_ref[0:1, 0:1]
    cos, sin = cos_ref[...], sin_ref[...]
    q = _rot(q_ref[...].astype(F32), cos, sin)
    k = _rot(k_ref[...].astype(F32), cos, sin) * (RET_DK ** -0.5)
    v = v_ref[...]
    ri = lax.broadcasted_iota(I32, (rows, rows), 0)
    ci = lax.broadcasted_iota(I32, (rows, rows), 1)
    pair = ((ri >> shift) == (ci >> shift)) & (ri >= ci)
    decay = jnp.where(pair, jnp.exp((ri - ci).astype(F32) * lg), 0.0)
    o = _dot((_dot_nt(q.astype(BF16), k.astype(BF16)) * decay).astype(BF16), v)
    rid = lax.broadcasted_iota(I32, (rows, 1), 0)
    step = (rid & (steps - 1)).astype(F32)
    row_batch = rid >> shift
    qd = (q * jnp.exp((step + 1.0) * lg)).astype(BF16)
    kd = k * jnp.exp((float(steps - 1) - step) * lg)
    dall = jnp.exp(float(steps) * lg)
    for j in range(rows // steps):
        s = s_in_ref[j]
        o = o + jnp.where(row_batch == j, _dot(qd, s.astype(BF16)), 0.0)
        kj = jnp.where(row_batch == j, kd, 0.0).astype(BF16)
        s_out_ref[j] = dall * s + _dot_tn(kj, v)
    o_ref[...] = _gate_out(o, g_ref[...])


def _ret_sample(proj, state, n_prompt, steps, heads, cos, sin, log_gamma):
    n_sample_batch = state.shape[0]
    per = SAMPLE_ROWS // steps
    base = n_prompt // SAMPLE_ROWS
    return pl.pallas_call(
        functools.partial(_ret_sample_kernel, steps),
        grid=(n_sample_batch // per, heads),
        in_specs=[pl.BlockSpec((SAMPLE_ROWS, RET_DK), lambda b, h: (base + b, h)),
                  pl.BlockSpec((SAMPLE_ROWS, RET_DK), lambda b, h: (base + b, heads + h)),
                  pl.BlockSpec((SAMPLE_ROWS, RET_DV), lambda b, h: (base + b, heads + h)),
                  pl.BlockSpec((SAMPLE_ROWS, RET_DV), lambda b, h: (base + b, 2 * heads + h)),
                  pl.BlockSpec((SAMPLE_ROWS, RET_DK), lambda b, h: (0, 0)),
                  pl.BlockSpec((SAMPLE_ROWS, RET_DK), lambda b, h: (0, 0)),
                  pl.BlockSpec((None, 1, 128), lambda b, h: (h, 0, 0)),
                  pl.BlockSpec((per, None, RET_DK, RET_DV), lambda b, h: (b, h, 0, 0))],
        out_specs=[pl.BlockSpec((SAMPLE_ROWS, RET_DV), lambda b, h: (b, h)),
                   pl.BlockSpec((per, None, RET_DK, RET_DV), lambda b, h: (b, h, 0, 0))],
        out_shape=[jax.ShapeDtypeStruct((n_sample_batch * steps, heads * RET_DV), BF16),
                   jax.ShapeDtypeStruct(state.shape, F32)],
        compiler_params=_params("arbitrary", "arbitrary"),
        name="ret_sample",
    )(proj, proj, proj, proj, cos, sin, log_gamma, state)


def _hg_prefix_matrix():
    c = HG_CHUNK
    t = np.arange(c)[:, None]
    s = np.arange(c)[None, :]
    le = (s <= t).astype(np.float32)
    mats = [le, le * ((s // HG_LEAF) == (t // HG_LEAF))]
    for half in HG_LEVELS:
        mid = (t // (2 * half)) * (2 * half) + half - 1
        mats.append(le - (s <= mid).astype(np.float32))
    return jnp.asarray(np.concatenate(mats, axis=0), dtype=BF16)


def _hg_gates(q, f, lb):
    forget = lb + (1.0 - lb) * jax.nn.sigmoid(f)
    return _silu(q), 1.0 - forget, jnp.log(forget)


def _hg_prompt_kernel(q_ref, f_ref, i_ref, g_ref, lb_ref, gn_ref, pm_ref, o_ref, s_out_ref, s_ref):
    c = HG_CHUNK
    seq = q_ref.shape[0]
    n_heads = q_ref.shape[1] // HG_DH
    ti = lax.broadcasted_iota(I32, (c, c), 0)
    si = lax.broadcasted_iota(I32, (c, c), 1)
    leaf_shift = int(math.log2(HG_LEAF))
    mask_leaf = (si <= ti) & ((ti >> leaf_shift) == (si >> leaf_shift))
    level_masks = []
    for half in HG_LEVELS:
        sh = int(math.log2(2 * half))
        level_masks.append(((ti >> sh) == (si >> sh)) & ((ti & (2 * half - 1)) >= half) & ((si & (2 * half - 1)) < half))
    eye = ti == si
    s_ref[...] = jnp.zeros_like(s_ref)

    def one_head(hh, rows):
        cols = slice(hh * HG_DH, (hh + 1) * HG_DH)
        qh, kk, lf = _hg_gates(q_ref[rows, cols].astype(F32), f_ref[rows, cols].astype(F32), lb_ref[:, cols])
        v = i_ref[rows, cols]
        lf2 = jnp.concatenate(_split2(lf), axis=1)

        def sums(part):
            r = _dot(pm_ref[part * c:(part + 1) * c, :], lf2)
            return r[:, :HG_DH] + r[:, HG_DH:]

        d_leaf = sums(1)
        a = jnp.where(mask_leaf,
                      _dot_nt((qh * jnp.exp(d_leaf)).astype(BF16), (kk * jnp.exp(-d_leaf)).astype(BF16)), 0.0)
        for lvl, mask in enumerate(level_masks):
            w = jnp.exp(-jnp.abs(sums(2 + lvl)))
            a = a + jnp.where(mask, _dot_nt((qh * w).astype(BF16), (kk * w).astype(BF16)), 0.0)
        b = sums(0)
        s = s_ref[hh]
        o = _dot(a.astype(BF16), v) + _dot((qh * jnp.exp(b)).astype(BF16), s.astype(BF16))
        b_last = b[c - 1:c, :]
        col = jnp.sum(jnp.where(eye, jnp.exp(b_last), 0.0), axis=1, keepdims=True)
        s_ref[hh] = col * s + _dot_tn((kk * jnp.exp(b_last - b)).astype(BF16), v)
        o_ref[rows, cols] = (_rms(o) * gn_ref[:, cols] * _silu(g_ref[rows, cols].astype(F32))).astype(BF16)

    def body(ci, carry):
        rows = pl.ds(pl.multiple_of(ci * c, c), c)
        for hh in range(n_heads):
            one_head(hh, rows)
        return carry

    lax.fori_loop(0, seq // c, body, 0)
    s_out_ref[...] = s_ref[...]


def _hg_prompt(proj, n_batch, seq, heads, lb, out_norm):
    pm = _hg_prefix_matrix()
    hp = HG_HEADS_PER_STEP
    width = hp * HG_DH
    groups = heads // hp
    col = lambda part: (lambda b, h: (b, part * groups + h))
    return pl.pallas_call(
        _hg_prompt_kernel,
        grid=(n_batch, groups),
        in_specs=[pl.BlockSpec((seq, width), col(0)),
                  pl.BlockSpec((seq, width), col(1)),
                  pl.BlockSpec((seq, width), col(2)),
                  pl.BlockSpec((seq, width), col(3)),
                  pl.BlockSpec((1, width), lambda b, h: (0, h)),
                  pl.BlockSpec((1, width), lambda b, h: (0, h)),
                  pl.BlockSpec(pm.shape, lambda b, h: (0, 0))],
        out_specs=[pl.BlockSpec((seq, width), lambda b, h: (b, h)),
                   pl.BlockSpec((None, hp, HG_DH, HG_DH), lambda b, h: (b, h, 0, 0))],
        out_shape=[jax.ShapeDtypeStruct((n_batch * seq, heads * HG_DH), BF16),
                   jax.ShapeDtypeStruct((n_batch, heads, HG_DH, HG_DH), F32)],
        scratch_shapes=[pltpu.VMEM((hp, HG_DH, HG_DH), F32)],
        compiler_params=_params("arbitrary", "arbitrary"),
        name="hg_prompt",
    )(proj, proj, proj, proj, lb, out_norm, pm)


def _hg_sample_kernel(steps, heads, q_ref, f_ref, i_ref, g_ref, lb_ref, gn_ref, s_in_ref, o_ref, s_out_ref):
    rows = q_ref.shape[0]
    shift = int(math.log2(steps))
    ri = lax.broadcasted_iota(I32, (rows, rows), 0)
    ci = lax.broadcasted_iota(I32, (rows, rows), 1)
    prefix = (((ri >> shift) == (ci >> shift)) & (ci <= ri)).astype(BF16)
    rid = lax.broadcasted_iota(I32, (rows, 1), 0)
    step = rid & (steps - 1)
    row_batch = rid >> shift
    ki = lax.broadcasted_iota(I32, (HG_DH, HG_DH), 0)
    vi = lax.broadcasted_iota(I32, (HG_DH, HG_DH), 1)
    eye = ki == vi

    def head(h):
        cols = pl.ds(pl.multiple_of(h * HG_DH, HG_DH), HG_DH)
        qh, kk, lf = _hg_gates(q_ref[:, cols].astype(F32), f_ref[:, cols].astype(F32), lb_ref[:, cols])
        v = i_ref[:, cols]
        vf = v.astype(F32)
        hi, lo = _split2(lf)
        b = _dot(prefix, jnp.concatenate([hi, lo], axis=1))
        b = b[:, :HG_DH] + b[:, HG_DH:]
        o = jnp.zeros((rows, HG_DH), F32)
        for dist in range(steps):
            if dist == 0:
                k_s, b_s, v_s = kk, b, vf
            else:
                k_s, b_s, v_s = (pltpu.roll(kk, dist, 0), pltpu.roll(b, dist, 0), pltpu.roll(vf, dist, 0))
            w = jnp.sum(qh * k_s * jnp.exp(jnp.minimum(b - b_s, 0.0)), axis=1, keepdims=True)
            o = o + jnp.where(step >= dist, w, 0.0) * v_s
        qd = (qh * jnp.exp(b)).astype(BF16)
        for j in range(rows // steps):
            s = s_in_ref[j, h]
            o = o + jnp.where(row_batch == j, _dot(qd, s.astype(BF16)), 0.0)
            b_last = b[(j + 1) * steps - 1:(j + 1) * steps, :]
            col = jnp.sum(jnp.where(eye, jnp.exp(b_last), 0.0), axis=1, keepdims=True)
            kj = jnp.where(row_batch == j, kk * jnp.exp(jnp.minimum(b_last - b, 0.0)), 0.0).astype(BF16)
            s_out_ref[j, h] = col * s + _dot_tn(kj, v)
        o_ref[:, cols] = (_rms(o) * gn_ref[:, cols] * _silu(g_ref[:, cols].astype(F32))).astype(BF16)

    def group(gi, carry):
        for hh in range(HG_HEADS_PER_STEP):
            head(gi * HG_HEADS_PER_STEP + hh)
        return carry

    lax.fori_loop(0, heads // HG_HEADS_PER_STEP, group, 0)


def _hg_sample(proj, state, n_prompt, steps, heads, lb, out_norm):
    n_sample_batch = state.shape[0]
    d = heads * HG_DH
    per = SAMPLE_ROWS // steps
    base = n_prompt // SAMPLE_ROWS
    col = lambda part: (lambda b: (base + b, part))
    return pl.pallas_call(
        functools.partial(_hg_sample_kernel, steps, heads),
        grid=(n_sample_batch // per,),
        in_specs=[pl.BlockSpec((SAMPLE_ROWS, d), col(0)),
                  pl.BlockSpec((SAMPLE_ROWS, d), col(1)),
                  pl.BlockSpec((SAMPLE_ROWS, d), col(2)),
                  pl.BlockSpec((SAMPLE_ROWS, d), col(3)),
                  pl.BlockSpec((1, d), lambda b: (0, 0)),
                  pl.BlockSpec((1, d), lambda b: (0, 0)),
                  pl.BlockSpec((per, heads, HG_DH, HG_DH), lambda b: (b, 0, 0, 0))],
        out_specs=[pl.BlockSpec((SAMPLE_ROWS, d), lambda b: (b, 0)),
                   pl.BlockSpec((per, heads, HG_DH, HG_DH), lambda b: (b, 0, 0, 0))],
        out_shape=[jax.ShapeDtypeStruct((n_sample_batch * steps, d), BF16),
                   jax.ShapeDtypeStruct(state.shape, F32)],
        compiler_params=_params("arbitrary"),
        name="hg_sample",
    )(proj, proj, proj, proj, lb, out_norm, state)


def _first_max(vals, lane, width):
    m = jnp.max(vals, axis=1, keepdims=True)
    idx = jnp.min(jnp.where(vals == m, lane, width), axis=1, keepdims=True)
    return m, idx


def _route(logits, bias):
    neg = -jnp.inf
    lane = lax.broadcasted_iota(I32, logits.shape, 1)
    group = lane >> GROUP_SHIFT
    scores = jax.nn.sigmoid(logits)
    biased = scores + bias
    best = sel = None
    for gi in range(N_GROUPS):
        vals = jnp.where(group == gi, biased, neg)
        m1, i1 = _first_max(vals, lane, N_EXPERTS)
        m2 = jnp.max(jnp.where(lane == i1, neg, vals), axis=1, keepdims=True)
        total = m1 + m2
        if gi == 0:
            best, sel = total, jnp.zeros_like(i1)
        else:
            better = total > best
            sel = jnp.where(better, gi, sel)
            best = jnp.where(better, total, best)
    vals = jnp.where(group == sel, biased, neg)
    _, e1 = _first_max(vals, lane, N_EXPERTS)
    _, e2 = _first_max(jnp.where(lane == e1, neg, vals), lane, N_EXPERTS)
    w1 = jnp.sum(jnp.where(lane == e1, scores, 0.0), axis=1, keepdims=True)
    w2 = jnp.sum(jnp.where(lane == e2, scores, 0.0), axis=1, keepdims=True)
    tot = w1 + w2
    return e1, e2, w1 / tot, w2 / tot


def _pack_pairs(h):
    half = h.shape[1] // 2
    lo = lax.bitcast_convert_type(h[:, :half].astype(BF16).astype(F32), jnp.uint32) >> 16
    hi = lax.bitcast_convert_type(h[:, half:].astype(BF16).astype(F32), jnp.uint32) & jnp.uint32(0xFFFF0000)
    return lo | hi


def _unpack_pairs(p):
    lo = lax.bitcast_convert_type(p << 16, F32).astype(BF16)
    hi = lax.bitcast_convert_type(p & jnp.uint32(0xFFFF0000), F32).astype(BF16)
    return lo, hi


def _mixout_kernel(prompt_tiles, o_p, o_s, w_ref, x_p, x_s, g1_ref, g2_ref, gm_p, gm_s, sh_p, sh_s, sc_p, sc_s,
                   rw_ref, rb_ref, xo_p, xo_s, h_ref, ridx_ref, rwt_ref, cnt_ref, carry_ref):
    i = pl.program_id(0)
    is_sample = i >= prompt_tiles
    tm = x_p.shape[0]

    @pl.when(i == 0)
    def _():
        carry_ref[...] = jnp.zeros_like(carry_ref)

    y = _dot(_pick(is_sample, o_p, o_s), w_ref[...])
    x = _pick(is_sample, x_p, x_s) + _pick(is_sample, gm_p, gm_s) * (_rms(y) * g1_ref[...])
    _store_split(is_sample, xo_p, xo_s, x)
    h = _rms(x) * g2_ref[...] * (1.0 + _pick(is_sample, sc_p, sc_s)) + _pick(is_sample, sh_p, sh_s)
    h_ref[...] = _pack_pairs(h)

    h1, h2 = _split2(h)
    w1, w2 = _split2(rw_ref[...])
    first = _dot(h1, jnp.concatenate([w1, w2], axis=1))
    logits = first[:, :N_EXPERTS] + (first[:, N_EXPERTS:] + _dot(h2, w1))
    e1, e2, p1, p2 = _route(logits, rb_ref[...])

    lane = lax.broadcasted_iota(I32, (tm, N_EXPERTS), 1)
    hot1, hot2 = lane == e1, lane == e2
    onehot = (hot1 | hot2).astype(BF16)
    ti = lax.broadcasted_iota(I32, (tm, tm), 0)
    si = lax.broadcasted_iota(I32, (tm, tm), 1)
    before = _dot((si < ti).astype(BF16), onehot) + carry_ref[...]
    r1 = jnp.sum(jnp.where(hot1, before, 0.0), axis=1, keepdims=True).astype(I32)
    r2 = jnp.sum(jnp.where(hot2, before, 0.0), axis=1, keepdims=True).astype(I32)
    carry = carry_ref[...] + jnp.sum(onehot.astype(F32), axis=0, keepdims=True)
    carry_ref[...] = carry

    wide = lax.broadcasted_iota(I32, (tm, 128), 1)
    ridx_ref[...] = jnp.where(wide == 0, e1, jnp.where(wide == 1, e2, jnp.where(wide == 2, r1, r2)))
    rwt_ref[...] = jnp.where(wide == 0, p1, p2)
    cnt_ref[...] = jnp.zeros_like(cnt_ref)
    cnt_ref[0:1, 0:N_EXPERTS] = carry


def _mixout(tok, o_prompt, o_sample, w_out, x_p, x_s, g1, g2, mod_p, mod_s, router_w, router_b):
    d = tok.d
    v = o_prompt.shape[1]
    return pl.pallas_call(
        functools.partial(_mixout_kernel, tok.prompt_tiles),
        grid=(tok.tiles,),
        in_specs=tok.split_specs(v)
                 + [pl.BlockSpec((v, d), lambda i: (0, 0), pipeline_mode=pl.Buffered(1))]
                 + tok.split_specs(d) + [_const_spec((1, d)), _const_spec((1, d))]
                 + tok.mod_specs(2) + tok.mod_specs(3) + tok.mod_specs(4)
                 + [_const_spec((d, N_EXPERTS)), _const_spec((1, N_EXPERTS))],
        out_specs=tok.split_specs(d, out=True) + [tok.row_spec(d // 2), tok.row_spec(128), tok.row_spec(128),
                                        _const_spec((8, 128))],
        out_shape=tok.split_shapes(d, F32) + [jax.ShapeDtypeStruct((tok.n, d // 2), jnp.uint32),
                                              jax.ShapeDtypeStruct((tok.n, 128), I32),
                                              jax.ShapeDtypeStruct((tok.n, 128), F32),
                                              jax.ShapeDtypeStruct((8, 128), F32)],
        scratch_shapes=[pltpu.VMEM((1, N_EXPERTS), F32)],
        compiler_params=_params("arbitrary"),
        name="mix_out",
    )(o_prompt, o_sample, w_out, x_p, x_s, g1, g2, mod_p, mod_s, mod_p, mod_s, mod_p, mod_s, router_w, router_b)


def _expert_plan(ridx, counts, n_tiles):
    n = ridx.shape[0]
    expert = ridx[:, 0:2]
    rank = ridx[:, 2:4]
    cnt = counts[0, :N_EXPERTS].astype(I32)
    padded = ((cnt + EXPERT_TILE - 1) // EXPERT_TILE) * EXPERT_TILE
    ends = jnp.cumsum(padded)
    starts = ends - padded
    pos = starts[expert] + rank
    tile = jnp.arange(n_tiles, dtype=I32)
    tile_start = tile * EXPERT_TILE
    used = tile_start < ends[-1]
    last_used = jnp.maximum(ends[-1] // EXPERT_TILE - 1, 0)
    tile_expert = jnp.sum((tile_start[:, None] >= ends[None, :]).astype(I32), axis=1)
    tile_expert = jnp.minimum(jnp.where(used, tile_expert, tile_expert[last_used]), N_EXPERTS - 1)
    valid = jnp.clip(cnt[tile_expert] - (tile_start - starts[tile_expert]), 0, EXPERT_TILE)
    valid = jnp.where(used, valid, 0)
    return tile_expert, valid, jnp.minimum(tile, last_used), pos.reshape(n // TOKEN_TILE, 1, 2 * TOKEN_TILE)


def _all_rows(count, fn):
    def body(r, carry):
        fn(r)
        return carry
    lax.fori_loop(0, count, body, 0)


def _dispatch_kernel(pos_ref, h_hbm, xs_in, xs_hbm, sem):
    del xs_in
    base = pl.program_id(0) * TOKEN_TILE

    def row(r):
        return pltpu.make_async_copy(h_hbm.at[pl.ds(base + (r >> 1), 1)],
                                     xs_hbm.at[pl.ds(pos_ref[0, 0, r], 1)], sem.at[0])

    _all_rows(2 * TOKEN_TILE, lambda r: row(r).start())
    _all_rows(2 * TOKEN_TILE, lambda r: row(r).wait())


def _dispatch(h_packed, pos, n_rows):
    n, width = h_packed.shape
    return pl.pallas_call(
        _dispatch_kernel,
        grid=(n // TOKEN_TILE,),
        in_specs=[pl.BlockSpec((1, 1, 2 * TOKEN_TILE), lambda i: (i, 0, 0), memory_space=pltpu.SMEM),
                  pl.BlockSpec(memory_space=pl.ANY),
                  pl.BlockSpec(memory_space=pl.ANY)],
        out_specs=pl.BlockSpec(memory_space=pl.ANY),
        out_shape=jax.ShapeDtypeStruct((n_rows, width), jnp.uint32),
        scratch_shapes=[pltpu.SemaphoreType.DMA((1,))],
        input_output_aliases={2: 0},
        compiler_params=_params("arbitrary"),
        name="dispatch",
    )(pos, h_packed, jnp.zeros((n_rows, width), jnp.uint32))


def _expert_kernel(te_ref, nv_ref, tin_ref, x_ref, wg_ref, wu_ref, wd_ref, y_ref):
    del te_ref, tin_ref
    n = nv_ref[pl.program_id(0)]
    half = x_ref.shape[1]

    @pl.when(n > 0)
    def _():
        lo, hi = _unpack_pairs(x_ref[...])
        a = _dot(lo, wg_ref[0:half, :]) + _dot(hi, wg_ref[half:2 * half, :])
        u = _dot(lo, wu_ref[0:half, :]) + _dot(hi, wu_ref[half:2 * half, :])
        y_ref[...] = _dot((_silu(a) * u).astype(BF16), wd_ref[...])

    @pl.when(n == 0)
    def _():
        y_ref[...] = jnp.zeros_like(y_ref)


def _experts(xs, plan, wg, wu, wd):
    d, f = wg.shape[1], wg.shape[2]
    tile_expert, valid, tile_in = plan[:3]
    n_tiles = tile_expert.shape[0]
    grid_spec = pltpu.PrefetchScalarGridSpec(
        num_scalar_prefetch=3,
        grid=(n_tiles,),
        in_specs=[pl.BlockSpec((EXPERT_TILE, d // 2), lambda i, te, nv, tin: (tin[i], 0)),
                  pl.BlockSpec((None, d, f), lambda i, te, nv, tin: (te[i], 0, 0)),
                  pl.BlockSpec((None, d, f), lambda i, te, nv, tin: (te[i], 0, 0)),
                  pl.BlockSpec((None, f, d), lambda i, te, nv, tin: (te[i], 0, 0))],
        out_specs=pl.BlockSpec((EXPERT_TILE, d), lambda i, te, nv, tin: (i, 0)),
    )
    return pl.pallas_call(
        _expert_kernel,
        grid_spec=grid_spec,
        out_shape=jax.ShapeDtypeStruct((n_tiles * EXPERT_TILE, d), F32),
        compiler_params=_params("arbitrary"),
        name="experts",
    )(tile_expert, valid, tile_in, xs, wg, wu, wd)


def _combine(pos_ref, ys_hbm, rwt_ref, ybuf, sem):
    def row(r):
        return pltpu.make_async_copy(ys_hbm.at[pl.ds(pos_ref[0, 0, r], 1)],
                                     ybuf.at[r & 1, pl.ds(r >> 1, 1)], sem.at[0])

    _all_rows(2 * TOKEN_TILE, lambda r: row(r).start())
    _all_rows(2 * TOKEN_TILE, lambda r: row(r).wait())
    rw = rwt_ref[...]
    return rw[:, 0:1] * ybuf[0] + rw[:, 1:2] * ybuf[1]


def _moeout_next_kernel(prompt_tiles, pos_ref, ys_hbm, x_p, x_s, rwt_ref, g3_ref, gf_p, gf_s, gn_ref,
                        sh_p, sh_s, sc_p, sc_s, xo_p, xo_s, h_ref, ybuf, sem):
    is_sample = pl.program_id(0) >= prompt_tiles
    y = _combine(pos_ref, ys_hbm, rwt_ref, ybuf, sem)
    x = _pick(is_sample, x_p, x_s) + _pick(is_sample, gf_p, gf_s) * (_rms(y) * g3_ref[...])
    _store_split(is_sample, xo_p, xo_s, x)
    h = _rms(x) * gn_ref[...] * (1.0 + _pick(is_sample, sc_p, sc_s)) + _pick(is_sample, sh_p, sh_s)
    h_ref[...] = h.astype(BF16)


def _moeout_last_kernel(prompt_tiles, pos_ref, ys_hbm, x_p, x_s, rwt_ref, g3_ref, gf_p, gf_s, xo_p, xo_s,
                        ybuf, sem):
    is_sample = pl.program_id(0) >= prompt_tiles
    y = _combine(pos_ref, ys_hbm, rwt_ref, ybuf, sem)
    x = _pick(is_sample, x_p, x_s) + _pick(is_sample, gf_p, gf_s) * (_rms(y) * g3_ref[...])
    _store_split(is_sample, xo_p, xo_s, x)


def _moeout(tok, pos, ys, x_p, x_s, rwt, g3, mod_p, mod_s, nxt=None):
    d = tok.d
    ins = ([pl.BlockSpec((1, 1, 2 * TOKEN_TILE), lambda i: (i, 0, 0), memory_space=pltpu.SMEM),
            pl.BlockSpec(memory_space=pl.ANY)]
           + tok.split_specs(d) + [tok.row_spec(128), _const_spec((1, d))] + tok.mod_specs(5))
    args = [pos, ys, x_p, x_s, rwt, g3, mod_p, mod_s]
    scratch = [pltpu.VMEM((2, TOKEN_TILE, d), F32), pltpu.SemaphoreType.DMA((1,))]
    if nxt is None:
        return pl.pallas_call(
            functools.partial(_moeout_last_kernel, tok.prompt_tiles),
            grid=(tok.tiles,), in_specs=ins, out_specs=tok.split_specs(d, out=True),
            out_shape=tok.split_shapes(d, F32), scratch_shapes=scratch,
            compiler_params=_params("arbitrary"), name="moe_out_last",
        )(*args)
    gain_n, mod_pn, mod_sn = nxt
    ins = ins + [_const_spec((1, d))] + tok.mod_specs(0) + tok.mod_specs(1)
    args = args + [gain_n, mod_pn, mod_sn, mod_pn, mod_sn]
    return pl.pallas_call(
        functools.partial(_moeout_next_kernel, tok.prompt_tiles),
        grid=(tok.tiles,), in_specs=ins, out_specs=tok.split_specs(d, out=True) + [tok.row_spec(d)],
        out_shape=tok.split_shapes(d, F32) + [jax.ShapeDtypeStruct((tok.n, d), BF16)],
        scratch_shapes=scratch,
        compiler_params=_params("arbitrary"), name="moe_out_next",
    )(*args)


def kernel(x_prompt, x_sample, c_prompt, c_sample, state_ret, state_hgrn, ada_w, ada_b, norm_gains,
           ret_w_in, ret_w_out, hg_w_in, hg_w_out, hg_lower_bound, hg_out_norm, router_w, router_b,
           exp_w_gate, exp_w_up, exp_w_down):
    n_batch, seq, d = x_prompt.shape
    n_dec, steps, _ = x_sample.shape
    depth = ada_w.shape[0]
    assert depth == 2 and d % RET_DK == 0 and d % HG_DH == 0
    assert seq % RET_CHUNK == 0 and (n_dec * steps) % TOKEN_TILE == 0 and SAMPLE_ROWS % steps == 0
    ret_heads = d // RET_DK
    hg_heads = d // HG_DH
    n_prompt = n_batch * seq
    n_sample = n_dec * steps
    tok = _Tokens(n_batch, seq, n_sample, d)
    n = tok.n

    n_cond = n_batch + n_dec
    pad = (-n_cond) % 8
    c_all = jnp.concatenate([c_prompt, c_sample, jnp.zeros((pad, d), F32)], axis=0)
    mod = _ada(c_all, ada_w, ada_b)
    mods = []
    for l in range(depth):
        mod_p = mod[l, :n_batch].reshape(n_batch, 6, 1, d)
        mod_s = jnp.repeat(mod[l, n_batch:n_cond], steps, axis=0)
        mods.append((mod_p, mod_s))
    gain = lambda l, k: norm_gains[l, k].reshape(1, d)

    x_p = x_prompt.reshape(n_prompt, d)
    x_s = x_sample.reshape(n_sample, d)
    n_tiles = (2 * n + N_EXPERTS * (EXPERT_TILE - 1)) // EXPERT_TILE + 1
    rw = router_w.astype(F32)
    rb = router_b.astype(F32).reshape(1, N_EXPERTS)

    def channel_mixer(l, x_p, x_s, h_packed, ridx, rwt, counts, nxt):
        plan = _expert_plan(ridx, counts, n_tiles)
        pos = plan[3]
        xs = _dispatch(h_packed, pos, n_tiles * EXPERT_TILE)
        ys = _experts(xs, plan, exp_w_gate[l].astype(BF16), exp_w_up[l].astype(BF16), exp_w_down[l].astype(BF16))
        return _moeout(tok, pos, ys, x_p, x_s, rwt, gain(l, 3), *mods[l], nxt=nxt)

    h = _prenorm(tok, x_p, x_s, gain(0, 0), *mods[0])
    proj = _proj(h, ret_w_in[0])
    log_gamma = jnp.log(1.0 - jnp.exp2(-5.0 - jnp.arange(ret_heads, dtype=F32)))
    log_gamma = jnp.broadcast_to(log_gamma[:, None, None], (ret_heads, 1, 128))
    cos_p, sin_p = _rope_tables(jnp.arange(seq))
    cos_s, sin_s = _rope_tables(PAST_LEN + jnp.arange(steps))
    reps = SAMPLE_ROWS // steps
    cos_s, sin_s = jnp.tile(cos_s, (reps, 1)), jnp.tile(sin_s, (reps, 1))
    o_p, ret_prompt = _ret_prompt(proj, n_batch, seq, ret_heads, cos_p, sin_p, log_gamma)
    o_s, ret_sample = _ret_sample(proj, state_ret[0], n_prompt, steps, ret_heads, cos_s, sin_s, log_gamma)
    x_p, x_s, hp, ridx, rwt, counts = _mixout(tok, o_p, o_s, ret_w_out[0].astype(BF16), x_p, x_s,
                                              gain(0, 1), gain(0, 2), *mods[0], rw, rb)
    x_p, x_s, h = channel_mixer(0, x_p, x_s, hp, ridx, rwt, counts, (gain(1, 0),) + mods[1])

    sm = jax.nn.softmax(hg_lower_bound.astype(F32), axis=0)
    lb = (jnp.cumsum(sm, axis=0) - sm[0])[1].reshape(1, d)
    proj = _proj(h, hg_w_in[0])
    out_norm = hg_out_norm[0].reshape(1, d)
    o_p, hg_prompt = _hg_prompt(proj, n_batch, seq, hg_heads, lb, out_norm)
    o_s, hg_sample = _hg_sample(proj, state_hgrn[0], n_prompt, steps, hg_heads, lb, out_norm)
    x_p, x_s, hp, ridx, rwt, counts = _mixout(tok, o_p, o_s, hg_w_out[0].astype(BF16), x_p, x_s,
                                              gain(1, 1), gain(1, 2), *mods[1], rw, rb)
    x_p, x_s = channel_mixer(1, x_p, x_s, hp, ridx, rwt, counts, None)

    return (x_p.reshape(n_batch, seq, d), x_s.reshape(n_dec, steps, d),
            ret_prompt[None], hg_prompt[None], ret_sample[None], hg_sample[None])
```

```python
import functools
import math

import numpy as np
import jax
import jax.numpy as jnp
from jax import lax
from jax.experimental import pallas as pl
from jax.experimental.pallas import tpu as pltpu

F32, BF16, I32 = jnp.float32, jnp.bfloat16, jnp.int32

EPS = 1e-6
ROPE_BASE = 10000.0
PAST_LEN = 16384
RET_DK = 256
RET_DV = 512
HG_DH = 128
N_EXPERTS = 16
GROUP_SHIFT = 2
N_GROUPS = N_EXPERTS >> GROUP_SHIFT

TOKEN_TILE = 256
PROJ_TILE = 512
PROJ_COLS = 1024
EXPERT_TILE = 384
RET_CHUNK = 256
HG_CHUNK = 128
HG_LEAF = 16
HG_LEVELS = (64, 32, 16)
HG_HEADS_PER_STEP = 4
SAMPLE_ROWS = 16
ROW_DMA_UNROLL = 8
ROW_CHUNK = 16
ROW_CHUNK_UNROLL = 4
VMEM_LIMIT = 56 * 1024 * 1024


def _params(*sem):
    return pltpu.CompilerParams(dimension_semantics=sem, vmem_limit_bytes=VMEM_LIMIT)


def _rms(x):
    return x * lax.rsqrt(jnp.mean(x * x, axis=-1, keepdims=True) + EPS)


def _silu(x):
    return x * jax.nn.sigmoid(x)


def _dot(a, b):
    return jnp.dot(a, b, preferred_element_type=F32)


def _dot_nt(a, b):
    return lax.dot_general(a, b, (((1,), (1,)), ((), ())), preferred_element_type=F32)


def _dot_tn(a, b):
    return lax.dot_general(a, b, (((0,), (0,)), ((), ())), preferred_element_type=F32)


def _split2(x):
    hi = x.astype(BF16)
    lo = (x - hi.astype(F32)).astype(BF16)
    return hi, lo


def _split3(x):
    hi = x.astype(BF16)
    r = x - hi.astype(F32)
    mid = r.astype(BF16)
    lo = (r - mid.astype(F32)).astype(BF16)
    return hi, mid, lo


def _ada_kernel(c_ref, w_ref, b_ref, o_ref):
    s = _silu(c_ref[...]).astype(BF16)
    o_ref[...] = _dot(s, w_ref[...].astype(BF16)) + b_ref[...]


def _ada(c_all, ada_w, ada_b):
    depth, d, d6 = ada_w.shape
    m = c_all.shape[0]
    tn = min(1024, d6)
    return pl.pallas_call(
        _ada_kernel,
        grid=(depth, d6 // tn),
        in_specs=[pl.BlockSpec((m, d), lambda l, j: (0, 0)),
                  pl.BlockSpec((None, d, tn), lambda l, j: (l, 0, j)),
                  pl.BlockSpec((None, 1, tn), lambda l, j: (l, 0, j))],
        out_specs=pl.BlockSpec((None, m, tn), lambda l, j: (l, 0, j)),
        out_shape=jax.ShapeDtypeStruct((depth, m, d6), F32),
        compiler_params=_params("arbitrary", "arbitrary"),
        name="ada_mod",
    )(c_all, ada_w, ada_b.reshape(depth, 1, d6))


class _Tokens:
    def __init__(self, n_prompt_batch, seq, n_sample, d):
        self.d = d
        self.n_prompt = n_prompt_batch * seq
        self.n = self.n_prompt + n_sample
        self.prompt_tiles = self.n_prompt // TOKEN_TILE
        self.tiles_per_batch = seq // TOKEN_TILE
        self.n_batch = n_prompt_batch
        self.tiles = self.n // TOKEN_TILE

    def mod_specs(self, comp):
        d, tpb, nb, npt = self.d, self.tiles_per_batch, self.n_batch, self.prompt_tiles
        return [pl.BlockSpec((None, None, 1, d), lambda i: (jnp.minimum(i // tpb, nb - 1), comp, 0, 0)),
                pl.BlockSpec((TOKEN_TILE, d), lambda i: (jnp.maximum(i - npt, 0), comp),
                             pipeline_mode=pl.Buffered(1))]

    def row_spec(self, width):
        return pl.BlockSpec((TOKEN_TILE, width), lambda i: (i, 0))

    def split_specs(self, width, out=False):
        npt = self.prompt_tiles
        mode = {} if out else dict(pipeline_mode=pl.Buffered(1))
        return [pl.BlockSpec((TOKEN_TILE, width), lambda i: (jnp.minimum(i, npt - 1), 0)),
                pl.BlockSpec((TOKEN_TILE, width), lambda i: (jnp.maximum(i - npt, 0), 0), **mode)]

    def split_shapes(self, width, dtype):
        return [jax.ShapeDtypeStruct((self.n_prompt, width), dtype),
                jax.ShapeDtypeStruct((self.n - self.n_prompt, width), dtype)]


def _const_spec(shape):
    return pl.BlockSpec(shape, lambda i: (0,) * len(shape))


def _pick(is_sample, p_ref, s_ref):
    return jnp.where(is_sample, s_ref[...], p_ref[...])


def _store_split(is_sample, p_ref, s_ref, val):
    @pl.when(is_sample)
    def _():
        s_ref[...] = val

    @pl.when(jnp.logical_not(is_sample))
    def _():
        p_ref[...] = val


def _prenorm_kernel(prompt_tiles, x_p, x_s, g_ref, sh_p, sh_s, sc_p, sc_s, h_ref):
    is_sample = pl.program_id(0) >= prompt_tiles
    x = _pick(is_sample, x_p, x_s)
    h = _rms(x) * g_ref[...] * (1.0 + _pick(is_sample, sc_p, sc_s)) + _pick(is_sample, sh_p, sh_s)
    h_ref[...] = h.astype(BF16)


def _prenorm(tok, x_p, x_s, gain, mod_p, mod_s):
    d = tok.d
    return pl.pallas_call(
        functools.partial(_prenorm_kernel, tok.prompt_tiles),
        grid=(tok.tiles,),
        in_specs=tok.split_specs(d) + [_const_spec((1, d))] + tok.mod_specs(0) + tok.mod_specs(1),
        out_specs=tok.row_spec(d),
        out_shape=jax.ShapeDtypeStruct((tok.n, d), BF16),
        compiler_params=_params("arbitrary"),
        name="prenorm",
    )(x_p, x_s, gain, mod_p, mod_s, mod_p, mod_s)


def _proj_kernel(h_ref, w_ref, o_ref, wb_ref):
    @pl.when(pl.program_id(1) == 0)
    def _():
        wb_ref[...] = w_ref[...].astype(BF16)

    o_ref[...] = _dot(h_ref[...], wb_ref[...]).astype(BF16)


def _proj(h, w):
    n, d = h.shape
    p = w.shape[1]
    tm = PROJ_TILE if n % PROJ_TILE == 0 else TOKEN_TILE
    tn = min(PROJ_COLS, p)
    return pl.pallas_call(
        _proj_kernel,
        grid=(p // tn, n // tm),
        in_specs=[pl.BlockSpec((tm, d), lambda j, i: (i, 0)),
                  pl.BlockSpec((d, tn), lambda j, i: (0, j))],
        out_specs=pl.BlockSpec((tm, tn), lambda j, i: (i, j)),
        out_shape=jax.ShapeDtypeStruct((n, p), BF16),
        scratch_shapes=[pltpu.VMEM((d, tn), BF16)],
        compiler_params=_params("arbitrary", "arbitrary"),
        name="in_proj",
    )(h, w)


def _rope_tables(pos):
    half = RET_DK // 2
    theta = 1.0 / (ROPE_BASE ** jnp.linspace(0.0, 1.0, half, dtype=F32))
    ang = pos.astype(F32)[:, None] * theta[None, :]
    cos, sin = jnp.cos(ang), jnp.sin(ang)
    return (jnp.repeat(cos, 2, axis=1),
            jnp.stack([-sin, sin], axis=-1).reshape(pos.shape[0], RET_DK))


def _rot(x, cos, sin_signed):
    lane = lax.broadcasted_iota(I32, x.shape, 1)
    width = x.shape[1]
    nbr = jnp.where((lane & 1) == 0, pltpu.roll(x, width - 1, 1), pltpu.roll(x, 1, 1))
    return x * cos + nbr * sin_signed


def _gate_out(o, g_ref_val):
    return (_rms(o) * _silu(g_ref_val.astype(F32))).astype(BF16)


def _ret_prompt_kernel(q_ref, k_ref, v_ref, g_ref, cos_ref, sin_ref, lg_ref, o_ref, s_out_ref, s_ref):
    c = RET_CHUNK
    seq = q_ref.shape[0]
    lg = lg_ref[0:1, 0:1]
    ti = lax.broadcasted_iota(I32, (c, c), 0)
    si = lax.broadcasted_iota(I32, (c, c), 1)
    decay = jnp.where(ti >= si, jnp.exp((ti - si).astype(F32) * lg), 0.0)
    tcol = lax.broadcasted_iota(I32, (c, 1), 0).astype(F32)
    dq = jnp.exp((tcol + 1.0) * lg)
    dk = jnp.exp((float(c - 1) - tcol) * lg)
    dchunk = jnp.exp(float(c) * lg)
    s_ref[...] = jnp.zeros_like(s_ref)

    def body(ci, carry):
        r0 = pl.multiple_of(ci * c, c)
        rows = pl.ds(r0, c)
        cos, sin = cos_ref[rows, :], sin_ref[rows, :]
        q = _rot(q_ref[rows, :].astype(F32), cos, sin)
        k = _rot(k_ref[rows, :].astype(F32), cos, sin) * (RET_DK ** -0.5)
        v = v_ref[rows, :]
        s = s_ref[...]
        scores = _dot_nt(q.astype(BF16), k.astype(BF16)) * decay
        o = _dot(scores.astype(BF16), v) + _dot((q * dq).astype(BF16), s.astype(BF16))
        s_ref[...] = dchunk * s + _dot_tn((k * dk).astype(BF16), v)
        o_ref[rows, :] = _gate_out(o, g_ref[rows, :])
        return carry

    lax.fori_loop(0, seq // c, body, 0)
    s_out_ref[...] = s_ref[...]


def _ret_prompt(proj, n_batch, seq, heads, cos, sin, log_gamma):
    qk_blocks = heads
    return pl.pallas_call(
        _ret_prompt_kernel,
        grid=(n_batch, heads),
        in_specs=[pl.BlockSpec((seq, RET_DK), lambda b, h: (b, h)),
                  pl.BlockSpec((seq, RET_DK), lambda b, h: (b, qk_blocks + h)),
                  pl.BlockSpec((seq, RET_DV), lambda b, h: (b, heads + h)),
                  pl.BlockSpec((seq, RET_DV), lambda b, h: (b, 2 * heads + h)),
                  pl.BlockSpec((seq, RET_DK), lambda b, h: (0, 0)),
                  pl.BlockSpec((seq, RET_DK), lambda b, h: (0, 0)),
                  pl.BlockSpec((None, 1, 128), lambda b, h: (h, 0, 0))],
        out_specs=[pl.BlockSpec((seq, RET_DV), lambda b, h: (b, h)),
                   pl.BlockSpec((None, None, RET_DK, RET_DV), lambda b, h: (b, h, 0, 0))],
        out_shape=[jax.ShapeDtypeStruct((n_batch * seq, heads * RET_DV), BF16),
                   jax.ShapeDtypeStruct((n_batch, heads, RET_DK, RET_DV), F32)],
        scratch_shapes=[pltpu.VMEM((RET_DK, RET_DV), F32)],
        compiler_params=_params("arbitrary", "arbitrary"),
        name="ret_prompt",
    )(proj, proj, proj, proj, cos, sin, log_gamma)


def _ret_sample_kernel(steps, q_ref, k_ref, v_ref, g_ref, cos_ref, sin_ref, lg_ref, s_in_ref, o_ref, s_out_ref):
    rows = q_ref.shape[0]
    shift = int(math.log2(steps))
    lg = lg_ref[0:1, 0:1]
    cos, sin = cos_ref[...], sin_ref[...]
    q = _rot(q_ref[...].astype(F32), cos, sin)
    k = _rot(k_ref[...].astype(F32), cos, sin) * (RET_DK ** -0.5)
    v = v_ref[...]
    ri = lax.broadcasted_iota(I32, (rows, rows), 0)
    ci = lax.broadcasted_iota(I32, (rows, rows), 1)
    pair = ((ri >> shift) == (ci >> shift)) & (ri >= ci)
    decay = jnp.where(pair, jnp.exp((ri - ci).astype(F32) * lg), 0.0)
    o = _dot((_dot_nt(q.astype(BF16), k.astype(BF16)) * decay).astype(BF16), v)
    rid = lax.broadcasted_iota(I32, (rows, 1), 0)
    step = (rid & (steps - 1)).astype(F32)
    row_batch = rid >> shift
    qd = (q * jnp.exp((step + 1.0) * lg)).astype(BF16)
    kd = k * jnp.exp((float(steps - 1) - step) * lg)
    dall = jnp.exp(float(steps) * lg)
    for j in range(rows // steps):
        s = s_in_ref[j]
        o = o + jnp.where(row_batch == j, _dot(qd, s.astype(BF16)), 0.0)
        kj = jnp.where(row_batch == j, kd, 0.0).astype(BF16)
        s_out_ref[j] = dall * s + _dot_tn(kj, v)
    o_ref[...] = _gate_out(o, g_ref[...])


def _ret_sample(proj, state, n_prompt, steps, heads, cos, sin, log_gamma):
    n_sample_batch = state.shape[0]
    per = SAMPLE_ROWS // steps
    base = n_prompt // SAMPLE_ROWS
    return pl.pallas_call(
        functools.partial(_ret_sample_kernel, steps),
        grid=(n_sample_batch // per, heads),
        in_specs=[pl.BlockSpec((SAMPLE_ROWS, RET_DK), lambda b, h: (base + b, h)),
                  pl.BlockSpec((SAMPLE_ROWS, RET_DK), lambda b, h: (base + b, heads + h)),
                  pl.BlockSpec((SAMPLE_ROWS, RET_DV), lambda b, h: (base + b, heads + h)),
                  pl.BlockSpec((SAMPLE_ROWS, RET_DV), lambda b, h: (base + b, 2 * heads + h)),
                  pl.BlockSpec((SAMPLE_ROWS, RET_DK), lambda b, h: (0, 0)),
                  pl.BlockSpec((SAMPLE_ROWS, RET_DK), lambda b, h: (0, 0)),
                  pl.BlockSpec((None, 1, 128), lambda b, h: (h, 0, 0)),
                  pl.BlockSpec((per, None, RET_DK, RET_DV), lambda b, h: (b, h, 0, 0))],
        out_specs=[pl.BlockSpec((SAMPLE_ROWS, RET_DV), lambda b, h: (b, h)),
                   pl.BlockSpec((per, None, RET_DK, RET_DV), lambda b, h: (b, h, 0, 0))],
        out_shape=[jax.ShapeDtypeStruct((n_sample_batch * steps, heads * RET_DV), BF16),
                   jax.ShapeDtypeStruct(state.shape, F32)],
        compiler_params=_params("arbitrary", "arbitrary"),
        name="ret_sample",
    )(proj, proj, proj, proj, cos, sin, log_gamma, state)


def _hg_prefix_matrix():
    c = HG_CHUNK
    t = np.arange(c)[:, None]
    s = np.arange(c)[None, :]
    le = (s <= t).astype(np.float32)
    mats = [le, le * ((s // HG_LEAF) == (t // HG_LEAF))]
    for half in HG_LEVELS:
        mid = (t // (2 * half)) * (2 * half) + half - 1
        mats.append(le - (s <= mid).astype(np.float32))
    return jnp.asarray(np.concatenate(mats, axis=0), dtype=BF16)


def _hg_gates(q, f, lb):
    forget = lb + (1.0 - lb) * jax.nn.sigmoid(f)
    return _silu(q), 1.0 - forget, jnp.log(forget)


def _hg_prompt_kernel(q_ref, f_ref, i_ref, g_ref, lb_ref, gn_ref, pm_ref, o_ref, s_out_ref, s_ref):
    c = HG_CHUNK
    seq = q_ref.shape[0]
    n_heads = q_ref.shape[1] // HG_DH
    ti = lax.broadcasted_iota(I32, (c, c), 0)
    si = lax.broadcasted_iota(I32, (c, c), 1)
    leaf_shift = int(math.log2(HG_LEAF))
    mask_leaf = (si <= ti) & ((ti >> leaf_shift) == (si >> leaf_shift))
    level_masks = []
    for half in HG_LEVELS:
        sh = int(math.log2(2 * half))
        level_masks.append(((ti >> sh) == (si >> sh)) & ((ti & (2 * half - 1)) >= half) & ((si & (2 * half - 1)) < half))
    eye = ti == si
    s_ref[...] = jnp.zeros_like(s_ref)

    def one_head(hh, rows):
        cols = slice(hh * HG_DH, (hh + 1) * HG_DH)
        qh, kk, lf = _hg_gates(q_ref[rows, cols].astype(F32), f_ref[rows, cols].astype(F32), lb_ref[:, cols])
        v = i_ref[rows, cols]
        lf2 = jnp.concatenate(_split2(lf), axis=1)

        def sums(part):
            r = _dot(pm_ref[part * c:(part + 1) * c, :], lf2)
            return r[:, :HG_DH] + r[:, HG_DH:]

        d_leaf = sums(1)
        a = jnp.where(mask_leaf,
                      _dot_nt((qh * jnp.exp(d_leaf)).astype(BF16), (kk * jnp.exp(-d_leaf)).astype(BF16)), 0.0)
        for lvl, mask in enumerate(level_masks):
            w = jnp.exp(-jnp.abs(sums(2 + lvl)))
            a = a + jnp.where(mask, _dot_nt((qh * w).astype(BF16), (kk * w).astype(BF16)), 0.0)
        b = sums(0)
        s = s_ref[hh]
        o = _dot(a.astype(BF16), v) + _dot((qh * jnp.exp(b)).astype(BF16), s.astype(BF16))
        b_last = b[c - 1:c, :]
        col = jnp.sum(jnp.where(eye, jnp.exp(b_last), 0.0), axis=1, keepdims=True)
        s_ref[hh] = col * s + _dot_tn((kk * jnp.exp(b_last - b)).astype(BF16), v)
        o_ref[rows, cols] = (_rms(o) * gn_ref[:, cols] * _silu(g_ref[rows, cols].astype(F32))).astype(BF16)

    def body(ci, carry):
        rows = pl.ds(pl.multiple_of(ci * c, c), c)
        for hh in range(n_heads):
            one_head(hh, rows)
        return carry

    lax.fori_loop(0, seq // c, body, 0)
    s_out_ref[...] = s_ref[...]


def _hg_prompt(proj, n_batch, seq, heads, lb, out_norm):
    pm = _hg_prefix_matrix()
    hp = HG_HEADS_PER_STEP
    width = hp * HG_DH
    groups = heads // hp
    col = lambda part: (lambda b, h: (b, part * groups + h))
    return pl.pallas_call(
        _hg_prompt_kernel,
        grid=(n_batch, groups),
        in_specs=[pl.BlockSpec((seq, width), col(0)),
                  pl.BlockSpec((seq, width), col(1)),
                  pl.BlockSpec((seq, width), col(2)),
                  pl.BlockSpec((seq, width), col(3)),
                  pl.BlockSpec((1, width), lambda b, h: (0, h)),
                  pl.BlockSpec((1, width), lambda b, h: (0, h)),
                  pl.BlockSpec(pm.shape, lambda b, h: (0, 0))],
        out_specs=[pl.BlockSpec((seq, width), lambda b, h: (b, h)),
                   pl.BlockSpec((None, hp, HG_DH, HG_DH), lambda b, h: (b, h, 0, 0))],
        out_shape=[jax.ShapeDtypeStruct((n_batch * seq, heads * HG_DH), BF16),
                   jax.ShapeDtypeStruct((n_batch, heads, HG_DH, HG_DH), F32)],
        scratch_shapes=[pltpu.VMEM((hp, HG_DH, HG_DH), F32)],
        compiler_params=_params("arbitrary", "arbitrary"),
        name="hg_prompt",
    )(proj, proj, proj, proj, lb, out_norm, pm)


def _hg_sample_kernel(steps, heads, q_ref, f_ref, i_ref, g_ref, lb_ref, gn_ref, s_in_ref, o_ref, s_out_ref):
    rows = q_ref.shape[0]
    shift = int(math.log2(steps))
    ri = lax.broadcasted_iota(I32, (rows, rows), 0)
    ci = lax.broadcasted_iota(I32, (rows, rows), 1)
    prefix = (((ri >> shift) == (ci >> shift)) & (ci <= ri)).astype(BF16)
    rid = lax.broadcasted_iota(I32, (rows, 1), 0)
    step = rid & (steps - 1)
    row_batch = rid >> shift
    ki = lax.broadcasted_iota(I32, (HG_DH, HG_DH), 0)
    vi = lax.broadcasted_iota(I32, (HG_DH, HG_DH), 1)
    eye = ki == vi

    def head(h):
        cols = pl.ds(pl.multiple_of(h * HG_DH, HG_DH), HG_DH)
        qh, kk, lf = _hg_gates(q_ref[:, cols].astype(F32), f_ref[:, cols].astype(F32), lb_ref[:, cols])
        v = i_ref[:, cols]
        vf = v.astype(F32)
        hi, lo = _split2(lf)
        b = _dot(prefix, jnp.concatenate([hi, lo], axis=1))
        b = b[:, :HG_DH] + b[:, HG_DH:]
        o = jnp.zeros((rows, HG_DH), F32)
        for dist in range(steps):
            if dist == 0:
                k_s, b_s, v_s = kk, b, vf
            else:
                k_s, b_s, v_s = (pltpu.roll(kk, dist, 0), pltpu.roll(b, dist, 0), pltpu.roll(vf, dist, 0))
            w = jnp.sum(qh * k_s * jnp.exp(jnp.minimum(b - b_s, 0.0)), axis=1, keepdims=True)
            o = o + jnp.where(step >= dist, w, 0.0) * v_s
        qd = (qh * jnp.exp(b)).astype(BF16)
        for j in range(rows // steps):
            s = s_in_ref[j, h]
            o = o + jnp.where(row_batch == j, _dot(qd, s.astype(BF16)), 0.0)
            b_last = b[(j + 1) * steps - 1:(j + 1) * steps, :]
            col = jnp.sum(jnp.where(eye, jnp.exp(b_last), 0.0), axis=1, keepdims=True)
            kj = jnp.where(row_batch == j, kk * jnp.exp(jnp.minimum(b_last - b, 0.0)), 0.0).astype(BF16)
            s_out_ref[j, h] = col * s + _dot_tn(kj, v)
        o_ref[:, cols] = (_rms(o) * gn_ref[:, cols] * _silu(g_ref[:, cols].astype(F32))).astype(BF16)

    def group(gi, carry):
        for hh in range(HG_HEADS_PER_STEP):
            head(gi * HG_HEADS_PER_STEP + hh)
        return carry

    lax.fori_loop(0, heads // HG_HEADS_PER_STEP, group, 0)


def _hg_sample(proj, state, n_prompt, steps, heads, lb, out_norm):
    n_sample_batch = state.shape[0]
    d = heads * HG_DH
    per = SAMPLE_ROWS // steps
    base = n_prompt // SAMPLE_ROWS
    col = lambda part: (lambda b: (base + b, part))
    return pl.pallas_call(
        functools.partial(_hg_sample_kernel, steps, heads),
        grid=(n_sample_batch // per,),
        in_specs=[pl.BlockSpec((SAMPLE_ROWS, d), col(0)),
                  pl.BlockSpec((SAMPLE_ROWS, d), col(1)),
                  pl.BlockSpec((SAMPLE_ROWS, d), col(2)),
                  pl.BlockSpec((SAMPLE_ROWS, d), col(3)),
                  pl.BlockSpec((1, d), lambda b: (0, 0)),
                  pl.BlockSpec((1, d), lambda b: (0, 0)),
                  pl.BlockSpec((per, heads, HG_DH, HG_DH), lambda b: (b, 0, 0, 0))],
        out_specs=[pl.BlockSpec((SAMPLE_ROWS, d), lambda b: (b, 0)),
                   pl.BlockSpec((per, heads, HG_DH, HG_DH), lambda b: (b, 0, 0, 0))],
        out_shape=[jax.ShapeDtypeStruct((n_sample_batch * steps, d), BF16),
                   jax.ShapeDtypeStruct(state.shape, F32)],
        compiler_params=_params("arbitrary"),
        name="hg_sample",
    )(proj, proj, proj, proj, lb, out_norm, state)


def _first_max(vals, lane, width):
    m = jnp.max(vals, axis=1, keepdims=True)
    idx = jnp.min(jnp.where(vals == m, lane, width), axis=1, keepdims=True)
    return m, idx


def _route(logits, bias):
    neg = -jnp.inf
    lane = lax.broadcasted_iota(I32, logits.shape, 1)
    group = lane >> GROUP_SHIFT
    scores = jax.nn.sigmoid(logits)
    biased = scores + bias
    best = sel = None
    for gi in range(N_GROUPS):
        vals = jnp.where(group == gi, biased, neg)
        m1, i1 = _first_max(vals, lane, N_EXPERTS)
        m2 = jnp.max(jnp.where(lane == i1, neg, vals), axis=1, keepdims=True)
        total = m1 + m2
        if gi == 0:
            best, sel = total, jnp.zeros_like(i1)
        else:
            better = total > best
            sel = jnp.where(better, gi, sel)
            best = jnp.where(better, total, best)
    vals = jnp.where(group == sel, biased, neg)
    _, e1 = _first_max(vals, lane, N_EXPERTS)
    _, e2 = _first_max(jnp.where(lane == e1, neg, vals), lane, N_EXPERTS)
    w1 = jnp.sum(jnp.where(lane == e1, scores, 0.0), axis=1, keepdims=True)
    w2 = jnp.sum(jnp.where(lane == e2, scores, 0.0), axis=1, keepdims=True)
    tot = w1 + w2
    return e1, e2, w1 / tot, w2 / tot


def _mixout_kernel(prompt_tiles, o_p, o_s, w_ref, x_p, x_s, g1_ref, g2_ref, gm_p, gm_s, sh_p, sh_s, sc_p, sc_s,
                   rw_ref, rb_ref, xo_p, xo_s, h_ref, ridx_ref, rwt_ref, cnt_ref, carry_ref, y_ref):
    i = pl.program_id(0)
    tm = x_p.shape[0]

    @pl.when(i == 0)
    def _():
        carry_ref[...] = jnp.zeros_like(carry_ref)

    def tile(sample):
        o_ref, x_ref, xo_ref = (o_s, x_s, xo_s) if sample else (o_p, x_p, xo_p)
        y_ref[...] = _dot(o_ref[...], w_ref[...])

        def chunk(rows):
            x = x_ref[rows, :] + _mod_rows(sample, gm_p, gm_s, rows) * (_rms(y_ref[rows, :]) * g1_ref[...])
            xo_ref[rows, :] = x
            h_ref[rows, :] = (_rms(x) * g2_ref[...] * (1.0 + _mod_rows(sample, sc_p, sc_s, rows))
                              + _mod_rows(sample, sh_p, sh_s, rows))

        _row_chunks(tm, chunk)

    _by_group(i >= prompt_tiles, tile)

    h1, h2 = _split2(h_ref[...])
    w1, w2 = _split2(rw_ref[...])
    first = _dot(h1, jnp.concatenate([w1, w2], axis=1))
    logits = first[:, :N_EXPERTS] + (first[:, N_EXPERTS:] + _dot(h2, w1))
    e1, e2, p1, p2 = _route(logits, rb_ref[...])

    lane = lax.broadcasted_iota(I32, (tm, N_EXPERTS), 1)
    hot1, hot2 = lane == e1, lane == e2
    onehot = (hot1 | hot2).astype(BF16)
    ti = lax.broadcasted_iota(I32, (tm, tm), 0)
    si = lax.broadcasted_iota(I32, (tm, tm), 1)
    before = _dot((si < ti).astype(BF16), onehot) + carry_ref[...]
    r1 = jnp.sum(jnp.where(hot1, before, 0.0), axis=1, keepdims=True).astype(I32)
    r2 = jnp.sum(jnp.where(hot2, before, 0.0), axis=1, keepdims=True).astype(I32)
    carry = carry_ref[...] + jnp.sum(onehot.astype(F32), axis=0, keepdims=True)
    carry_ref[...] = carry

    wide = lax.broadcasted_iota(I32, (tm, 128), 1)
    ridx_ref[...] = jnp.where(wide == 0, e1, jnp.where(wide == 1, e2, jnp.where(wide == 2, r1, r2)))
    rwt_ref[...] = jnp.where(wide == 0, p1, p2)
    cnt_ref[...] = jnp.zeros_like(cnt_ref)
    cnt_ref[0:1, 0:N_EXPERTS] = carry


def _mixout(tok, o_prompt, o_sample, w_out, x_p, x_s, g1, g2, mod_p, mod_s, router_w, router_b):
    d = tok.d
    v = o_prompt.shape[1]
    return pl.pallas_call(
        functools.partial(_mixout_kernel, tok.prompt_tiles),
        grid=(tok.tiles,),
        in_specs=tok.split_specs(v)
                 + [pl.BlockSpec((v, d), lambda i: (0, 0), pipeline_mode=pl.Buffered(1))]
                 + tok.split_specs(d) + [_const_spec((1, d)), _const_spec((1, d))]
                 + tok.mod_specs(2) + tok.mod_specs(3) + tok.mod_specs(4)
                 + [_const_spec((d, N_EXPERTS)), _const_spec((1, N_EXPERTS))],
        out_specs=tok.split_specs(d, out=True) + [tok.row_spec(d), tok.row_spec(128), tok.row_spec(128),
                                        _const_spec((8, 128))],
        out_shape=tok.split_shapes(d, F32) + [jax.ShapeDtypeStruct((tok.n, d), F32),
                                              jax.ShapeDtypeStruct((tok.n, 128), I32),
                                              jax.ShapeDtypeStruct((tok.n, 128), F32),
                                              jax.ShapeDtypeStruct((8, 128), F32)],
        scratch_shapes=[pltpu.VMEM((1, N_EXPERTS), F32), pltpu.VMEM((TOKEN_TILE, d), F32)],
        compiler_params=_params("arbitrary"),
        name="mix_out",
    )(o_prompt, o_sample, w_out, x_p, x_s, g1, g2, mod_p, mod_s, mod_p, mod_s, mod_p, mod_s, router_w, router_b)


def _expert_plan(ridx, counts, n_tiles):
    n = ridx.shape[0]
    expert = ridx[:, 0:2]
    rank = ridx[:, 2:4]
    cnt = counts[0, :N_EXPERTS].astype(I32)
    padded = ((cnt + EXPERT_TILE - 1) // EXPERT_TILE) * EXPERT_TILE
    ends = jnp.cumsum(padded)
    starts = ends - padded
    pos = starts[expert] + rank
    tile = jnp.arange(n_tiles, dtype=I32)
    tile_start = tile * EXPERT_TILE
    used = tile_start < ends[-1]
    last_used = jnp.maximum(ends[-1] // EXPERT_TILE - 1, 0)
    tile_expert = jnp.sum((tile_start[:, None] >= ends[None, :]).astype(I32), axis=1)
    tile_expert = jnp.minimum(jnp.where(used, tile_expert, tile_expert[last_used]), N_EXPERTS - 1)
    valid = jnp.clip(cnt[tile_expert] - (tile_start - starts[tile_expert]), 0, EXPERT_TILE)
    valid = jnp.where(used, valid, 0)
    pos = pos.reshape(n // TOKEN_TILE, TOKEN_TILE, 2).transpose(0, 2, 1)
    return tile_expert, valid, jnp.minimum(tile, last_used), pos


def _row_copies(count, copy, whole):
    def body(r, carry):
        copy(r).start()
        return carry
    lax.fori_loop(0, count, body, 0, unroll=ROW_DMA_UNROLL)
    whole.wait()


def _dispatch_kernel(pos_ref, h_ref, xs_in, xs_hbm, sem):
    del xs_in
    rows = h_ref.shape[0]
    for choice in range(2):
        def row(r, choice=choice):
            return pltpu.make_async_copy(h_ref.at[pl.ds(r, 1)],
                                         xs_hbm.at[pl.ds(pos_ref[0, choice, r], 1)], sem.at[choice])
        whole = pltpu.make_async_copy(h_ref, xs_hbm.at[pl.ds(0, rows)], sem.at[choice])
        _row_copies(rows, row, whole)


def _dispatch(h, pos, n_rows):
    n, d = h.shape
    return pl.pallas_call(
        _dispatch_kernel,
        grid=(n // TOKEN_TILE,),
        in_specs=[pl.BlockSpec((1, 2, TOKEN_TILE), lambda i: (i, 0, 0), memory_space=pltpu.SMEM),
                  pl.BlockSpec((TOKEN_TILE, d), lambda i: (i, 0)),
                  pl.BlockSpec(memory_space=pl.ANY)],
        out_specs=pl.BlockSpec(memory_space=pl.ANY),
        out_shape=jax.ShapeDtypeStruct((n_rows, d), F32),
        scratch_shapes=[pltpu.SemaphoreType.DMA((2,))],
        input_output_aliases={2: 0},
        compiler_params=_params("arbitrary"),
        name="dispatch",
    )(pos, h, jnp.zeros((n_rows, d), F32))


def _expert_kernel(te_ref, nv_ref, tin_ref, x_ref, wg_ref, wu_ref, wd_ref, y_ref):
    del te_ref, tin_ref
    n = nv_ref[pl.program_id(0)]

    @pl.when(n > 0)
    def _():
        x = x_ref[...].astype(BF16)
        a = _dot(x, wg_ref[...])
        u = _dot(x, wu_ref[...])
        y_ref[...] = _dot((_silu(a) * u).astype(BF16), wd_ref[...])

    @pl.when(n == 0)
    def _():
        y_ref[...] = jnp.zeros_like(y_ref)


def _experts(xs, plan, layer, wg, wu, wd):
    d, f = wg.shape[2], wg.shape[3]
    tile_expert, valid, tile_in = plan[:3]
    n_tiles = tile_expert.shape[0]
    grid_spec = pltpu.PrefetchScalarGridSpec(
        num_scalar_prefetch=3,
        grid=(n_tiles,),
        in_specs=[pl.BlockSpec((EXPERT_TILE, d), lambda i, te, nv, tin: (tin[i], 0)),
                  pl.BlockSpec((None, None, d, f), lambda i, te, nv, tin: (layer, te[i], 0, 0)),
                  pl.BlockSpec((None, None, d, f), lambda i, te, nv, tin: (layer, te[i], 0, 0)),
                  pl.BlockSpec((None, None, f, d), lambda i, te, nv, tin: (layer, te[i], 0, 0))],
        out_specs=pl.BlockSpec((EXPERT_TILE, d), lambda i, te, nv, tin: (i, 0)),
    )
    return pl.pallas_call(
        _expert_kernel,
        grid_spec=grid_spec,
        out_shape=jax.ShapeDtypeStruct((n_tiles * EXPERT_TILE, d), F32),
        compiler_params=_params("arbitrary"),
        name="experts",
    )(tile_expert, valid, tile_in, xs, wg, wu, wd)


def _by_group(is_sample, body):
    @pl.when(is_sample)
    def _():
        body(True)

    @pl.when(jnp.logical_not(is_sample))
    def _():
        body(False)


def _row_chunks(n_rows, fn):
    def body(ci, carry):
        fn(pl.ds(pl.multiple_of(ci * ROW_CHUNK, ROW_CHUNK), ROW_CHUNK))
        return carry
    lax.fori_loop(0, n_rows // ROW_CHUNK, body, 0, unroll=ROW_CHUNK_UNROLL)


def _mod_rows(sample, p_ref, s_ref, rows):
    return s_ref[rows, :] if sample else p_ref[...]


def _moeout_kernel(prompt_tiles, with_next, pos_ref, ys_hbm, x_p, x_s, rwt_ref, g3_ref, gf_p, gf_s, *rest):
    if with_next:
        gn_ref, sh_p, sh_s, sc_p, sc_s, xo_p, xo_s, h_ref, ybuf, sem = rest
    else:
        xo_p, xo_s, ybuf, sem = rest
    n_rows = ybuf.shape[1]
    for choice in range(2):
        def row(r, choice=choice):
            return pltpu.make_async_copy(ys_hbm.at[pl.ds(pos_ref[0, choice, r], 1)],
                                         ybuf.at[choice, pl.ds(r, 1)], sem.at[choice])
        whole = pltpu.make_async_copy(ys_hbm.at[pl.ds(0, n_rows)], ybuf.at[choice], sem.at[choice])
        _row_copies(n_rows, row, whole)

    def tile(sample):
        x_ref, xo_ref = (x_s, xo_s) if sample else (x_p, xo_p)

        def chunk(rows):
            rw = rwt_ref[rows, :]
            y = rw[:, 0:1] * ybuf[0, rows, :] + rw[:, 1:2] * ybuf[1, rows, :]
            x = x_ref[rows, :] + _mod_rows(sample, gf_p, gf_s, rows) * (_rms(y) * g3_ref[...])
            xo_ref[rows, :] = x
            if with_next:
                h = (_rms(x) * gn_ref[...] * (1.0 + _mod_rows(sample, sc_p, sc_s, rows))
                     + _mod_rows(sample, sh_p, sh_s, rows))
                h_ref[rows, :] = h.astype(BF16)

        _row_chunks(n_rows, chunk)

    _by_group(pl.program_id(0) >= prompt_tiles, tile)


def _moeout(tok, pos, ys, x_p, x_s, rwt, g3, mod_p, mod_s, nxt=None):
    d = tok.d
    ins = ([pl.BlockSpec((1, 2, TOKEN_TILE), lambda i: (i, 0, 0), memory_space=pltpu.SMEM),
            pl.BlockSpec(memory_space=pl.ANY)]
           + tok.split_specs(d) + [tok.row_spec(128), _const_spec((1, d))] + tok.mod_specs(5))
    args = [pos, ys, x_p, x_s, rwt, g3, mod_p, mod_s]
    scratch = [pltpu.VMEM((2, TOKEN_TILE, d), F32), pltpu.SemaphoreType.DMA((2,))]
    if nxt is None:
        return pl.pallas_call(
            functools.partial(_moeout_kernel, tok.prompt_tiles, False),
            grid=(tok.tiles,), in_specs=ins, out_specs=tok.split_specs(d, out=True),
            out_shape=tok.split_shapes(d, F32), scratch_shapes=scratch,
            compiler_params=_params("arbitrary"), name="moe_out_last",
        )(*args)
    gain_n, mod_pn, mod_sn = nxt
    ins = ins + [_const_spec((1, d))] + tok.mod_specs(0) + tok.mod_specs(1)
    args = args + [gain_n, mod_pn, mod_sn, mod_pn, mod_sn]
    return pl.pallas_call(
        functools.partial(_moeout_kernel, tok.prompt_tiles, True),
        grid=(tok.tiles,), in_specs=ins, out_specs=tok.split_specs(d, out=True) + [tok.row_spec(d)],
        out_shape=tok.split_shapes(d, F32) + [jax.ShapeDtypeStruct((tok.n, d), BF16)],
        scratch_shapes=scratch,
        compiler_params=_params("arbitrary"), name="moe_out_next",
    )(*args)


def kernel(x_prompt, x_sample, c_prompt, c_sample, state_ret, state_hgrn, ada_w, ada_b, norm_gains,
           ret_w_in, ret_w_out, hg_w_in, hg_w_out, hg_lower_bound, hg_out_norm, router_w, router_b,
           exp_w_gate, exp_w_up, exp_w_down):
    n_batch, seq, d = x_prompt.shape
    n_dec, steps, _ = x_sample.shape
    depth = ada_w.shape[0]
    assert depth == 2 and d % RET_DK == 0 and d % HG_DH == 0
    assert seq % RET_CHUNK == 0 and (n_dec * steps) % TOKEN_TILE == 0 and SAMPLE_ROWS % steps == 0
    ret_heads = d // RET_DK
    hg_heads = d // HG_DH
    n_prompt = n_batch * seq
    n_sample = n_dec * steps
    tok = _Tokens(n_batch, seq, n_sample, d)
    n = tok.n

    n_cond = n_batch + n_dec
    pad = (-n_cond) % 8
    c_all = jnp.concatenate([c_prompt, c_sample, jnp.zeros((pad, d), F32)], axis=0)
    mod = _ada(c_all, ada_w, ada_b)
    mods = []
    for l in range(depth):
        mod_p = mod[l, :n_batch].reshape(n_batch, 6, 1, d)
        mod_s = jnp.repeat(mod[l, n_batch:n_cond], steps, axis=0)
        mods.append((mod_p, mod_s))
    gain = lambda l, k: norm_gains[l, k].reshape(1, d)

    x_p = x_prompt.reshape(n_prompt, d)
    x_s = x_sample.reshape(n_sample, d)
    n_tiles = (2 * n + N_EXPERTS * (EXPERT_TILE - 1)) // EXPERT_TILE + 1
    rw = router_w.astype(F32)
    rb = router_b.astype(F32).reshape(1, N_EXPERTS)

    wg, wu, wd = exp_w_gate.astype(BF16), exp_w_up.astype(BF16), exp_w_down.astype(BF16)

    def channel_mixer(l, x_p, x_s, h, ridx, rwt, counts, nxt):
        plan = _expert_plan(ridx, counts, n_tiles)
        pos = plan[3]
        xs = _dispatch(h, pos, n_tiles * EXPERT_TILE)
        ys = _experts(xs, plan, l, wg, wu, wd)
        return _moeout(tok, pos, ys, x_p, x_s, rwt, gain(l, 3), *mods[l], nxt=nxt)

    h = _prenorm(tok, x_p, x_s, gain(0, 0), *mods[0])
    proj = _proj(h, ret_w_in[0])
    log_gamma = jnp.log(1.0 - jnp.exp2(-5.0 - jnp.arange(ret_heads, dtype=F32)))
    log_gamma = jnp.broadcast_to(log_gamma[:, None, None], (ret_heads, 1, 128))
    cos_p, sin_p = _rope_tables(jnp.arange(seq))
    cos_s, sin_s = _rope_tables(PAST_LEN + jnp.arange(steps))
    reps = SAMPLE_ROWS // steps
    cos_s, sin_s = jnp.tile(cos_s, (reps, 1)), jnp.tile(sin_s, (reps, 1))
    o_p, ret_prompt = _ret_prompt(proj, n_batch, seq, ret_heads, cos_p, sin_p, log_gamma)
    o_s, ret_sample = _ret_sample(proj, state_ret[0], n_prompt, steps, ret_heads, cos_s, sin_s, log_gamma)
    x_p, x_s, hp, ridx, rwt, counts = _mixout(tok, o_p, o_s, ret_w_out[0].astype(BF16), x_p, x_s,
                                              gain(0, 1), gain(0, 2), *mods[0], rw, rb)
    x_p, x_s, h = channel_mixer(0, x_p, x_s, hp, ridx, rwt, counts, (gain(1, 0),) + mods[1])

    sm = jax.nn.softmax(hg_lower_bound.astype(F32), axis=0)
    lb = (jnp.cumsum(sm, axis=0) - sm[0])[1].reshape(1, d)
    proj = _proj(h, hg_w_in[0])
    out_norm = hg_out_norm[0].reshape(1, d)
    o_p, hg_prompt = _hg_prompt(proj, n_batch, seq, hg_heads, lb, out_norm)
    o_s, hg_sample = _hg_sample(proj, state_hgrn[0], n_prompt, steps, hg_heads, lb, out_norm)
    x_p, x_s, hp, ridx, rwt, counts = _mixout(tok, o_p, o_s, hg_w_out[0].astype(BF16), x_p, x_s,
                                              gain(1, 1), gain(1, 2), *mods[1], rw, rb)
    x_p, x_s = channel_mixer(1, x_p, x_s, hp, ridx, rwt, counts, None)

    return (x_p.reshape(n_batch, seq, d), x_s.reshape(n_dec, steps, d),
            ret_prompt[None], hg_prompt[None], ret_sample[None], hg_sample[None])
```

```python
import functools
import math

import numpy as np
import jax
import jax.numpy as jnp
from jax import lax
from jax.experimental import pallas as pl
from jax.experimental.pallas import tpu as pltpu

F32, BF16, I32 = jnp.float32, jnp.bfloat16, jnp.int32

EPS = 1e-6
ROPE_BASE = 10000.0
PAST_LEN = 16384
RET_DK = 256
RET_DV = 512
HG_DH = 128
N_EXPERTS = 16
GROUP_SHIFT = 2
N_GROUPS = N_EXPERTS >> GROUP_SHIFT

TOKEN_TILE = 256
PROJ_TILE = 512
PROJ_COLS = 1024
EXPERT_TILE = 384
RET_CHUNK = 256
HG_CHUNK = 128
HG_LEAF = 16
HG_LEVELS = (64, 32, 16)
HG_HEADS_PER_STEP = 4
SAMPLE_ROWS = 16
SUBLANES = 8
EXPERT_F_SPLIT = 2
RET_SAMPLE_HEADS = 2
ROW_CHUNK = 16
ROW_CHUNK_UNROLL = 4
VMEM_LIMIT = 56 * 1024 * 1024


def _params(*sem):
    return pltpu.CompilerParams(dimension_semantics=sem, vmem_limit_bytes=VMEM_LIMIT)


def _rms(x):
    return x * lax.rsqrt(jnp.mean(x * x, axis=-1, keepdims=True) + EPS)


def _silu(x):
    return x * jax.nn.sigmoid(x)


def _dot(a, b):
    return jnp.dot(a, b, preferred_element_type=F32)


def _dot_nt(a, b):
    return lax.dot_general(a, b, (((1,), (1,)), ((), ())), preferred_element_type=F32)


def _dot_tn(a, b):
    return lax.dot_general(a, b, (((0,), (0,)), ((), ())), preferred_element_type=F32)


def _split2(x):
    hi = x.astype(BF16)
    lo = (x - hi.astype(F32)).astype(BF16)
    return hi, lo


def _split3(x):
    hi = x.astype(BF16)
    r = x - hi.astype(F32)
    mid = r.astype(BF16)
    lo = (r - mid.astype(F32)).astype(BF16)
    return hi, mid, lo


def _ada_kernel(c_ref, w_ref, b_ref, o_ref):
    s = _silu(c_ref[...]).astype(BF16)
    o_ref[...] = _dot(s, w_ref[...].astype(BF16)) + b_ref[...]


def _ada(c_all, ada_w, ada_b):
    depth, d, d6 = ada_w.shape
    m = c_all.shape[0]
    tn = min(1024, d6)
    return pl.pallas_call(
        _ada_kernel,
        grid=(depth, d6 // tn),
        in_specs=[pl.BlockSpec((m, d), lambda l, j: (0, 0)),
                  pl.BlockSpec((None, d, tn), lambda l, j: (l, 0, j)),
                  pl.BlockSpec((None, 1, tn), lambda l, j: (l, 0, j))],
        out_specs=pl.BlockSpec((None, m, tn), lambda l, j: (l, 0, j)),
        out_shape=jax.ShapeDtypeStruct((depth, m, d6), F32),
        compiler_params=_params("arbitrary", "arbitrary"),
        name="ada_mod",
    )(c_all, ada_w, ada_b.reshape(depth, 1, d6))


class _Tokens:
    def __init__(self, n_prompt_batch, seq, n_sample, d):
        self.d = d
        self.n_prompt = n_prompt_batch * seq
        self.n = self.n_prompt + n_sample
        self.prompt_tiles = self.n_prompt // TOKEN_TILE
        self.tiles_per_batch = seq // TOKEN_TILE
        self.n_batch = n_prompt_batch
        self.tiles = self.n // TOKEN_TILE

    def mod_specs(self, comp):
        d, tpb, nb, npt = self.d, self.tiles_per_batch, self.n_batch, self.prompt_tiles
        return [pl.BlockSpec((None, None, 1, d), lambda i: (jnp.minimum(i // tpb, nb - 1), comp, 0, 0)),
                pl.BlockSpec((TOKEN_TILE, d), lambda i: (jnp.maximum(i - npt, 0), comp),
                             pipeline_mode=pl.Buffered(1))]

    def row_spec(self, width):
        return pl.BlockSpec((TOKEN_TILE, width), lambda i: (i, 0))

    def split_specs(self, width, out=False):
        npt = self.prompt_tiles
        mode = {} if out else dict(pipeline_mode=pl.Buffered(1))
        return [pl.BlockSpec((TOKEN_TILE, width), lambda i: (jnp.minimum(i, npt - 1), 0)),
                pl.BlockSpec((TOKEN_TILE, width), lambda i: (jnp.maximum(i - npt, 0), 0), **mode)]

    def split_shapes(self, width, dtype):
        return [jax.ShapeDtypeStruct((self.n_prompt, width), dtype),
                jax.ShapeDtypeStruct((self.n - self.n_prompt, width), dtype)]


def _const_spec(shape):
    return pl.BlockSpec(shape, lambda i: (0,) * len(shape))


def _pick(is_sample, p_ref, s_ref):
    return jnp.where(is_sample, s_ref[...], p_ref[...])


def _store_split(is_sample, p_ref, s_ref, val):
    @pl.when(is_sample)
    def _():
        s_ref[...] = val

    @pl.when(jnp.logical_not(is_sample))
    def _():
        p_ref[...] = val


def _prenorm_kernel(prompt_tiles, x_p, x_s, g_ref, sh_p, sh_s, sc_p, sc_s, h_ref):
    is_sample = pl.program_id(0) >= prompt_tiles
    x = _pick(is_sample, x_p, x_s)
    h = _rms(x) * g_ref[...] * (1.0 + _pick(is_sample, sc_p, sc_s)) + _pick(is_sample, sh_p, sh_s)
    h_ref[...] = h.astype(BF16)


def _prenorm(tok, x_p, x_s, gain, mod_p, mod_s):
    d = tok.d
    return pl.pallas_call(
        functools.partial(_prenorm_kernel, tok.prompt_tiles),
        grid=(tok.tiles,),
        in_specs=tok.split_specs(d) + [_const_spec((1, d))] + tok.mod_specs(0) + tok.mod_specs(1),
        out_specs=tok.row_spec(d),
        out_shape=jax.ShapeDtypeStruct((tok.n, d), BF16),
        compiler_params=_params("arbitrary"),
        name="prenorm",
    )(x_p, x_s, gain, mod_p, mod_s, mod_p, mod_s)


def _proj_kernel(h_ref, w_ref, o_ref, wb_ref):
    @pl.when(pl.program_id(1) == 0)
    def _():
        wb_ref[...] = w_ref[...].astype(BF16)

    o_ref[...] = _dot(h_ref[...], wb_ref[...]).astype(BF16)


def _proj(h, w):
    n, d = h.shape
    p = w.shape[1]
    tm = PROJ_TILE if n % PROJ_TILE == 0 else TOKEN_TILE
    tn = min(PROJ_COLS, p)
    return pl.pallas_call(
        _proj_kernel,
        grid=(p // tn, n // tm),
        in_specs=[pl.BlockSpec((tm, d), lambda j, i: (i, 0)),
                  pl.BlockSpec((d, tn), lambda j, i: (0, j))],
        out_specs=pl.BlockSpec((tm, tn), lambda j, i: (i, j)),
        out_shape=jax.ShapeDtypeStruct((n, p), BF16),
        scratch_shapes=[pltpu.VMEM((d, tn), BF16)],
        compiler_params=_params("arbitrary", "arbitrary"),
        name="in_proj",
    )(h, w)


def _rope_tables(pos):
    half = RET_DK // 2
    theta = 1.0 / (ROPE_BASE ** jnp.linspace(0.0, 1.0, half, dtype=F32))
    ang = pos.astype(F32)[:, None] * theta[None, :]
    cos, sin = jnp.cos(ang), jnp.sin(ang)
    return (jnp.repeat(cos, 2, axis=1),
            jnp.stack([-sin, sin], axis=-1).reshape(pos.shape[0], RET_DK))


def _rot(x, cos, sin_signed):
    lane = lax.broadcasted_iota(I32, x.shape, 1)
    width = x.shape[1]
    nbr = jnp.where((lane & 1) == 0, pltpu.roll(x, width - 1, 1), pltpu.roll(x, 1, 1))
    return x * cos + nbr * sin_signed


def _gate_out(o, g_ref_val):
    return (_rms(o) * _silu(g_ref_val.astype(F32))).astype(BF16)


def _ret_prompt_kernel(q_ref, k_ref, v_ref, g_ref, cos_ref, sin_ref, lg_ref, o_ref, s_out_ref, s_ref):
    c = RET_CHUNK
    seq = q_ref.shape[0]
    lg = lg_ref[0:1, 0:1]
    ti = lax.broadcasted_iota(I32, (c, c), 0)
    si = lax.broadcasted_iota(I32, (c, c), 1)
    decay = jnp.where(ti >= si, jnp.exp((ti - si).astype(F32) * lg), 0.0)
    tcol = lax.broadcasted_iota(I32, (c, 1), 0).astype(F32)
    dq = jnp.exp((tcol + 1.0) * lg)
    dk = jnp.exp((float(c - 1) - tcol) * lg)
    dchunk = jnp.exp(float(c) * lg)
    s_ref[...] = jnp.zeros_like(s_ref)

    def body(ci, carry):
        r0 = pl.multiple_of(ci * c, c)
        rows = pl.ds(r0, c)
        cos, sin = cos_ref[rows, :], sin_ref[rows, :]
        q = _rot(q_ref[rows, :].astype(F32), cos, sin)
        k = _rot(k_ref[rows, :].astype(F32), cos, sin) * (RET_DK ** -0.5)
        v = v_ref[rows, :]
        s = s_ref[...]
        scores = _dot_nt(q.astype(BF16), k.astype(BF16)) * decay
        o = _dot(scores.astype(BF16), v) + _dot((q * dq).astype(BF16), s.astype(BF16))
        s_ref[...] = dchunk * s + _dot_tn((k * dk).astype(BF16), v)
        o_ref[rows, :] = _gate_out(o, g_ref[rows, :])
        return carry

    lax.fori_loop(0, seq // c, body, 0)
    s_out_ref[...] = s_ref[...]


def _ret_prompt(proj, n_batch, seq, heads, cos, sin, log_gamma):
    qk_blocks = heads
    return pl.pallas_call(
        _ret_prompt_kernel,
        grid=(n_batch, heads),
        in_specs=[pl.BlockSpec((seq, RET_DK), lambda b, h: (b, h)),
                  pl.BlockSpec((seq, RET_DK), lambda b, h: (b, qk_blocks + h)),
                  pl.BlockSpec((seq, RET_DV), lambda b, h: (b, heads + h)),
                  pl.BlockSpec((seq, RET_DV), lambda b, h: (b, 2 * heads + h)),
                  pl.BlockSpec((seq, RET_DK), lambda b, h: (0, 0)),
                  pl.BlockSpec((seq, RET_DK), lambda b, h: (0, 0)),
                  pl.BlockSpec((None, 1, 128), lambda b, h: (h, 0, 0))],
        out_specs=[pl.BlockSpec((seq, RET_DV), lambda b, h: (b, h)),
                   pl.BlockSpec((None, None, RET_DK, RET_DV), lambda b, h: (b, h, 0, 0))],
        out_shape=[jax.ShapeDtypeStruct((n_batch * seq, heads * RET_DV), BF16),
                   jax.ShapeDtypeStruct((n_batch, heads, RET_DK, RET_DV), F32)],
        scratch_shapes=[pltpu.VMEM((RET_DK, RET_DV), F32)],
        compiler_params=_params("arbitrary", "arbitrary"),
        name="ret_prompt",
    )(proj, proj, proj, proj, cos, sin, log_gamma)


def _ret_sample_kernel(steps, q_ref, k_ref, v_ref, g_ref, cos_ref, sin_ref, lg_ref, s_in_ref, o_ref, s_out_ref):
    rows = q_ref.shape[0]
    shift = int(math.log2(steps))
    cos, sin = cos_ref[...], sin_ref[...]
    ri = lax.broadcasted_iota(I32, (rows, rows), 0)
    ci = lax.broadcasted_iota(I32, (rows, rows), 1)
    pair = ((ri >> shift) == (ci >> shift)) & (ri >= ci)
    rid = lax.broadcasted_iota(I32, (rows, 1), 0)
    step = (rid & (steps - 1)).astype(F32)
    row_batch = rid >> shift
    for hh in range(RET_SAMPLE_HEADS):
        kc = slice(hh * RET_DK, (hh + 1) * RET_DK)
        vc = slice(hh * RET_DV, (hh + 1) * RET_DV)
        lg = lg_ref[hh, 0:1, 0:1]
        q = _rot(q_ref[:, kc].astype(F32), cos, sin)
        k = _rot(k_ref[:, kc].astype(F32), cos, sin) * (RET_DK ** -0.5)
        v = v_ref[:, vc]
        decay = jnp.where(pair, jnp.exp((ri - ci).astype(F32) * lg), 0.0)
        o = _dot((_dot_nt(q.astype(BF16), k.astype(BF16)) * decay).astype(BF16), v)
        qd = (q * jnp.exp((step + 1.0) * lg)).astype(BF16)
        kd = k * jnp.exp((float(steps - 1) - step) * lg)
        dall = jnp.exp(float(steps) * lg)
        for j in range(rows // steps):
            s = s_in_ref[j, hh]
            o = o + jnp.where(row_batch == j, _dot(qd, s.astype(BF16)), 0.0)
            kj = jnp.where(row_batch == j, kd, 0.0).astype(BF16)
            s_out_ref[j, hh] = dall * s + _dot_tn(kj, v)
        o_ref[:, vc] = _gate_out(o, g_ref[:, vc])


def _ret_sample(proj, state, n_prompt, steps, heads, cos, sin, log_gamma):
    n_sample_batch = state.shape[0]
    per = SAMPLE_ROWS // steps
    base = n_prompt // SAMPLE_ROWS
    hs = RET_SAMPLE_HEADS
    groups = heads // hs
    kw, vw = hs * RET_DK, hs * RET_DV
    return pl.pallas_call(
        functools.partial(_ret_sample_kernel, steps),
        grid=(n_sample_batch // per, groups),
        in_specs=[pl.BlockSpec((SAMPLE_ROWS, kw), lambda b, h: (base + b, h)),
                  pl.BlockSpec((SAMPLE_ROWS, kw), lambda b, h: (base + b, groups + h)),
                  pl.BlockSpec((SAMPLE_ROWS, vw), lambda b, h: (base + b, groups + h)),
                  pl.BlockSpec((SAMPLE_ROWS, vw), lambda b, h: (base + b, 2 * groups + h)),
                  pl.BlockSpec((SAMPLE_ROWS, RET_DK), lambda b, h: (0, 0)),
                  pl.BlockSpec((SAMPLE_ROWS, RET_DK), lambda b, h: (0, 0)),
                  pl.BlockSpec((hs, 1, 128), lambda b, h: (h, 0, 0)),
                  pl.BlockSpec((per, hs, RET_DK, RET_DV), lambda b, h: (b, h, 0, 0))],
        out_specs=[pl.BlockSpec((SAMPLE_ROWS, vw), lambda b, h: (b, h)),
                   pl.BlockSpec((per, hs, RET_DK, RET_DV), lambda b, h: (b, h, 0, 0))],
        out_shape=[jax.ShapeDtypeStruct((n_sample_batch * steps, heads * RET_DV), BF16),
                   jax.ShapeDtypeStruct(state.shape, F32)],
        compiler_params=_params("arbitrary", "arbitrary"),
        name="ret_sample",
    )(proj, proj, proj, proj, cos, sin, log_gamma, state)


def _hg_prefix_matrix():
    c = HG_CHUNK
    t = np.arange(c)[:, None]
    s = np.arange(c)[None, :]
    le = (s <= t).astype(np.float32)
    mats = [le, le * ((s // HG_LEAF) == (t // HG_LEAF))]
    for half in HG_LEVELS:
        mid = (t // (2 * half)) * (2 * half) + half - 1
        mats.append(le - (s <= mid).astype(np.float32))
    return jnp.asarray(np.concatenate(mats, axis=0), dtype=BF16)


def _hg_gates(q, f, lb):
    forget = lb + (1.0 - lb) * jax.nn.sigmoid(f)
    return _silu(q), 1.0 - forget, jnp.log(forget)


def _hg_prompt_kernel(q_ref, f_ref, i_ref, g_ref, lb_ref, gn_ref, pm_ref, o_ref, s_out_ref, s_ref):
    c = HG_CHUNK
    seq = q_ref.shape[0]
    n_heads = q_ref.shape[1] // HG_DH
    ti = lax.broadcasted_iota(I32, (c, c), 0)
    si = lax.broadcasted_iota(I32, (c, c), 1)
    leaf_shift = int(math.log2(HG_LEAF))
    mask_leaf = (si <= ti) & ((ti >> leaf_shift) == (si >> leaf_shift))
    level_masks = []
    for half in HG_LEVELS:
        sh = int(math.log2(2 * half))
        level_masks.append(((ti >> sh) == (si >> sh)) & ((ti & (2 * half - 1)) >= half) & ((si & (2 * half - 1)) < half))
    eye = ti == si
    s_ref[...] = jnp.zeros_like(s_ref)

    def one_head(hh, rows):
        cols = slice(hh * HG_DH, (hh + 1) * HG_DH)
        qh, kk, lf = _hg_gates(q_ref[rows, cols].astype(F32), f_ref[rows, cols].astype(F32), lb_ref[:, cols])
        v = i_ref[rows, cols]
        lf2 = jnp.concatenate(_split2(lf), axis=1)

        def sums(part):
            r = _dot(pm_ref[part * c:(part + 1) * c, :], lf2)
            return r[:, :HG_DH] + r[:, HG_DH:]

        d_leaf = sums(1)
        a = jnp.where(mask_leaf,
                      _dot_nt((qh * jnp.exp(d_leaf)).astype(BF16), (kk * jnp.exp(-d_leaf)).astype(BF16)), 0.0)
        for lvl, mask in enumerate(level_masks):
            w = jnp.exp(-jnp.abs(sums(2 + lvl)))
            a = a + jnp.where(mask, _dot_nt((qh * w).astype(BF16), (kk * w).astype(BF16)), 0.0)
        b = sums(0)
        s = s_ref[hh]
        o = _dot(a.astype(BF16), v) + _dot((qh * jnp.exp(b)).astype(BF16), s.astype(BF16))
        b_last = b[c - 1:c, :]
        col = jnp.sum(jnp.where(eye, jnp.exp(b_last), 0.0), axis=1, keepdims=True)
        s_ref[hh] = col * s + _dot_tn((kk * jnp.exp(b_last - b)).astype(BF16), v)
        o_ref[rows, cols] = (_rms(o) * gn_ref[:, cols] * _silu(g_ref[rows, cols].astype(F32))).astype(BF16)

    def body(ci, carry):
        rows = pl.ds(pl.multiple_of(ci * c, c), c)
        for hh in range(n_heads):
            one_head(hh, rows)
        return carry

    lax.fori_loop(0, seq // c, body, 0)
    s_out_ref[...] = s_ref[...]


def _hg_prompt(proj, n_batch, seq, heads, lb, out_norm):
    pm = _hg_prefix_matrix()
    hp = HG_HEADS_PER_STEP
    width = hp * HG_DH
    groups = heads // hp
    col = lambda part: (lambda b, h: (b, part * groups + h))
    return pl.pallas_call(
        _hg_prompt_kernel,
        grid=(n_batch, groups),
        in_specs=[pl.BlockSpec((seq, width), col(0)),
                  pl.BlockSpec((seq, width), col(1)),
                  pl.BlockSpec((seq, width), col(2)),
                  pl.BlockSpec((seq, width), col(3)),
                  pl.BlockSpec((1, width), lambda b, h: (0, h)),
                  pl.BlockSpec((1, width), lambda b, h: (0, h)),
                  pl.BlockSpec(pm.shape, lambda b, h: (0, 0))],
        out_specs=[pl.BlockSpec((seq, width), lambda b, h: (b, h)),
                   pl.BlockSpec((None, hp, HG_DH, HG_DH), lambda b, h: (b, h, 0, 0))],
        out_shape=[jax.ShapeDtypeStruct((n_batch * seq, heads * HG_DH), BF16),
                   jax.ShapeDtypeStruct((n_batch, heads, HG_DH, HG_DH), F32)],
        scratch_shapes=[pltpu.VMEM((hp, HG_DH, HG_DH), F32)],
        compiler_params=_params("arbitrary", "arbitrary"),
        name="hg_prompt",
    )(proj, proj, proj, proj, lb, out_norm, pm)


def _hg_sample_kernel(steps, heads, q_ref, f_ref, i_ref, g_ref, lb_ref, gn_ref, s_in_ref, o_ref, s_out_ref):
    rows = q_ref.shape[0]
    shift = int(math.log2(steps))
    ri = lax.broadcasted_iota(I32, (rows, rows), 0)
    ci = lax.broadcasted_iota(I32, (rows, rows), 1)
    prefix = (((ri >> shift) == (ci >> shift)) & (ci <= ri)).astype(BF16)
    rid = lax.broadcasted_iota(I32, (rows, 1), 0)
    step = rid & (steps - 1)
    row_batch = rid >> shift
    ki = lax.broadcasted_iota(I32, (HG_DH, HG_DH), 0)
    vi = lax.broadcasted_iota(I32, (HG_DH, HG_DH), 1)
    eye = ki == vi

    def head(h):
        cols = pl.ds(pl.multiple_of(h * HG_DH, HG_DH), HG_DH)
        qh, kk, lf = _hg_gates(q_ref[:, cols].astype(F32), f_ref[:, cols].astype(F32), lb_ref[:, cols])
        v = i_ref[:, cols]
        vf = v.astype(F32)
        hi, lo = _split2(lf)
        b = _dot(prefix, jnp.concatenate([hi, lo], axis=1))
        b = b[:, :HG_DH] + b[:, HG_DH:]
        o = jnp.zeros((rows, HG_DH), F32)
        for dist in range(steps):
            if dist == 0:
                k_s, b_s, v_s = kk, b, vf
            else:
                k_s, b_s, v_s = (pltpu.roll(kk, dist, 0), pltpu.roll(b, dist, 0), pltpu.roll(vf, dist, 0))
            w = jnp.sum(qh * k_s * jnp.exp(jnp.minimum(b - b_s, 0.0)), axis=1, keepdims=True)
            o = o + jnp.where(step >= dist, w, 0.0) * v_s
        qd = (qh * jnp.exp(b)).astype(BF16)
        for j in range(rows // steps):
            s = s_in_ref[j, h]
            o = o + jnp.where(row_batch == j, _dot(qd, s.astype(BF16)), 0.0)
            b_last = b[(j + 1) * steps - 1:(j + 1) * steps, :]
            col = jnp.sum(jnp.where(eye, jnp.exp(b_last), 0.0), axis=1, keepdims=True)
            kj = jnp.where(row_batch == j, kk * jnp.exp(jnp.minimum(b_last - b, 0.0)), 0.0).astype(BF16)
            s_out_ref[j, h] = col * s + _dot_tn(kj, v)
        o_ref[:, cols] = (_rms(o) * gn_ref[:, cols] * _silu(g_ref[:, cols].astype(F32))).astype(BF16)

    def group(gi, carry):
        for hh in range(HG_HEADS_PER_STEP):
            head(gi * HG_HEADS_PER_STEP + hh)
        return carry

    lax.fori_loop(0, heads // HG_HEADS_PER_STEP, group, 0)


def _hg_sample(proj, state, n_prompt, steps, heads, lb, out_norm):
    n_sample_batch = state.shape[0]
    d = heads * HG_DH
    per = SAMPLE_ROWS // steps
    base = n_prompt // SAMPLE_ROWS
    col = lambda part: (lambda b: (base + b, part))
    return pl.pallas_call(
        functools.partial(_hg_sample_kernel, steps, heads),
        grid=(n_sample_batch // per,),
        in_specs=[pl.BlockSpec((SAMPLE_ROWS, d), col(0)),
                  pl.BlockSpec((SAMPLE_ROWS, d), col(1)),
                  pl.BlockSpec((SAMPLE_ROWS, d), col(2)),
                  pl.BlockSpec((SAMPLE_ROWS, d), col(3)),
                  pl.BlockSpec((1, d), lambda b: (0, 0)),
                  pl.BlockSpec((1, d), lambda b: (0, 0)),
                  pl.BlockSpec((per, heads, HG_DH, HG_DH), lambda b: (b, 0, 0, 0))],
        out_specs=[pl.BlockSpec((SAMPLE_ROWS, d), lambda b: (b, 0)),
                   pl.BlockSpec((per, heads, HG_DH, HG_DH), lambda b: (b, 0, 0, 0))],
        out_shape=[jax.ShapeDtypeStruct((n_sample_batch * steps, d), BF16),
                   jax.ShapeDtypeStruct(state.shape, F32)],
        compiler_params=_params("arbitrary"),
        name="hg_sample",
    )(proj, proj, proj, proj, lb, out_norm, state)


def _first_max(vals, lane, width):
    m = jnp.max(vals, axis=1, keepdims=True)
    idx = jnp.min(jnp.where(vals == m, lane, width), axis=1, keepdims=True)
    return m, idx


def _route(logits, bias):
    neg = -jnp.inf
    lane = lax.broadcasted_iota(I32, logits.shape, 1)
    group = lane >> GROUP_SHIFT
    scores = jax.nn.sigmoid(logits)
    biased = scores + bias
    best = sel = None
    for gi in range(N_GROUPS):
        vals = jnp.where(group == gi, biased, neg)
        m1, i1 = _first_max(vals, lane, N_EXPERTS)
        m2 = jnp.max(jnp.where(lane == i1, neg, vals), axis=1, keepdims=True)
        total = m1 + m2
        if gi == 0:
            best, sel = total, jnp.zeros_like(i1)
        else:
            better = total > best
            sel = jnp.where(better, gi, sel)
            best = jnp.where(better, total, best)
    vals = jnp.where(group == sel, biased, neg)
    _, e1 = _first_max(vals, lane, N_EXPERTS)
    _, e2 = _first_max(jnp.where(lane == e1, neg, vals), lane, N_EXPERTS)
    w1 = jnp.sum(jnp.where(lane == e1, scores, 0.0), axis=1, keepdims=True)
    w2 = jnp.sum(jnp.where(lane == e2, scores, 0.0), axis=1, keepdims=True)
    tot = w1 + w2
    return e1, e2, w1 / tot, w2 / tot


def _mixout_kernel(prompt_tiles, o_p, o_s, w_ref, x_p, x_s, g1_ref, g2_ref, gm_p, gm_s, sh_p, sh_s, sc_p, sc_s,
                   rw_ref, rb_ref, xo_p, xo_s, h_ref, ridx_ref, rwt_ref, cnt_ref, carry_ref, y_ref):
    i = pl.program_id(0)
    tm = x_p.shape[0]

    @pl.when(i == 0)
    def _():
        carry_ref[...] = jnp.zeros_like(carry_ref)

    def tile(sample):
        o_ref, x_ref, xo_ref = (o_s, x_s, xo_s) if sample else (o_p, x_p, xo_p)
        y_ref[...] = _dot(o_ref[...], w_ref[...])

        def chunk(rows, _):
            x = x_ref[rows, :] + _mod_rows(sample, gm_p, gm_s, rows) * (_rms(y_ref[rows, :]) * g1_ref[...])
            xo_ref[rows, :] = x
            h_ref[rows, :] = (_rms(x) * g2_ref[...] * (1.0 + _mod_rows(sample, sc_p, sc_s, rows))
                              + _mod_rows(sample, sh_p, sh_s, rows))

        _row_chunks(tm, chunk)

    _by_group(i >= prompt_tiles, tile)

    h1, h2 = _split2(h_ref[...])
    w1, w2 = _split2(rw_ref[...])
    first = _dot(h1, jnp.concatenate([w1, w2], axis=1))
    logits = first[:, :N_EXPERTS] + (first[:, N_EXPERTS:] + _dot(h2, w1))
    e1, e2, p1, p2 = _route(logits, rb_ref[...])

    lane = lax.broadcasted_iota(I32, (tm, N_EXPERTS), 1)
    hot1, hot2 = lane == e1, lane == e2
    onehot = (hot1 | hot2).astype(BF16)
    ti = lax.broadcasted_iota(I32, (tm, tm), 0)
    si = lax.broadcasted_iota(I32, (tm, tm), 1)
    before = _dot((si < ti).astype(BF16), onehot) + carry_ref[...]
    r1 = jnp.sum(jnp.where(hot1, before, 0.0), axis=1, keepdims=True).astype(I32)
    r2 = jnp.sum(jnp.where(hot2, before, 0.0), axis=1, keepdims=True).astype(I32)
    carry = carry_ref[...] + jnp.sum(onehot.astype(F32), axis=0, keepdims=True)
    carry_ref[...] = carry

    wide = lax.broadcasted_iota(I32, (tm, 128), 1)
    ridx_ref[...] = jnp.where(wide == 0, e1, jnp.where(wide == 1, e2, jnp.where(wide == 2, r1, r2)))
    rwt_ref[...] = jnp.where(wide == 0, p1, p2)
    cnt_ref[...] = jnp.zeros_like(cnt_ref)
    cnt_ref[0:1, 0:N_EXPERTS] = carry


def _mixout(tok, o_prompt, o_sample, w_out, x_p, x_s, g1, g2, mod_p, mod_s, router_w, router_b):
    d = tok.d
    v = o_prompt.shape[1]
    return pl.pallas_call(
        functools.partial(_mixout_kernel, tok.prompt_tiles),
        grid=(tok.tiles,),
        in_specs=tok.split_specs(v)
                 + [pl.BlockSpec((v, d), lambda i: (0, 0), pipeline_mode=pl.Buffered(1))]
                 + tok.split_specs(d) + [_const_spec((1, d)), _const_spec((1, d))]
                 + tok.mod_specs(2) + tok.mod_specs(3) + tok.mod_specs(4)
                 + [_const_spec((d, N_EXPERTS)), _const_spec((1, N_EXPERTS))],
        out_specs=tok.split_specs(d, out=True) + [tok.row_spec(d), tok.row_spec(128), tok.row_spec(128),
                                        _const_spec((8, 128))],
        out_shape=tok.split_shapes(d, F32) + [jax.ShapeDtypeStruct((tok.n, d), F32),
                                              jax.ShapeDtypeStruct((tok.n, 128), I32),
                                              jax.ShapeDtypeStruct((tok.n, 128), F32),
                                              jax.ShapeDtypeStruct((8, 128), F32)],
        scratch_shapes=[pltpu.VMEM((1, N_EXPERTS), F32), pltpu.VMEM((TOKEN_TILE, d), F32)],
        compiler_params=_params("arbitrary"),
        name="mix_out",
    )(o_prompt, o_sample, w_out, x_p, x_s, g1, g2, mod_p, mod_s, mod_p, mod_s, mod_p, mod_s, router_w, router_b)


def _expert_plan(ridx, counts, n_tiles):
    n = ridx.shape[0]
    expert = ridx[:, 0:2]
    rank = ridx[:, 2:4]
    cnt = counts[0, :N_EXPERTS].astype(I32)
    padded = ((cnt + EXPERT_TILE - 1) // EXPERT_TILE) * EXPERT_TILE
    ends = jnp.cumsum(padded)
    starts = ends - padded
    pos = starts[expert] + rank
    tile = jnp.arange(n_tiles, dtype=I32)
    tile_start = tile * EXPERT_TILE
    used = tile_start < ends[-1]
    last_used = jnp.maximum(ends[-1] // EXPERT_TILE - 1, 0)
    tile_expert = jnp.sum((tile_start[:, None] >= ends[None, :]).astype(I32), axis=1)
    tile_expert = jnp.minimum(jnp.where(used, tile_expert, tile_expert[last_used]), N_EXPERTS - 1)
    valid = jnp.clip(cnt[tile_expert] - (tile_start - starts[tile_expert]), 0, EXPERT_TILE)
    valid = jnp.where(used, valid, 0)
    pos = pos.reshape(n // TOKEN_TILE, TOKEN_TILE, 2).transpose(0, 2, 1)
    sub = jnp.arange(EXPERT_F_SPLIT, dtype=I32)[None, :]
    snake = jnp.where((tile[:, None] & 1) == 0, sub, EXPERT_F_SPLIT - 1 - sub)
    part = jnp.where(used[:, None], snake, snake[last_used, EXPERT_F_SPLIT - 1]).reshape(-1)
    return tile_expert, valid, jnp.minimum(tile, last_used), part, pos


def _row_copies(groups, copy, whole):
    def body(g, carry):
        for j in range(SUBLANES):
            copy(g, j).start()
        return carry
    lax.fori_loop(0, groups, body, 0)
    whole.wait()


def _dispatch_kernel(pos_ref, h_ref, xs_in, xs_hbm, sem):
    del xs_in
    groups = h_ref.shape[0]
    for choice in range(2):
        def row(g, j, choice=choice):
            p = pos_ref[0, choice, g * SUBLANES + j]
            return pltpu.make_async_copy(h_ref.at[g, pl.ds(j, 1)],
                                         xs_hbm.at[p >> 3, pl.ds(p & (SUBLANES - 1), 1)], sem.at[choice])
        whole = pltpu.make_async_copy(h_ref, xs_hbm.at[pl.ds(0, groups)], sem.at[choice])
        _row_copies(groups, row, whole)


def _dispatch(h, pos, n_rows):
    n, d = h.shape
    tile_groups = TOKEN_TILE // SUBLANES
    return pl.pallas_call(
        _dispatch_kernel,
        grid=(n // TOKEN_TILE,),
        in_specs=[pl.BlockSpec((1, 2, TOKEN_TILE), lambda i: (i, 0, 0), memory_space=pltpu.SMEM),
                  pl.BlockSpec((tile_groups, SUBLANES, d), lambda i: (i, 0, 0)),
                  pl.BlockSpec(memory_space=pl.ANY)],
        out_specs=pl.BlockSpec(memory_space=pl.ANY),
        out_shape=jax.ShapeDtypeStruct((n_rows // SUBLANES, SUBLANES, d), F32),
        scratch_shapes=[pltpu.SemaphoreType.DMA((2,))],
        input_output_aliases={2: 0},
        compiler_params=_params("arbitrary"),
        name="dispatch",
    )(pos, h.reshape(n // SUBLANES, SUBLANES, d), jnp.zeros((n_rows // SUBLANES, SUBLANES, d), F32))


def _expert_kernel(te_ref, nv_ref, tin_ref, part_ref, x_ref, wg_ref, wu_ref, wd_ref, y_ref):
    del te_ref, tin_ref, part_ref
    n = nv_ref[pl.program_id(0)]
    step = pl.program_id(1)
    rows = x_ref.shape[0] * SUBLANES

    @pl.when(n > 0)
    def _():
        x = x_ref[...].reshape(rows, x_ref.shape[2]).astype(BF16)
        a = _dot(x, wg_ref[...].astype(BF16))
        u = _dot(x, wu_ref[...].astype(BF16))
        y = _dot((_silu(a) * u).astype(BF16), wd_ref[...].astype(BF16)).reshape(y_ref.shape)

        @pl.when(step == 0)
        def _():
            y_ref[...] = y

        @pl.when(step > 0)
        def _():
            y_ref[...] += y

    @pl.when((n == 0) & (step == 0))
    def _():
        y_ref[...] = jnp.zeros_like(y_ref)


def _experts(xs, plan, layer, wg, wu, wd):
    d, f = wg.shape[2], wg.shape[3]
    fs = f // EXPERT_F_SPLIT
    tile_expert, valid, tile_in, part = plan[:4]
    n_tiles = tile_expert.shape[0]
    tile_groups = EXPERT_TILE // SUBLANES
    which = lambda i, s, part: part[i * EXPERT_F_SPLIT + s]
    grid_spec = pltpu.PrefetchScalarGridSpec(
        num_scalar_prefetch=4,
        grid=(n_tiles, EXPERT_F_SPLIT),
        in_specs=[pl.BlockSpec((tile_groups, SUBLANES, d), lambda i, s, te, nv, tin, part: (tin[i], 0, 0)),
                  pl.BlockSpec((None, None, d, fs), lambda i, s, te, nv, tin, part: (layer, te[i], 0, which(i, s, part))),
                  pl.BlockSpec((None, None, d, fs), lambda i, s, te, nv, tin, part: (layer, te[i], 0, which(i, s, part))),
                  pl.BlockSpec((None, None, fs, d), lambda i, s, te, nv, tin, part: (layer, te[i], which(i, s, part), 0))],
        out_specs=pl.BlockSpec((tile_groups, SUBLANES, d), lambda i, s, te, nv, tin, part: (i, 0, 0)),
    )
    return pl.pallas_call(
        _expert_kernel,
        grid_spec=grid_spec,
        out_shape=jax.ShapeDtypeStruct((n_tiles * tile_groups, SUBLANES, d), F32),
        compiler_params=_params("arbitrary", "arbitrary"),
        name="experts",
    )(tile_expert, valid, tile_in, part, xs, wg, wu, wd)


def _by_group(is_sample, body):
    @pl.when(is_sample)
    def _():
        body(True)

    @pl.when(jnp.logical_not(is_sample))
    def _():
        body(False)


def _row_chunks(n_rows, fn):
    per = ROW_CHUNK // SUBLANES

    def body(ci, carry):
        fn(pl.ds(pl.multiple_of(ci * ROW_CHUNK, ROW_CHUNK), ROW_CHUNK), pl.ds(pl.multiple_of(ci * per, per), per))
        return carry
    lax.fori_loop(0, n_rows // ROW_CHUNK, body, 0, unroll=ROW_CHUNK_UNROLL)


def _mod_rows(sample, p_ref, s_ref, rows):
    return s_ref[rows, :] if sample else p_ref[...]


def _moeout_kernel(prompt_tiles, with_next, pos_ref, ys_hbm, x_p, x_s, rwt_ref, g3_ref, gf_p, gf_s, *rest):
    if with_next:
        gn_ref, sh_p, sh_s, sc_p, sc_s, xo_p, xo_s, h_ref, ybuf, sem = rest
    else:
        xo_p, xo_s, ybuf, sem = rest
    groups = ybuf.shape[1]
    n_rows = groups * SUBLANES
    d = ybuf.shape[3]
    for choice in range(2):
        def row(g, j, choice=choice):
            p = pos_ref[0, choice, g * SUBLANES + j]
            return pltpu.make_async_copy(ys_hbm.at[p >> 3, pl.ds(p & (SUBLANES - 1), 1)],
                                         ybuf.at[choice, g, pl.ds(j, 1)], sem.at[choice])
        whole = pltpu.make_async_copy(ys_hbm.at[pl.ds(0, groups)], ybuf.at[choice], sem.at[choice])
        _row_copies(groups, row, whole)

    def tile(sample):
        x_ref, xo_ref = (x_s, xo_s) if sample else (x_p, xo_p)

        def chunk(rows, grp):
            rw = rwt_ref[rows, :]
            y = (rw[:, 0:1] * ybuf[0, grp].reshape(ROW_CHUNK, d)
                 + rw[:, 1:2] * ybuf[1, grp].reshape(ROW_CHUNK, d))
            x = x_ref[rows, :] + _mod_rows(sample, gf_p, gf_s, rows) * (_rms(y) * g3_ref[...])
            xo_ref[rows, :] = x
            if with_next:
                h = (_rms(x) * gn_ref[...] * (1.0 + _mod_rows(sample, sc_p, sc_s, rows))
                     + _mod_rows(sample, sh_p, sh_s, rows))
                h_ref[rows, :] = h.astype(BF16)

        _row_chunks(n_rows, chunk)

    _by_group(pl.program_id(0) >= prompt_tiles, tile)


def _moeout(tok, pos, ys, x_p, x_s, rwt, g3, mod_p, mod_s, nxt=None):
    d = tok.d
    ins = ([pl.BlockSpec((1, 2, TOKEN_TILE), lambda i: (i, 0, 0), memory_space=pltpu.SMEM),
            pl.BlockSpec(memory_space=pl.ANY)]
           + tok.split_specs(d) + [tok.row_spec(128), _const_spec((1, d))] + tok.mod_specs(5))
    args = [pos, ys, x_p, x_s, rwt, g3, mod_p, mod_s]
    scratch = [pltpu.VMEM((2, TOKEN_TILE // SUBLANES, SUBLANES, d), F32), pltpu.SemaphoreType.DMA((2,))]
    if nxt is None:
        return pl.pallas_call(
            functools.partial(_moeout_kernel, tok.prompt_tiles, False),
            grid=(tok.tiles,), in_specs=ins, out_specs=tok.split_specs(d, out=True),
            out_shape=tok.split_shapes(d, F32), scratch_shapes=scratch,
            compiler_params=_params("arbitrary"), name="moe_out_last",
        )(*args)
    gain_n, mod_pn, mod_sn = nxt
    ins = ins + [_const_spec((1, d))] + tok.mod_specs(0) + tok.mod_specs(1)
    args = args + [gain_n, mod_pn, mod_sn, mod_pn, mod_sn]
    return pl.pallas_call(
        functools.partial(_moeout_kernel, tok.prompt_tiles, True),
        grid=(tok.tiles,), in_specs=ins, out_specs=tok.split_specs(d, out=True) + [tok.row_spec(d)],
        out_shape=tok.split_shapes(d, F32) + [jax.ShapeDtypeStruct((tok.n, d), BF16)],
        scratch_shapes=scratch,
        compiler_params=_params("arbitrary"), name="moe_out_next",
    )(*args)


def kernel(x_prompt, x_sample, c_prompt, c_sample, state_ret, state_hgrn, ada_w, ada_b, norm_gains,
           ret_w_in, ret_w_out, hg_w_in, hg_w_out, hg_lower_bound, hg_out_norm, router_w, router_b,
           exp_w_gate, exp_w_up, exp_w_down):
    n_batch, seq, d = x_prompt.shape
    n_dec, steps, _ = x_sample.shape
    depth = ada_w.shape[0]
    assert depth == 2 and d % RET_DK == 0 and d % HG_DH == 0
    assert seq % RET_CHUNK == 0 and (n_dec * steps) % TOKEN_TILE == 0 and SAMPLE_ROWS % steps == 0
    ret_heads = d // RET_DK
    hg_heads = d // HG_DH
    n_prompt = n_batch * seq
    n_sample = n_dec * steps
    tok = _Tokens(n_batch, seq, n_sample, d)
    n = tok.n

    n_cond = n_batch + n_dec
    pad = (-n_cond) % 8
    c_all = jnp.concatenate([c_prompt, c_sample, jnp.zeros((pad, d), F32)], axis=0)
    mod = _ada(c_all, ada_w, ada_b)
    mods = []
    for l in range(depth):
        mod_p = mod[l, :n_batch].reshape(n_batch, 6, 1, d)
        mod_s = jnp.repeat(mod[l, n_batch:n_cond], steps, axis=0)
        mods.append((mod_p, mod_s))
    gain = lambda l, k: norm_gains[l, k].reshape(1, d)

    x_p = x_prompt.reshape(n_prompt, d)
    x_s = x_sample.reshape(n_sample, d)
    n_tiles = (2 * n + N_EXPERTS * (EXPERT_TILE - 1)) // EXPERT_TILE + 1
    rw = router_w.astype(F32)
    rb = router_b.astype(F32).reshape(1, N_EXPERTS)

    def channel_mixer(l, x_p, x_s, h, ridx, rwt, counts, nxt):
        plan = _expert_plan(ridx, counts, n_tiles)
        pos = plan[4]
        xs = _dispatch(h, pos, n_tiles * EXPERT_TILE)
        ys = _experts(xs, plan, l, exp_w_gate, exp_w_up, exp_w_down)
        return _moeout(tok, pos, ys, x_p, x_s, rwt, gain(l, 3), *mods[l], nxt=nxt)

    h = _prenorm(tok, x_p, x_s, gain(0, 0), *mods[0])
    proj = _proj(h, ret_w_in[0])
    log_gamma = jnp.log(1.0 - jnp.exp2(-5.0 - jnp.arange(ret_heads, dtype=F32)))
    log_gamma = jnp.broadcast_to(log_gamma[:, None, None], (ret_heads, 1, 128))
    cos_p, sin_p = _rope_tables(jnp.arange(seq))
    cos_s, sin_s = _rope_tables(PAST_LEN + jnp.arange(steps))
    reps = SAMPLE_ROWS // steps
    cos_s, sin_s = jnp.tile(cos_s, (reps, 1)), jnp.tile(sin_s, (reps, 1))
    o_p, ret_prompt = _ret_prompt(proj, n_batch, seq, ret_heads, cos_p, sin_p, log_gamma)
    o_s, ret_sample = _ret_sample(proj, state_ret[0], n_prompt, steps, ret_heads, cos_s, sin_s, log_gamma)
    x_p, x_s, hp, ridx, rwt, counts = _mixout(tok, o_p, o_s, ret_w_out[0].astype(BF16), x_p, x_s,
                                              gain(0, 1), gain(0, 2), *mods[0], rw, rb)
    x_p, x_s, h = channel_mixer(0, x_p, x_s, hp, ridx, rwt, counts, (gain(1, 0),) + mods[1])

    sm = jax.nn.softmax(hg_lower_bound.astype(F32), axis=0)
    lb = (jnp.cumsum(sm, axis=0) - sm[0])[1].reshape(1, d)
    proj = _proj(h, hg_w_in[0])
    out_norm = hg_out_norm[0].reshape(1, d)
    o_p, hg_prompt = _hg_prompt(proj, n_batch, seq, hg_heads, lb, out_norm)
    o_s, hg_sample = _hg_sample(proj, state_hgrn[0], n_prompt, steps, hg_heads, lb, out_norm)
    x_p, x_s, hp, ridx, rwt, counts = _mixout(tok, o_p, o_s, hg_w_out[0].astype(BF16), x_p, x_s,
                                              gain(1, 1), gain(1, 2), *mods[1], rw, rb)
    x_p, x_s = channel_mixer(1, x_p, x_s, hp, ridx, rwt, counts, None)

    return (x_p.reshape(n_batch, seq, d), x_s.reshape(n_dec, steps, d),
            ret_prompt[None], hg_prompt[None], ret_sample[None], hg_sample[None])
```

```python
import functools
import math

import numpy as np
import jax
import jax.numpy as jnp
from jax import lax
from jax.experimental import pallas as pl
from jax.experimental.pallas import tpu as pltpu

F32, BF16, I32 = jnp.float32, jnp.bfloat16, jnp.int32

EPS = 1e-6
ROPE_BASE = 10000.0
PAST_LEN = 16384
RET_DK = 256
RET_DV = 512
HG_DH = 128
N_EXPERTS = 16
GROUP_SHIFT = 2
N_GROUPS = N_EXPERTS >> GROUP_SHIFT

TOKEN_TILE = 256
PROJ_TILE = 512
PROJ_COLS = 1024
EXPERT_TILE = 384
RET_CHUNK = 256
HG_CHUNK = 128
HG_LEAF = 16
HG_LEVELS = (64, 32, 16)
HG_HEADS_PER_STEP = 4
HG_LEAF_LOG_LIMIT = -80.0
SAMPLE_ROWS = 16
SUBLANES = 8
EXPERT_F_SPLIT = 2
RET_SAMPLE_HEADS = 2
ROW_CHUNK = 16
ROW_CHUNK_UNROLL = 4
VMEM_LIMIT = 56 * 1024 * 1024


def _params(*sem):
    return pltpu.CompilerParams(dimension_semantics=sem, vmem_limit_bytes=VMEM_LIMIT)


def _rms(x):
    return x * lax.rsqrt(jnp.mean(x * x, axis=-1, keepdims=True) + EPS)


def _silu(x):
    return x * jax.nn.sigmoid(x)


def _dot(a, b):
    return jnp.dot(a, b, preferred_element_type=F32)


def _dot_nt(a, b):
    return lax.dot_general(a, b, (((1,), (1,)), ((), ())), preferred_element_type=F32)


def _dot_tn(a, b):
    return lax.dot_general(a, b, (((0,), (0,)), ((), ())), preferred_element_type=F32)


def _split2(x):
    hi = x.astype(BF16)
    lo = (x - hi.astype(F32)).astype(BF16)
    return hi, lo


def _split3(x):
    hi = x.astype(BF16)
    r = x - hi.astype(F32)
    mid = r.astype(BF16)
    lo = (r - mid.astype(F32)).astype(BF16)
    return hi, mid, lo


def _ada_kernel(c_ref, w_ref, b_ref, o_ref):
    s = _silu(c_ref[...]).astype(BF16)
    o_ref[...] = _dot(s, w_ref[...].astype(BF16)) + b_ref[...]


def _ada(c_all, ada_w, ada_b):
    depth, d, d6 = ada_w.shape
    m = c_all.shape[0]
    tn = min(1024, d6)
    return pl.pallas_call(
        _ada_kernel,
        grid=(depth, d6 // tn),
        in_specs=[pl.BlockSpec((m, d), lambda l, j: (0, 0)),
                  pl.BlockSpec((None, d, tn), lambda l, j: (l, 0, j)),
                  pl.BlockSpec((None, 1, tn), lambda l, j: (l, 0, j))],
        out_specs=pl.BlockSpec((None, m, tn), lambda l, j: (l, 0, j)),
        out_shape=jax.ShapeDtypeStruct((depth, m, d6), F32),
        compiler_params=_params("arbitrary", "arbitrary"),
        name="ada_mod",
    )(c_all, ada_w, ada_b.reshape(depth, 1, d6))


class _Tokens:
    def __init__(self, n_prompt_batch, seq, n_sample, d):
        self.d = d
        self.n_prompt = n_prompt_batch * seq
        self.n = self.n_prompt + n_sample
        self.prompt_tiles = self.n_prompt // TOKEN_TILE
        self.tiles_per_batch = seq // TOKEN_TILE
        self.n_batch = n_prompt_batch
        self.tiles = self.n // TOKEN_TILE

    def mod_specs(self, comp):
        d, tpb, nb, npt = self.d, self.tiles_per_batch, self.n_batch, self.prompt_tiles
        return [pl.BlockSpec((None, None, 1, d), lambda i: (jnp.minimum(i // tpb, nb - 1), comp, 0, 0)),
                pl.BlockSpec((TOKEN_TILE, d), lambda i: (jnp.maximum(i - npt, 0), comp),
                             pipeline_mode=pl.Buffered(1))]

    def row_spec(self, width):
        return pl.BlockSpec((TOKEN_TILE, width), lambda i: (i, 0))

    def split_specs(self, width, out=False):
        npt = self.prompt_tiles
        mode = {} if out else dict(pipeline_mode=pl.Buffered(1))
        return [pl.BlockSpec((TOKEN_TILE, width), lambda i: (jnp.minimum(i, npt - 1), 0)),
                pl.BlockSpec((TOKEN_TILE, width), lambda i: (jnp.maximum(i - npt, 0), 0), **mode)]

    def split_shapes(self, width, dtype):
        return [jax.ShapeDtypeStruct((self.n_prompt, width), dtype),
                jax.ShapeDtypeStruct((self.n - self.n_prompt, width), dtype)]


def _const_spec(shape):
    return pl.BlockSpec(shape, lambda i: (0,) * len(shape))


def _pick(is_sample, p_ref, s_ref):
    return jnp.where(is_sample, s_ref[...], p_ref[...])


def _store_split(is_sample, p_ref, s_ref, val):
    @pl.when(is_sample)
    def _():
        s_ref[...] = val

    @pl.when(jnp.logical_not(is_sample))
    def _():
        p_ref[...] = val


def _prenorm_kernel(prompt_tiles, x_p, x_s, g_ref, sh_p, sh_s, sc_p, sc_s, h_ref):
    is_sample = pl.program_id(0) >= prompt_tiles
    x = _pick(is_sample, x_p, x_s)
    h = _rms(x) * g_ref[...] * (1.0 + _pick(is_sample, sc_p, sc_s)) + _pick(is_sample, sh_p, sh_s)
    h_ref[...] = h.astype(BF16)


def _prenorm(tok, x_p, x_s, gain, mod_p, mod_s):
    d = tok.d
    return pl.pallas_call(
        functools.partial(_prenorm_kernel, tok.prompt_tiles),
        grid=(tok.tiles,),
        in_specs=tok.split_specs(d) + [_const_spec((1, d))] + tok.mod_specs(0) + tok.mod_specs(1),
        out_specs=tok.row_spec(d),
        out_shape=jax.ShapeDtypeStruct((tok.n, d), BF16),
        compiler_params=_params("arbitrary"),
        name="prenorm",
    )(x_p, x_s, gain, mod_p, mod_s, mod_p, mod_s)


def _proj_kernel(h_ref, w_ref, o_ref, wb_ref):
    @pl.when(pl.program_id(1) == 0)
    def _():
        wb_ref[...] = w_ref[...].astype(BF16)

    o_ref[...] = _dot(h_ref[...], wb_ref[...]).astype(BF16)


def _proj(h, w):
    n, d = h.shape
    p = w.shape[1]
    tm = PROJ_TILE if n % PROJ_TILE == 0 else TOKEN_TILE
    tn = min(PROJ_COLS, p)
    return pl.pallas_call(
        _proj_kernel,
        grid=(p // tn, n // tm),
        in_specs=[pl.BlockSpec((tm, d), lambda j, i: (i, 0)),
                  pl.BlockSpec((d, tn), lambda j, i: (0, j))],
        out_specs=pl.BlockSpec((tm, tn), lambda j, i: (i, j)),
        out_shape=jax.ShapeDtypeStruct((n, p), BF16),
        scratch_shapes=[pltpu.VMEM((d, tn), BF16)],
        compiler_params=_params("arbitrary", "arbitrary"),
        name="in_proj",
    )(h, w)


def _rope_tables(pos):
    half = RET_DK // 2
    theta = 1.0 / (ROPE_BASE ** jnp.linspace(0.0, 1.0, half, dtype=F32))
    ang = pos.astype(F32)[:, None] * theta[None, :]
    cos, sin = jnp.cos(ang), jnp.sin(ang)
    return (jnp.repeat(cos, 2, axis=1),
            jnp.stack([-sin, sin], axis=-1).reshape(pos.shape[0], RET_DK))


def _rot(x, cos, sin_signed):
    lane = lax.broadcasted_iota(I32, x.shape, 1)
    width = x.shape[1]
    nbr = jnp.where((lane & 1) == 0, pltpu.roll(x, width - 1, 1), pltpu.roll(x, 1, 1))
    return x * cos + nbr * sin_signed


def _gate_out(o, g_ref_val):
    return (_rms(o) * _silu(g_ref_val.astype(F32))).astype(BF16)


def _ret_prompt_kernel(q_ref, k_ref, v_ref, g_ref, cos_ref, sin_ref, lg_ref, o_ref, s_out_ref, s_ref):
    c = RET_CHUNK
    seq = q_ref.shape[0]
    lg = lg_ref[0:1, 0:1]
    ti = lax.broadcasted_iota(I32, (c, c), 0)
    si = lax.broadcasted_iota(I32, (c, c), 1)
    decay = jnp.where(ti >= si, jnp.exp((ti - si).astype(F32) * lg), 0.0)
    tcol = lax.broadcasted_iota(I32, (c, 1), 0).astype(F32)
    dq = jnp.exp((tcol + 1.0) * lg)
    dk = jnp.exp((float(c - 1) - tcol) * lg)
    dchunk = jnp.exp(float(c) * lg)
    s_ref[...] = jnp.zeros_like(s_ref)

    def body(ci, carry):
        r0 = pl.multiple_of(ci * c, c)
        rows = pl.ds(r0, c)
        cos, sin = cos_ref[rows, :], sin_ref[rows, :]
        q = _rot(q_ref[rows, :].astype(F32), cos, sin)
        k = _rot(k_ref[rows, :].astype(F32), cos, sin) * (RET_DK ** -0.5)
        v = v_ref[rows, :]
        s = s_ref[...]
        scores = _dot_nt(q.astype(BF16), k.astype(BF16)) * decay
        o = _dot(scores.astype(BF16), v) + _dot((q * dq).astype(BF16), s.astype(BF16))
        s_ref[...] = dchunk * s + _dot_tn((k * dk).astype(BF16), v)
        o_ref[rows, :] = _gate_out(o, g_ref[rows, :])
        return carry

    lax.fori_loop(0, seq // c, body, 0)
    s_out_ref[...] = s_ref[...]


def _ret_prompt(proj, n_batch, seq, heads, cos, sin, log_gamma):
    qk_blocks = heads
    return pl.pallas_call(
        _ret_prompt_kernel,
        grid=(n_batch, heads),
        in_specs=[pl.BlockSpec((seq, RET_DK), lambda b, h: (b, h)),
                  pl.BlockSpec((seq, RET_DK), lambda b, h: (b, qk_blocks + h)),
                  pl.BlockSpec((seq, RET_DV), lambda b, h: (b, heads + h)),
                  pl.BlockSpec((seq, RET_DV), lambda b, h: (b, 2 * heads + h)),
                  pl.BlockSpec((seq, RET_DK), lambda b, h: (0, 0)),
                  pl.BlockSpec((seq, RET_DK), lambda b, h: (0, 0)),
                  pl.BlockSpec((None, 1, 128), lambda b, h: (h, 0, 0))],
        out_specs=[pl.BlockSpec((seq, RET_DV), lambda b, h: (b, h)),
                   pl.BlockSpec((None, None, RET_DK, RET_DV), lambda b, h: (b, h, 0, 0))],
        out_shape=[jax.ShapeDtypeStruct((n_batch * seq, heads * RET_DV), BF16),
                   jax.ShapeDtypeStruct((n_batch, heads, RET_DK, RET_DV), F32)],
        scratch_shapes=[pltpu.VMEM((RET_DK, RET_DV), F32)],
        compiler_params=_params("arbitrary", "arbitrary"),
        name="ret_prompt",
    )(proj, proj, proj, proj, cos, sin, log_gamma)


def _ret_sample_kernel(steps, q_ref, k_ref, v_ref, g_ref, cos_ref, sin_ref, lg_ref, s_in_ref, o_ref, s_out_ref):
    rows = q_ref.shape[0]
    shift = int(math.log2(steps))
    cos, sin = cos_ref[...], sin_ref[...]
    ri = lax.broadcasted_iota(I32, (rows, rows), 0)
    ci = lax.broadcasted_iota(I32, (rows, rows), 1)
    pair = ((ri >> shift) == (ci >> shift)) & (ri >= ci)
    rid = lax.broadcasted_iota(I32, (rows, 1), 0)
    step = (rid & (steps - 1)).astype(F32)
    row_batch = rid >> shift
    for hh in range(RET_SAMPLE_HEADS):
        kc = slice(hh * RET_DK, (hh + 1) * RET_DK)
        vc = slice(hh * RET_DV, (hh + 1) * RET_DV)
        lg = lg_ref[hh, 0:1, 0:1]
        q = _rot(q_ref[:, kc].astype(F32), cos, sin)
        k = _rot(k_ref[:, kc].astype(F32), cos, sin) * (RET_DK ** -0.5)
        v = v_ref[:, vc]
        decay = jnp.where(pair, jnp.exp((ri - ci).astype(F32) * lg), 0.0)
        o = _dot((_dot_nt(q.astype(BF16), k.astype(BF16)) * decay).astype(BF16), v)
        qd = (q * jnp.exp((step + 1.0) * lg)).astype(BF16)
        kd = k * jnp.exp((float(steps - 1) - step) * lg)
        dall = jnp.exp(float(steps) * lg)
        for j in range(rows // steps):
            s = s_in_ref[j, hh]
            o = o + jnp.where(row_batch == j, _dot(qd, s.astype(BF16)), 0.0)
            kj = jnp.where(row_batch == j, kd, 0.0).astype(BF16)
            s_out_ref[j, hh] = dall * s + _dot_tn(kj, v)
        o_ref[:, vc] = _gate_out(o, g_ref[:, vc])


def _ret_sample(proj, state, n_prompt, steps, heads, cos, sin, log_gamma):
    n_sample_batch = state.shape[0]
    per = SAMPLE_ROWS // steps
    base = n_prompt // SAMPLE_ROWS
    hs = RET_SAMPLE_HEADS
    groups = heads // hs
    kw, vw = hs * RET_DK, hs * RET_DV
    return pl.pallas_call(
        functools.partial(_ret_sample_kernel, steps),
        grid=(n_sample_batch // per, groups),
        in_specs=[pl.BlockSpec((SAMPLE_ROWS, kw), lambda b, h: (base + b, h)),
                  pl.BlockSpec((SAMPLE_ROWS, kw), lambda b, h: (base + b, groups + h)),
                  pl.BlockSpec((SAMPLE_ROWS, vw), lambda b, h: (base + b, groups + h)),
                  pl.BlockSpec((SAMPLE_ROWS, vw), lambda b, h: (base + b, 2 * groups + h)),
                  pl.BlockSpec((SAMPLE_ROWS, RET_DK), lambda b, h: (0, 0)),
                  pl.BlockSpec((SAMPLE_ROWS, RET_DK), lambda b, h: (0, 0)),
                  pl.BlockSpec((hs, 1, 128), lambda b, h: (h, 0, 0)),
                  pl.BlockSpec((per, hs, RET_DK, RET_DV), lambda b, h: (b, h, 0, 0))],
        out_specs=[pl.BlockSpec((SAMPLE_ROWS, vw), lambda b, h: (b, h)),
                   pl.BlockSpec((per, hs, RET_DK, RET_DV), lambda b, h: (b, h, 0, 0))],
        out_shape=[jax.ShapeDtypeStruct((n_sample_batch * steps, heads * RET_DV), BF16),
                   jax.ShapeDtypeStruct(state.shape, F32)],
        compiler_params=_params("arbitrary", "arbitrary"),
        name="ret_sample",
    )(proj, proj, proj, proj, cos, sin, log_gamma, state)


def _hg_prefix_matrix():
    c = HG_CHUNK
    t = np.arange(c)[:, None]
    s = np.arange(c)[None, :]
    le = (s <= t).astype(np.float32)
    mats = [le, le * ((s // HG_LEAF) == (t // HG_LEAF))]
    for half in HG_LEVELS:
        mid = (t // (2 * half)) * (2 * half) + half - 1
        mats.append(le - (s <= mid).astype(np.float32))
    return jnp.asarray(np.concatenate(mats, axis=0), dtype=BF16)


def _hg_gates(q, f, lb):
    forget = lb + (1.0 - lb) * jax.nn.sigmoid(f)
    return _silu(q), 1.0 - forget, jnp.log(forget)


def _hg_guard_kernel(f_ref, lb_ref, o_ref):
    @pl.when(pl.program_id(0) == 0)
    def _():
        o_ref[...] = jnp.zeros_like(o_ref)

    rows, width = f_ref.shape
    lb = lb_ref[...]
    lf = jnp.log(lb + (1.0 - lb) * jax.nn.sigmoid(f_ref[...].astype(F32)))
    leaf_sum = jnp.sum(lf.reshape(rows // HG_LEAF, HG_LEAF, width), axis=1)
    o_ref[...] = jnp.minimum(o_ref[...], jnp.min(leaf_sum))


def _hg_guard(proj, n_prompt, heads, lb):
    d = heads * HG_DH
    tile = PROJ_TILE
    return pl.pallas_call(
        _hg_guard_kernel,
        grid=(n_prompt // tile,),
        in_specs=[pl.BlockSpec((tile, d), lambda i: (i, 1)),
                  pl.BlockSpec((1, d), lambda i: (0, 0))],
        out_specs=pl.BlockSpec((8, 128), lambda i: (0, 0)),
        out_shape=jax.ShapeDtypeStruct((8, 128), F32),
        compiler_params=_params("arbitrary"),
        name="hg_guard",
    )(proj, lb)


def _hg_prompt_kernel(exact_leaf, q_ref, f_ref, i_ref, g_ref, lb_ref, gn_ref, pm_ref, o_ref, s_out_ref, s_ref):
    c = HG_CHUNK
    seq = q_ref.shape[0]
    n_heads = q_ref.shape[1] // HG_DH
    ti = lax.broadcasted_iota(I32, (c, c), 0)
    si = lax.broadcasted_iota(I32, (c, c), 1)
    leaf_shift = int(math.log2(HG_LEAF))
    mask_leaf = (si <= ti) & ((ti >> leaf_shift) == (si >> leaf_shift))
    level_masks = []
    for half in HG_LEVELS:
        sh = int(math.log2(2 * half))
        level_masks.append(((ti >> sh) == (si >> sh)) & ((ti & (2 * half - 1)) >= half) & ((si & (2 * half - 1)) < half))
    eye = ti == si
    leaf_pos = lax.broadcasted_iota(I32, (c, 1), 0) & (HG_LEAF - 1)
    s_ref[...] = jnp.zeros_like(s_ref)

    width = n_heads * HG_DH
    head_cols = [slice(hh * HG_DH, (hh + 1) * HG_DH) for hh in range(n_heads)]

    def body(ci, carry):
        rows = pl.ds(pl.multiple_of(ci * c, c), c)
        qh, kk, lf = _hg_gates(q_ref[rows, :].astype(F32), f_ref[rows, :].astype(F32), lb_ref[...])
        sums = _dot(pm_ref[...], jnp.concatenate(_split2(lf), axis=1))
        sums = sums[:, :width] + sums[:, width:]
        b = sums[0:c]
        if exact_leaf:
            a = [jnp.zeros((c, c), F32) for _ in head_cols]
            vf = i_ref[rows, :].astype(F32)
            o_leaf = [jnp.zeros((c, HG_DH), F32) for _ in head_cols]
            for dist in range(HG_LEAF):
                if dist == 0:
                    k_s, b_s, v_s = kk, b, vf
                else:
                    k_s, b_s, v_s = pltpu.roll(kk, dist, 0), pltpu.roll(b, dist, 0), pltpu.roll(vf, dist, 0)
                pair = qh * k_s * jnp.exp(jnp.minimum(b - b_s, 0.0))
                for hh, hc in enumerate(head_cols):
                    w = jnp.sum(pair[:, hc], axis=1, keepdims=True)
                    o_leaf[hh] = o_leaf[hh] + jnp.where(leaf_pos >= dist, w, 0.0) * v_s[:, hc]
        else:
            d_leaf = sums[c:2 * c]
            q_f = (qh * jnp.exp(d_leaf)).astype(BF16)
            k_f = (kk * jnp.exp(-d_leaf)).astype(BF16)
            a = [jnp.where(mask_leaf, _dot_nt(q_f[:, hc], k_f[:, hc]), 0.0) for hc in head_cols]
            o_leaf = [0.0 for _ in head_cols]
        for lvl, mask in enumerate(level_masks):
            w = jnp.exp(-jnp.abs(sums[(2 + lvl) * c:(3 + lvl) * c]))
            q_f = (qh * w).astype(BF16)
            k_f = (kk * w).astype(BF16)
            a = [a_h + jnp.where(mask, _dot_nt(q_f[:, hc], k_f[:, hc]), 0.0) for a_h, hc in zip(a, head_cols)]
        b_last = b[c - 1:c, :]
        q_b = (qh * jnp.exp(b)).astype(BF16)
        k_b = (kk * jnp.exp(b_last - b)).astype(BF16)
        e_last = jnp.exp(b_last)
        gate = gn_ref[...] * _silu(g_ref[rows, :].astype(F32))
        for hh, hc in enumerate(head_cols):
            v = i_ref[rows, hc]
            s = s_ref[hh]
            o = _dot(a[hh].astype(BF16), v) + _dot(q_b[:, hc], s.astype(BF16)) + o_leaf[hh]
            col = jnp.sum(jnp.where(eye, e_last[:, hc], 0.0), axis=1, keepdims=True)
            s_ref[hh] = col * s + _dot_tn(k_b[:, hc], v)
            o_ref[rows, hc] = (_rms(o) * gate[:, hc]).astype(BF16)
        return carry

    lax.fori_loop(0, seq // c, body, 0)
    s_out_ref[...] = s_ref[...]


def _hg_prompt(exact_leaf, proj, lb, out_norm, *, n_batch, seq, heads):
    pm = _hg_prefix_matrix()
    hp = HG_HEADS_PER_STEP
    width = hp * HG_DH
    groups = heads // hp
    col = lambda part: (lambda b, h: (b, part * groups + h))
    return pl.pallas_call(
        functools.partial(_hg_prompt_kernel, exact_leaf),
        grid=(n_batch, groups),
        in_specs=[pl.BlockSpec((seq, width), col(0)),
                  pl.BlockSpec((seq, width), col(1)),
                  pl.BlockSpec((seq, width), col(2)),
                  pl.BlockSpec((seq, width), col(3)),
                  pl.BlockSpec((1, width), lambda b, h: (0, h)),
                  pl.BlockSpec((1, width), lambda b, h: (0, h)),
                  pl.BlockSpec(pm.shape, lambda b, h: (0, 0))],
        out_specs=[pl.BlockSpec((seq, width), lambda b, h: (b, h)),
                   pl.BlockSpec((None, hp, HG_DH, HG_DH), lambda b, h: (b, h, 0, 0))],
        out_shape=[jax.ShapeDtypeStruct((n_batch * seq, heads * HG_DH), BF16),
                   jax.ShapeDtypeStruct((n_batch, heads, HG_DH, HG_DH), F32)],
        scratch_shapes=[pltpu.VMEM((hp, HG_DH, HG_DH), F32)],
        compiler_params=_params("arbitrary", "arbitrary"),
        name="hg_prompt",
    )(proj, proj, proj, proj, lb, out_norm, pm)


def _hg_sample_kernel(steps, heads, q_ref, f_ref, i_ref, g_ref, lb_ref, gn_ref, s_in_ref, o_ref, s_out_ref):
    rows = q_ref.shape[0]
    shift = int(math.log2(steps))
    ri = lax.broadcasted_iota(I32, (rows, rows), 0)
    ci = lax.broadcasted_iota(I32, (rows, rows), 1)
    prefix = (((ri >> shift) == (ci >> shift)) & (ci <= ri)).astype(BF16)
    rid = lax.broadcasted_iota(I32, (rows, 1), 0)
    step = rid & (steps - 1)
    row_batch = rid >> shift
    ki = lax.broadcasted_iota(I32, (HG_DH, HG_DH), 0)
    vi = lax.broadcasted_iota(I32, (HG_DH, HG_DH), 1)
    eye = ki == vi

    def head(h):
        cols = pl.ds(pl.multiple_of(h * HG_DH, HG_DH), HG_DH)
        qh, kk, lf = _hg_gates(q_ref[:, cols].astype(F32), f_ref[:, cols].astype(F32), lb_ref[:, cols])
        v = i_ref[:, cols]
        vf = v.astype(F32)
        hi, lo = _split2(lf)
        b = _dot(prefix, jnp.concatenate([hi, lo], axis=1))
        b = b[:, :HG_DH] + b[:, HG_DH:]
        o = jnp.zeros((rows, HG_DH), F32)
        for dist in range(steps):
            if dist == 0:
                k_s, b_s, v_s = kk, b, vf
            else:
                k_s, b_s, v_s = (pltpu.roll(kk, dist, 0), pltpu.roll(b, dist, 0), pltpu.roll(vf, dist, 0))
            w = jnp.sum(qh * k_s * jnp.exp(jnp.minimum(b - b_s, 0.0)), axis=1, keepdims=True)
            o = o + jnp.where(step >= dist, w, 0.0) * v_s
        qd = (qh * jnp.exp(b)).astype(BF16)
        for j in range(rows // steps):
            s = s_in_ref[j, h]
            o = o + jnp.where(row_batch == j, _dot(qd, s.astype(BF16)), 0.0)
            b_last = b[(j + 1) * steps - 1:(j + 1) * steps, :]
            col = jnp.sum(jnp.where(eye, jnp.exp(b_last), 0.0), axis=1, keepdims=True)
            kj = jnp.where(row_batch == j, kk * jnp.exp(jnp.minimum(b_last - b, 0.0)), 0.0).astype(BF16)
            s_out_ref[j, h] = col * s + _dot_tn(kj, v)
        o_ref[:, cols] = (_rms(o) * gn_ref[:, cols] * _silu(g_ref[:, cols].astype(F32))).astype(BF16)

    def group(gi, carry):
        for hh in range(HG_HEADS_PER_STEP):
            head(gi * HG_HEADS_PER_STEP + hh)
        return carry

    lax.fori_loop(0, heads // HG_HEADS_PER_STEP, group, 0)


def _hg_sample(proj, state, n_prompt, steps, heads, lb, out_norm):
    n_sample_batch = state.shape[0]
    d = heads * HG_DH
    per = SAMPLE_ROWS // steps
    base = n_prompt // SAMPLE_ROWS
    col = lambda part: (lambda b: (base + b, part))
    return pl.pallas_call(
        functools.partial(_hg_sample_kernel, steps, heads),
        grid=(n_sample_batch // per,),
        in_specs=[pl.BlockSpec((SAMPLE_ROWS, d), col(0)),
                  pl.BlockSpec((SAMPLE_ROWS, d), col(1)),
                  pl.BlockSpec((SAMPLE_ROWS, d), col(2)),
                  pl.BlockSpec((SAMPLE_ROWS, d), col(3)),
                  pl.BlockSpec((1, d), lambda b: (0, 0)),
                  pl.BlockSpec((1, d), lambda b: (0, 0)),
                  pl.BlockSpec((per, heads, HG_DH, HG_DH), lambda b: (b, 0, 0, 0))],
        out_specs=[pl.BlockSpec((SAMPLE_ROWS, d), lambda b: (b, 0)),
                   pl.BlockSpec((per, heads, HG_DH, HG_DH), lambda b: (b, 0, 0, 0))],
        out_shape=[jax.ShapeDtypeStruct((n_sample_batch * steps, d), BF16),
                   jax.ShapeDtypeStruct(state.shape, F32)],
        compiler_params=_params("arbitrary"),
        name="hg_sample",
    )(proj, proj, proj, proj, lb, out_norm, state)


def _first_max(vals, lane, width):
    m = jnp.max(vals, axis=1, keepdims=True)
    idx = jnp.min(jnp.where(vals == m, lane, width), axis=1, keepdims=True)
    return m, idx


def _route(logits, bias):
    neg = -jnp.inf
    lane = lax.broadcasted_iota(I32, logits.shape, 1)
    group = lane >> GROUP_SHIFT
    scores = jax.nn.sigmoid(logits)
    biased = scores + bias
    best = sel = None
    for gi in range(N_GROUPS):
        vals = jnp.where(group == gi, biased, neg)
        m1, i1 = _first_max(vals, lane, N_EXPERTS)
        m2 = jnp.max(jnp.where(lane == i1, neg, vals), axis=1, keepdims=True)
        total = m1 + m2
        if gi == 0:
            best, sel = total, jnp.zeros_like(i1)
        else:
            better = total > best
            sel = jnp.where(better, gi, sel)
            best = jnp.where(better, total, best)
    vals = jnp.where(group == sel, biased, neg)
    _, e1 = _first_max(vals, lane, N_EXPERTS)
    _, e2 = _first_max(jnp.where(lane == e1, neg, vals), lane, N_EXPERTS)
    w1 = jnp.sum(jnp.where(lane == e1, scores, 0.0), axis=1, keepdims=True)
    w2 = jnp.sum(jnp.where(lane == e2, scores, 0.0), axis=1, keepdims=True)
    tot = w1 + w2
    return e1, e2, w1 / tot, w2 / tot


def _mixout_kernel(prompt_tiles, o_p, o_s, w_ref, x_p, x_s, g1_ref, g2_ref, gm_p, gm_s, sh_p, sh_s, sc_p, sc_s,
                   rw_ref, rb_ref, xo_p, xo_s, h_ref, ridx_ref, rwt_ref, cnt_ref, carry_ref, y_ref):
    i = pl.program_id(0)
    tm = x_p.shape[0]

    @pl.when(i == 0)
    def _():
        carry_ref[...] = jnp.zeros_like(carry_ref)

    def tile(sample):
        o_ref, x_ref, xo_ref = (o_s, x_s, xo_s) if sample else (o_p, x_p, xo_p)
        y_ref[...] = _dot(o_ref[...], w_ref[...])

        def chunk(rows, _):
            x = x_ref[rows, :] + _mod_rows(sample, gm_p, gm_s, rows) * (_rms(y_ref[rows, :]) * g1_ref[...])
            xo_ref[rows, :] = x
            h_ref[rows, :] = (_rms(x) * g2_ref[...] * (1.0 + _mod_rows(sample, sc_p, sc_s, rows))
                              + _mod_rows(sample, sh_p, sh_s, rows))

        _row_chunks(tm, chunk)

    _by_group(i >= prompt_tiles, tile)

    h1, h2 = _split2(h_ref[...])
    w1, w2 = _split2(rw_ref[...])
    first = _dot(h1, jnp.concatenate([w1, w2], axis=1))
    logits = first[:, :N_EXPERTS] + (first[:, N_EXPERTS:] + _dot(h2, w1))
    e1, e2, p1, p2 = _route(logits, rb_ref[...])

    lane = lax.broadcasted_iota(I32, (tm, N_EXPERTS), 1)
    hot1, hot2 = lane == e1, lane == e2
    onehot = (hot1 | hot2).astype(BF16)
    ti = lax.broadcasted_iota(I32, (tm, tm), 0)
    si = lax.broadcasted_iota(I32, (tm, tm), 1)
    before = _dot((si < ti).astype(BF16), onehot) + carry_ref[...]
    r1 = jnp.sum(jnp.where(hot1, before, 0.0), axis=1, keepdims=True).astype(I32)
    r2 = jnp.sum(jnp.where(hot2, before, 0.0), axis=1, keepdims=True).astype(I32)
    carry = carry_ref[...] + jnp.sum(onehot.astype(F32), axis=0, keepdims=True)
    carry_ref[...] = carry

    wide = lax.broadcasted_iota(I32, (tm, 128), 1)
    ridx_ref[...] = jnp.where(wide == 0, e1, jnp.where(wide == 1, e2, jnp.where(wide == 2, r1, r2)))
    rwt_ref[...] = jnp.where(wide == 0, p1, p2)
    cnt_ref[...] = jnp.zeros_like(cnt_ref)
    cnt_ref[0:1, 0:N_EXPERTS] = carry


def _mixout(tok, o_prompt, o_sample, w_out, x_p, x_s, g1, g2, mod_p, mod_s, router_w, router_b):
    d = tok.d
    v = o_prompt.shape[1]
    return pl.pallas_call(
        functools.partial(_mixout_kernel, tok.prompt_tiles),
        grid=(tok.tiles,),
        in_specs=tok.split_specs(v)
                 + [pl.BlockSpec((v, d), lambda i: (0, 0), pipeline_mode=pl.Buffered(1))]
                 + tok.split_specs(d) + [_const_spec((1, d)), _const_spec((1, d))]
                 + tok.mod_specs(2) + tok.mod_specs(3) + tok.mod_specs(4)
                 + [_const_spec((d, N_EXPERTS)), _const_spec((1, N_EXPERTS))],
        out_specs=tok.split_specs(d, out=True) + [tok.row_spec(d), tok.row_spec(128), tok.row_spec(128),
                                        _const_spec((8, 128))],
        out_shape=tok.split_shapes(d, F32) + [jax.ShapeDtypeStruct((tok.n, d), F32),
                                              jax.ShapeDtypeStruct((tok.n, 128), I32),
                                              jax.ShapeDtypeStruct((tok.n, 128), F32),
                                              jax.ShapeDtypeStruct((8, 128), F32)],
        scratch_shapes=[pltpu.VMEM((1, N_EXPERTS), F32), pltpu.VMEM((TOKEN_TILE, d), F32)],
        compiler_params=_params("arbitrary"),
        name="mix_out",
    )(o_prompt, o_sample, w_out, x_p, x_s, g1, g2, mod_p, mod_s, mod_p, mod_s, mod_p, mod_s, router_w, router_b)


def _expert_plan(ridx, counts, n_tiles):
    n = ridx.shape[0]
    expert = ridx[:, 0:2]
    rank = ridx[:, 2:4]
    cnt = counts[0, :N_EXPERTS].astype(I32)
    padded = ((cnt + EXPERT_TILE - 1) // EXPERT_TILE) * EXPERT_TILE
    ends = jnp.cumsum(padded)
    starts = ends - padded
    pos = starts[expert] + rank
    tile = jnp.arange(n_tiles, dtype=I32)
    tile_start = tile * EXPERT_TILE
    used = tile_start < ends[-1]
    last_used = jnp.maximum(ends[-1] // EXPERT_TILE - 1, 0)
    tile_expert = jnp.sum((tile_start[:, None] >= ends[None, :]).astype(I32), axis=1)
    tile_expert = jnp.minimum(jnp.where(used, tile_expert, tile_expert[last_used]), N_EXPERTS - 1)
    valid = jnp.clip(cnt[tile_expert] - (tile_start - starts[tile_expert]), 0, EXPERT_TILE)
    valid = jnp.where(used, valid, 0)
    pos = pos.reshape(n // TOKEN_TILE, TOKEN_TILE, 2).transpose(0, 2, 1)
    sub = jnp.arange(EXPERT_F_SPLIT, dtype=I32)[None, :]
    snake = jnp.where((tile[:, None] & 1) == 0, sub, EXPERT_F_SPLIT - 1 - sub)
    part = jnp.where(used[:, None], snake, snake[last_used, EXPERT_F_SPLIT - 1]).reshape(-1)
    block = jnp.repeat(tile_expert, EXPERT_F_SPLIT) * EXPERT_F_SPLIT + part
    fresh = jnp.concatenate([jnp.ones((1,), I32), (block[1:] != block[:-1]).astype(I32)])
    return tile_expert, valid, jnp.minimum(tile, last_used), part, fresh, pos


def _row_copies(groups, copy, whole):
    def body(g, carry):
        for j in range(SUBLANES):
            copy(g, j).start()
        return carry
    lax.fori_loop(0, groups, body, 0)
    whole.wait()


def _dispatch_kernel(pos_ref, h_ref, xs_in, xs_hbm, sem):
    del xs_in
    groups = h_ref.shape[0]
    for choice in range(2):
        def row(g, j, choice=choice):
            p = pos_ref[0, choice, g * SUBLANES + j]
            return pltpu.make_async_copy(h_ref.at[g, pl.ds(j, 1)],
                                         xs_hbm.at[p >> 3, pl.ds(p & (SUBLANES - 1), 1)], sem.at[choice])
        whole = pltpu.make_async_copy(h_ref, xs_hbm.at[pl.ds(0, groups)], sem.at[choice])
        _row_copies(groups, row, whole)


def _dispatch(h, pos, n_rows):
    n, d = h.shape
    tile_groups = TOKEN_TILE // SUBLANES
    return pl.pallas_call(
        _dispatch_kernel,
        grid=(n // TOKEN_TILE,),
        in_specs=[pl.BlockSpec((1, 2, TOKEN_TILE), lambda i: (i, 0, 0), memory_space=pltpu.SMEM),
                  pl.BlockSpec((tile_groups, SUBLANES, d), lambda i: (i, 0, 0)),
                  pl.BlockSpec(memory_space=pl.ANY)],
        out_specs=pl.BlockSpec(memory_space=pl.ANY),
        out_shape=jax.ShapeDtypeStruct((n_rows // SUBLANES, SUBLANES, d), F32),
        scratch_shapes=[pltpu.SemaphoreType.DMA((2,))],
        input_output_aliases={2: 0},
        compiler_params=_params("arbitrary"),
        name="dispatch",
    )(pos, h.reshape(n // SUBLANES, SUBLANES, d), jnp.zeros((n_rows // SUBLANES, SUBLANES, d), F32))


def _expert_kernel(te_ref, nv_ref, tin_ref, part_ref, fresh_ref, x_ref, wg_ref, wu_ref, wd_ref, y_ref,
                   wg_bf, wu_bf, wd_bf):
    del te_ref, tin_ref, part_ref
    n = nv_ref[pl.program_id(0)]
    step = pl.program_id(1)
    rows = x_ref.shape[0] * SUBLANES

    @pl.when((n > 0) & (fresh_ref[pl.program_id(0) * EXPERT_F_SPLIT + step] > 0))
    def _():
        wg_bf[...] = wg_ref[...].astype(BF16)
        wu_bf[...] = wu_ref[...].astype(BF16)
        wd_bf[...] = wd_ref[...].astype(BF16)

    @pl.when(n > 0)
    def _():
        x = x_ref[...].reshape(rows, x_ref.shape[2]).astype(BF16)
        a = _dot(x, wg_bf[...])
        u = _dot(x, wu_bf[...])
        y = _dot((_silu(a) * u).astype(BF16), wd_bf[...]).reshape(y_ref.shape)

        @pl.when(step == 0)
        def _():
            y_ref[...] = y

        @pl.when(step > 0)
        def _():
            y_ref[...] += y

    @pl.when((n == 0) & (step == 0))
    def _():
        y_ref[...] = jnp.zeros_like(y_ref)


def _experts(xs, plan, layer, wg, wu, wd):
    d, f = wg.shape[2], wg.shape[3]
    fs = f // EXPERT_F_SPLIT
    tile_expert, valid, tile_in, part, fresh = plan[:5]
    n_tiles = tile_expert.shape[0]
    tile_groups = EXPERT_TILE // SUBLANES
    which = lambda i, s, part: part[i * EXPERT_F_SPLIT + s]
    grid_spec = pltpu.PrefetchScalarGridSpec(
        num_scalar_prefetch=5,
        grid=(n_tiles, EXPERT_F_SPLIT),
        in_specs=[pl.BlockSpec((tile_groups, SUBLANES, d), lambda i, s, te, nv, tin, part, fr: (tin[i], 0, 0)),
                  pl.BlockSpec((None, None, d, fs),
                               lambda i, s, te, nv, tin, part, fr: (layer, te[i], 0, which(i, s, part))),
                  pl.BlockSpec((None, None, d, fs),
                               lambda i, s, te, nv, tin, part, fr: (layer, te[i], 0, which(i, s, part))),
                  pl.BlockSpec((None, None, fs, d),
                               lambda i, s, te, nv, tin, part, fr: (layer, te[i], which(i, s, part), 0))],
        out_specs=pl.BlockSpec((tile_groups, SUBLANES, d), lambda i, s, te, nv, tin, part, fr: (i, 0, 0)),
        scratch_shapes=[pltpu.VMEM((d, fs), BF16), pltpu.VMEM((d, fs), BF16), pltpu.VMEM((fs, d), BF16)],
    )
    return pl.pallas_call(
        _expert_kernel,
        grid_spec=grid_spec,
        out_shape=jax.ShapeDtypeStruct((n_tiles * tile_groups, SUBLANES, d), F32),
        compiler_params=_params("arbitrary", "arbitrary"),
        name="experts",
    )(tile_expert, valid, tile_in, part, fresh, xs, wg, wu, wd)


def _by_group(is_sample, body):
    @pl.when(is_sample)
    def _():
        body(True)

    @pl.when(jnp.logical_not(is_sample))
    def _():
        body(False)


def _row_chunks(n_rows, fn):
    per = ROW_CHUNK // SUBLANES

    def body(ci, carry):
        fn(pl.ds(pl.multiple_of(ci * ROW_CHUNK, ROW_CHUNK), ROW_CHUNK), pl.ds(pl.multiple_of(ci * per, per), per))
        return carry
    lax.fori_loop(0, n_rows // ROW_CHUNK, body, 0, unroll=ROW_CHUNK_UNROLL)


def _mod_rows(sample, p_ref, s_ref, rows):
    return s_ref[rows, :] if sample else p_ref[...]


def _moeout_kernel(prompt_tiles, with_next, pos_ref, ys_hbm, x_p, x_s, rwt_ref, g3_ref, gf_p, gf_s, *rest):
    if with_next:
        gn_ref, sh_p, sh_s, sc_p, sc_s, xo_p, xo_s, h_ref, ybuf, sem = rest
    else:
        xo_p, xo_s, ybuf, sem = rest
    groups = ybuf.shape[1]
    n_rows = groups * SUBLANES
    d = ybuf.shape[3]
    for choice in range(2):
        def row(g, j, choice=choice):
            p = pos_ref[0, choice, g * SUBLANES + j]
            return pltpu.make_async_copy(ys_hbm.at[p >> 3, pl.ds(p & (SUBLANES - 1), 1)],
                                         ybuf.at[choice, g, pl.ds(j, 1)], sem.at[choice])
        whole = pltpu.make_async_copy(ys_hbm.at[pl.ds(0, groups)], ybuf.at[choice], sem.at[choice])
        _row_copies(groups, row, whole)

    def tile(sample):
        x_ref, xo_ref = (x_s, xo_s) if sample else (x_p, xo_p)

        def chunk(rows, grp):
            rw = rwt_ref[rows, :]
            y = (rw[:, 0:1] * ybuf[0, grp].reshape(ROW_CHUNK, d)
                 + rw[:, 1:2] * ybuf[1, grp].reshape(ROW_CHUNK, d))
            x = x_ref[rows, :] + _mod_rows(sample, gf_p, gf_s, rows) * (_rms(y) * g3_ref[...])
            xo_ref[rows, :] = x
            if with_next:
                h = (_rms(x) * gn_ref[...] * (1.0 + _mod_rows(sample, sc_p, sc_s, rows))
                     + _mod_rows(sample, sh_p, sh_s, rows))
                h_ref[rows, :] = h.astype(BF16)

        _row_chunks(n_rows, chunk)

    _by_group(pl.program_id(0) >= prompt_tiles, tile)


def _moeout(tok, pos, ys, x_p, x_s, rwt, g3, mod_p, mod_s, nxt=None):
    d = tok.d
    ins = ([pl.BlockSpec((1, 2, TOKEN_TILE), lambda i: (i, 0, 0), memory_space=pltpu.SMEM),
            pl.BlockSpec(memory_space=pl.ANY)]
           + tok.split_specs(d) + [tok.row_spec(128), _const_spec((1, d))] + tok.mod_specs(5))
    args = [pos, ys, x_p, x_s, rwt, g3, mod_p, mod_s]
    scratch = [pltpu.VMEM((2, TOKEN_TILE // SUBLANES, SUBLANES, d), F32), pltpu.SemaphoreType.DMA((2,))]
    if nxt is None:
        return pl.pallas_call(
            functools.partial(_moeout_kernel, tok.prompt_tiles, False),
            grid=(tok.tiles,), in_specs=ins, out_specs=tok.split_specs(d, out=True),
            out_shape=tok.split_shapes(d, F32), scratch_shapes=scratch,
            compiler_params=_params("arbitrary"), name="moe_out_last",
        )(*args)
    gain_n, mod_pn, mod_sn = nxt
    ins = ins + [_const_spec((1, d))] + tok.mod_specs(0) + tok.mod_specs(1)
    args = args + [gain_n, mod_pn, mod_sn, mod_pn, mod_sn]
    return pl.pallas_call(
        functools.partial(_moeout_kernel, tok.prompt_tiles, True),
        grid=(tok.tiles,), in_specs=ins, out_specs=tok.split_specs(d, out=True) + [tok.row_spec(d)],
        out_shape=tok.split_shapes(d, F32) + [jax.ShapeDtypeStruct((tok.n, d), BF16)],
        scratch_shapes=scratch,
        compiler_params=_params("arbitrary"), name="moe_out_next",
    )(*args)


def kernel(x_prompt, x_sample, c_prompt, c_sample, state_ret, state_hgrn, ada_w, ada_b, norm_gains,
           ret_w_in, ret_w_out, hg_w_in, hg_w_out, hg_lower_bound, hg_out_norm, router_w, router_b,
           exp_w_gate, exp_w_up, exp_w_down):
    n_batch, seq, d = x_prompt.shape
    n_dec, steps, _ = x_sample.shape
    depth = ada_w.shape[0]
    assert depth == 2 and d % RET_DK == 0 and d % HG_DH == 0
    assert seq % RET_CHUNK == 0 and (n_dec * steps) % TOKEN_TILE == 0 and SAMPLE_ROWS % steps == 0
    ret_heads = d // RET_DK
    hg_heads = d // HG_DH
    n_prompt = n_batch * seq
    n_sample = n_dec * steps
    tok = _Tokens(n_batch, seq, n_sample, d)
    n = tok.n

    n_cond = n_batch + n_dec
    pad = (-n_cond) % 8
    c_all = jnp.concatenate([c_prompt, c_sample, jnp.zeros((pad, d), F32)], axis=0)
    mod = _ada(c_all, ada_w, ada_b)
    mods = []
    for l in range(depth):
        mod_p = mod[l, :n_batch].reshape(n_batch, 6, 1, d)
        mod_s = jnp.repeat(mod[l, n_batch:n_cond], steps, axis=0)
        mods.append((mod_p, mod_s))
    gain = lambda l, k: norm_gains[l, k].reshape(1, d)

    x_p = x_prompt.reshape(n_prompt, d)
    x_s = x_sample.reshape(n_sample, d)
    n_tiles = (2 * n + N_EXPERTS * (EXPERT_TILE - 1)) // EXPERT_TILE + 1
    rw = router_w.astype(F32)
    rb = router_b.astype(F32).reshape(1, N_EXPERTS)

    def channel_mixer(l, x_p, x_s, h, ridx, rwt, counts, nxt):
        plan = _expert_plan(ridx, counts, n_tiles)
        pos = plan[5]
        xs = _dispatch(h, pos, n_tiles * EXPERT_TILE)
        ys = _experts(xs, plan, l, exp_w_gate, exp_w_up, exp_w_down)
        return _moeout(tok, pos, ys, x_p, x_s, rwt, gain(l, 3), *mods[l], nxt=nxt)

    h = _prenorm(tok, x_p, x_s, gain(0, 0), *mods[0])
    proj = _proj(h, ret_w_in[0])
    log_gamma = jnp.log(1.0 - jnp.exp2(-5.0 - jnp.arange(ret_heads, dtype=F32)))
    log_gamma = jnp.broadcast_to(log_gamma[:, None, None], (ret_heads, 1, 128))
    cos_p, sin_p = _rope_tables(jnp.arange(seq))
    cos_s, sin_s = _rope_tables(PAST_LEN + jnp.arange(steps))
    reps = SAMPLE_ROWS // steps
    cos_s, sin_s = jnp.tile(cos_s, (reps, 1)), jnp.tile(sin_s, (reps, 1))
    o_p, ret_prompt = _ret_prompt(proj, n_batch, seq, ret_heads, cos_p, sin_p, log_gamma)
    o_s, ret_sample = _ret_sample(proj, state_ret[0], n_prompt, steps, ret_heads, cos_s, sin_s, log_gamma)
    x_p, x_s, hp, ridx, rwt, counts = _mixout(tok, o_p, o_s, ret_w_out[0].astype(BF16), x_p, x_s,
                                              gain(0, 1), gain(0, 2), *mods[0], rw, rb)
    x_p, x_s, h = channel_mixer(0, x_p, x_s, hp, ridx, rwt, counts, (gain(1, 0),) + mods[1])

    sm = jax.nn.softmax(hg_lower_bound.astype(F32), axis=0)
    lb = (jnp.cumsum(sm, axis=0) - sm[0])[1].reshape(1, d)
    proj = _proj(h, hg_w_in[0])
    out_norm = hg_out_norm[0].reshape(1, d)
    leaf_min = _hg_guard(proj, n_prompt, hg_heads, lb)[0, 0]
    o_p, hg_prompt = lax.cond(
        leaf_min > HG_LEAF_LOG_LIMIT,
        functools.partial(_hg_prompt, False, n_batch=n_batch, seq=seq, heads=hg_heads),
        functools.partial(_hg_prompt, True, n_batch=n_batch, seq=seq, heads=hg_heads),
        proj, lb, out_norm)
    o_s, hg_sample = _hg_sample(proj, state_hgrn[0], n_prompt, steps, hg_heads, lb, out_norm)
    x_p, x_s, hp, ridx, rwt, counts = _mixout(tok, o_p, o_s, hg_w_out[0].astype(BF16), x_p, x_s,
                                              gain(1, 1), gain(1, 2), *mods[1], rw, rb)
    x_p, x_s = channel_mixer(1, x_p, x_s, hp, ridx, rwt, counts, None)

    return (x_p.reshape(n_batch, seq, d), x_s.reshape(n_dec, steps, d),
            ret_prompt[None], hg_prompt[None], ret_sample[None], hg_sample[None])
```

```python
import functools
import math

import numpy as np
import jax
import jax.numpy as jnp
from jax import lax
from jax.experimental import pallas as pl
from jax.experimental.pallas import tpu as pltpu

F32, BF16, I32 = jnp.float32, jnp.bfloat16, jnp.int32

EPS = 1e-6
ROPE_BASE = 10000.0
PAST_LEN = 16384
RET_DK = 256
RET_DV = 512
HG_DH = 128
N_EXPERTS = 16
GROUP_SHIFT = 2
N_GROUPS = N_EXPERTS >> GROUP_SHIFT

TOKEN_TILE = 256
PROJ_TILE = 512
PROJ_COLS = 1024
EXPERT_TILE = 384
RET_CHUNK = 256
HG_CHUNK = 128
HG_LEAF = 16
HG_LEVELS = (64, 32, 16)
HG_HEADS_PER_STEP = 4
HG_LEAF_LOG_LIMIT = -80.0
SAMPLE_ROWS = 16
SUBLANES = 8
EXPERT_F_SPLIT = 2
RET_SAMPLE_HEADS = 2
ROW_CHUNK = 16
ROW_CHUNK_UNROLL = 4
VMEM_LIMIT = 56 * 1024 * 1024


def _params(*sem):
    return pltpu.CompilerParams(dimension_semantics=sem, vmem_limit_bytes=VMEM_LIMIT)


def _rms(x):
    return x * lax.rsqrt(jnp.mean(x * x, axis=-1, keepdims=True) + EPS)


def _silu(x):
    return x * jax.nn.sigmoid(x)


def _dot(a, b):
    return jnp.dot(a, b, preferred_element_type=F32)


def _dot_nt(a, b):
    return lax.dot_general(a, b, (((1,), (1,)), ((), ())), preferred_element_type=F32)


def _dot_tn(a, b):
    return lax.dot_general(a, b, (((0,), (0,)), ((), ())), preferred_element_type=F32)


def _split2(x):
    hi = x.astype(BF16)
    lo = (x - hi.astype(F32)).astype(BF16)
    return hi, lo


def _split3(x):
    hi = x.astype(BF16)
    r = x - hi.astype(F32)
    mid = r.astype(BF16)
    lo = (r - mid.astype(F32)).astype(BF16)
    return hi, mid, lo


def _ada_kernel(c_ref, w_ref, b_ref, o_ref):
    s = _silu(c_ref[...]).astype(BF16)
    o_ref[...] = _dot(s, w_ref[...].astype(BF16)) + b_ref[...]


def _ada(c_all, ada_w, ada_b):
    depth, d, d6 = ada_w.shape
    m = c_all.shape[0]
    tn = min(1024, d6)
    return pl.pallas_call(
        _ada_kernel,
        grid=(depth, d6 // tn),
        in_specs=[pl.BlockSpec((m, d), lambda l, j: (0, 0)),
                  pl.BlockSpec((None, d, tn), lambda l, j: (l, 0, j)),
                  pl.BlockSpec((None, 1, tn), lambda l, j: (l, 0, j))],
        out_specs=pl.BlockSpec((None, m, tn), lambda l, j: (l, 0, j)),
        out_shape=jax.ShapeDtypeStruct((depth, m, d6), F32),
        compiler_params=_params("arbitrary", "arbitrary"),
        name="ada_mod",
    )(c_all, ada_w, ada_b.reshape(depth, 1, d6))


class _Tokens:
    def __init__(self, n_prompt_batch, seq, n_sample, d):
        self.d = d
        self.n_prompt = n_prompt_batch * seq
        self.n = self.n_prompt + n_sample
        self.prompt_tiles = self.n_prompt // TOKEN_TILE
        self.tiles_per_batch = seq // TOKEN_TILE
        self.n_batch = n_prompt_batch
        self.tiles = self.n // TOKEN_TILE

    def mod_specs(self, comp):
        d, tpb, nb, npt = self.d, self.tiles_per_batch, self.n_batch, self.prompt_tiles
        return [pl.BlockSpec((None, None, 1, d), lambda i: (jnp.minimum(i // tpb, nb - 1), comp, 0, 0)),
                pl.BlockSpec((TOKEN_TILE, d), lambda i: (jnp.maximum(i - npt, 0), comp),
                             pipeline_mode=pl.Buffered(1))]

    def row_spec(self, width):
        return pl.BlockSpec((TOKEN_TILE, width), lambda i: (i, 0))

    def split_specs(self, width, out=False):
        npt = self.prompt_tiles
        mode = {} if out else dict(pipeline_mode=pl.Buffered(1))
        return [pl.BlockSpec((TOKEN_TILE, width), lambda i: (jnp.minimum(i, npt - 1), 0)),
                pl.BlockSpec((TOKEN_TILE, width), lambda i: (jnp.maximum(i - npt, 0), 0), **mode)]

    def split_shapes(self, width, dtype):
        return [jax.ShapeDtypeStruct((self.n_prompt, width), dtype),
                jax.ShapeDtypeStruct((self.n - self.n_prompt, width), dtype)]


def _const_spec(shape):
    return pl.BlockSpec(shape, lambda i: (0,) * len(shape))


def _pick(is_sample, p_ref, s_ref):
    return jnp.where(is_sample, s_ref[...], p_ref[...])


def _store_split(is_sample, p_ref, s_ref, val):
    @pl.when(is_sample)
    def _():
        s_ref[...] = val

    @pl.when(jnp.logical_not(is_sample))
    def _():
        p_ref[...] = val


def _prenorm_kernel(prompt_tiles, x_p, x_s, g_ref, sh_p, sh_s, sc_p, sc_s, h_ref):
    is_sample = pl.program_id(0) >= prompt_tiles
    x = _pick(is_sample, x_p, x_s)
    h = _rms(x) * g_ref[...] * (1.0 + _pick(is_sample, sc_p, sc_s)) + _pick(is_sample, sh_p, sh_s)
    h_ref[...] = h.astype(BF16)


def _prenorm(tok, x_p, x_s, gain, mod_p, mod_s):
    d = tok.d
    return pl.pallas_call(
        functools.partial(_prenorm_kernel, tok.prompt_tiles),
        grid=(tok.tiles,),
        in_specs=tok.split_specs(d) + [_const_spec((1, d))] + tok.mod_specs(0) + tok.mod_specs(1),
        out_specs=tok.row_spec(d),
        out_shape=jax.ShapeDtypeStruct((tok.n, d), BF16),
        compiler_params=_params("arbitrary"),
        name="prenorm",
    )(x_p, x_s, gain, mod_p, mod_s, mod_p, mod_s)


def _proj_kernel(h_ref, w_ref, o_ref, wb_ref):
    @pl.when(pl.program_id(1) == 0)
    def _():
        wb_ref[...] = w_ref[...].astype(BF16)

    o_ref[...] = _dot(h_ref[...], wb_ref[...]).astype(BF16)


def _proj(h, w):
    n, d = h.shape
    p = w.shape[1]
    tm = PROJ_TILE if n % PROJ_TILE == 0 else TOKEN_TILE
    tn = min(PROJ_COLS, p)
    return pl.pallas_call(
        _proj_kernel,
        grid=(p // tn, n // tm),
        in_specs=[pl.BlockSpec((tm, d), lambda j, i: (i, 0)),
                  pl.BlockSpec((d, tn), lambda j, i: (0, j))],
        out_specs=pl.BlockSpec((tm, tn), lambda j, i: (i, j)),
        out_shape=jax.ShapeDtypeStruct((n, p), BF16),
        scratch_shapes=[pltpu.VMEM((d, tn), BF16)],
        compiler_params=_params("arbitrary", "arbitrary"),
        name="in_proj",
    )(h, w)


def _rope_tables(pos):
    half = RET_DK // 2
    theta = 1.0 / (ROPE_BASE ** jnp.linspace(0.0, 1.0, half, dtype=F32))
    ang = pos.astype(F32)[:, None] * theta[None, :]
    cos, sin = jnp.cos(ang), jnp.sin(ang)
    return (jnp.repeat(cos, 2, axis=1),
            jnp.stack([-sin, sin], axis=-1).reshape(pos.shape[0], RET_DK))


def _rot(x, cos, sin_signed):
    lane = lax.broadcasted_iota(I32, x.shape, 1)
    width = x.shape[1]
    nbr = jnp.where((lane & 1) == 0, pltpu.roll(x, width - 1, 1), pltpu.roll(x, 1, 1))
    return x * cos + nbr * sin_signed


def _gate_out(o, g_ref_val):
    return (_rms(o) * _silu(g_ref_val.astype(F32))).astype(BF16)


def _ret_prompt_kernel(q_ref, k_ref, v_ref, g_ref, cos_ref, sin_ref, lg_ref, o_ref, s_out_ref, s_ref):
    c = RET_CHUNK
    seq = q_ref.shape[0]
    lg = lg_ref[0:1, 0:1]
    ti = lax.broadcasted_iota(I32, (c, c), 0)
    si = lax.broadcasted_iota(I32, (c, c), 1)
    decay = jnp.where(ti >= si, jnp.exp((ti - si).astype(F32) * lg), 0.0)
    tcol = lax.broadcasted_iota(I32, (c, 1), 0).astype(F32)
    dq = jnp.exp((tcol + 1.0) * lg)
    dk = jnp.exp((float(c - 1) - tcol) * lg)
    dchunk = jnp.exp(float(c) * lg)
    s_ref[...] = jnp.zeros_like(s_ref)

    def body(ci, carry):
        r0 = pl.multiple_of(ci * c, c)
        rows = pl.ds(r0, c)
        cos, sin = cos_ref[rows, :], sin_ref[rows, :]
        q = _rot(q_ref[rows, :].astype(F32), cos, sin)
        k = _rot(k_ref[rows, :].astype(F32), cos, sin) * (RET_DK ** -0.5)
        v = v_ref[rows, :]
        s = s_ref[...]
        scores = _dot_nt(q.astype(BF16), k.astype(BF16)) * decay
        o = _dot(scores.astype(BF16), v) + _dot((q * dq).astype(BF16), s.astype(BF16))
        s_ref[...] = dchunk * s + _dot_tn((k * dk).astype(BF16), v)
        o_ref[rows, :] = _gate_out(o, g_ref[rows, :])
        return carry

    lax.fori_loop(0, seq // c, body, 0)
    s_out_ref[...] = s_ref[...]


def _ret_prompt(proj, n_batch, seq, heads, cos, sin, log_gamma):
    qk_blocks = heads
    return pl.pallas_call(
        _ret_prompt_kernel,
        grid=(n_batch, heads),
        in_specs=[pl.BlockSpec((seq, RET_DK), lambda b, h: (b, h)),
                  pl.BlockSpec((seq, RET_DK), lambda b, h: (b, qk_blocks + h)),
                  pl.BlockSpec((seq, RET_DV), lambda b, h: (b, heads + h)),
                  pl.BlockSpec((seq, RET_DV), lambda b, h: (b, 2 * heads + h)),
                  pl.BlockSpec((seq, RET_DK), lambda b, h: (0, 0)),
                  pl.BlockSpec((seq, RET_DK), lambda b, h: (0, 0)),
                  pl.BlockSpec((None, 1, 128), lambda b, h: (h, 0, 0))],
        out_specs=[pl.BlockSpec((seq, RET_DV), lambda b, h: (b, h)),
                   pl.BlockSpec((None, None, RET_DK, RET_DV), lambda b, h: (b, h, 0, 0))],
        out_shape=[jax.ShapeDtypeStruct((n_batch * seq, heads * RET_DV), BF16),
                   jax.ShapeDtypeStruct((n_batch, heads, RET_DK, RET_DV), F32)],
        scratch_shapes=[pltpu.VMEM((RET_DK, RET_DV), F32)],
        compiler_params=_params("arbitrary", "arbitrary"),
        name="ret_prompt",
    )(proj, proj, proj, proj, cos, sin, log_gamma)


def _ret_sample_kernel(steps, q_ref, k_ref, v_ref, g_ref, cos_ref, sin_ref, lg_ref, s_in_ref, o_ref, s_out_ref):
    rows = q_ref.shape[0]
    shift = int(math.log2(steps))
    cos, sin = cos_ref[...], sin_ref[...]
    ri = lax.broadcasted_iota(I32, (rows, rows), 0)
    ci = lax.broadcasted_iota(I32, (rows, rows), 1)
    pair = ((ri >> shift) == (ci >> shift)) & (ri >= ci)
    rid = lax.broadcasted_iota(I32, (rows, 1), 0)
    step = (rid & (steps - 1)).astype(F32)
    row_batch = rid >> shift
    for hh in range(RET_SAMPLE_HEADS):
        kc = slice(hh * RET_DK, (hh + 1) * RET_DK)
        vc = slice(hh * RET_DV, (hh + 1) * RET_DV)
        lg = lg_ref[hh, 0:1, 0:1]
        q = _rot(q_ref[:, kc].astype(F32), cos, sin)
        k = _rot(k_ref[:, kc].astype(F32), cos, sin) * (RET_DK ** -0.5)
        v = v_ref[:, vc]
        decay = jnp.where(pair, jnp.exp((ri - ci).astype(F32) * lg), 0.0)
        o = _dot((_dot_nt(q.astype(BF16), k.astype(BF16)) * decay).astype(BF16), v)
        qd = (q * jnp.exp((step + 1.0) * lg)).astype(BF16)
        kd = k * jnp.exp((float(steps - 1) - step) * lg)
        dall = jnp.exp(float(steps) * lg)
        for j in range(rows // steps):
            s = s_in_ref[j, hh]
            o = o + jnp.where(row_batch == j, _dot(qd, s.astype(BF16)), 0.0)
            kj = jnp.where(row_batch == j, kd, 0.0).astype(BF16)
            s_out_ref[j, hh] = dall * s + _dot_tn(kj, v)
        o_ref[:, vc] = _gate_out(o, g_ref[:, vc])


def _ret_sample(proj, state, n_prompt, steps, heads, cos, sin, log_gamma):
    n_sample_batch = state.shape[0]
    per = SAMPLE_ROWS // steps
    base = n_prompt // SAMPLE_ROWS
    hs = RET_SAMPLE_HEADS
    groups = heads // hs
    kw, vw = hs * RET_DK, hs * RET_DV
    return pl.pallas_call(
        functools.partial(_ret_sample_kernel, steps),
        grid=(n_sample_batch // per, groups),
        in_specs=[pl.BlockSpec((SAMPLE_ROWS, kw), lambda b, h: (base + b, h)),
                  pl.BlockSpec((SAMPLE_ROWS, kw), lambda b, h: (base + b, groups + h)),
                  pl.BlockSpec((SAMPLE_ROWS, vw), lambda b, h: (base + b, groups + h)),
                  pl.BlockSpec((SAMPLE_ROWS, vw), lambda b, h: (base + b, 2 * groups + h)),
                  pl.BlockSpec((SAMPLE_ROWS, RET_DK), lambda b, h: (0, 0)),
                  pl.BlockSpec((SAMPLE_ROWS, RET_DK), lambda b, h: (0, 0)),
                  pl.BlockSpec((hs, 1, 128), lambda b, h: (h, 0, 0)),
                  pl.BlockSpec((per, hs, RET_DK, RET_DV), lambda b, h: (b, h, 0, 0))],
        out_specs=[pl.BlockSpec((SAMPLE_ROWS, vw), lambda b, h: (b, h)),
                   pl.BlockSpec((per, hs, RET_DK, RET_DV), lambda b, h: (b, h, 0, 0))],
        out_shape=[jax.ShapeDtypeStruct((n_sample_batch * steps, heads * RET_DV), BF16),
                   jax.ShapeDtypeStruct(state.shape, F32)],
        compiler_params=_params("arbitrary", "arbitrary"),
        name="ret_sample",
    )(proj, proj, proj, proj, cos, sin, log_gamma, state)


def _hg_prefix_matrix():
    c = HG_CHUNK
    t = np.arange(c)[:, None]
    s = np.arange(c)[None, :]
    le = (s <= t).astype(np.float32)
    mats = [le, le * ((s // HG_LEAF) == (t // HG_LEAF))]
    for half in HG_LEVELS:
        mid = (t // (2 * half)) * (2 * half) + half - 1
        mats.append(le - (s <= mid).astype(np.float32))
    return jnp.asarray(np.concatenate(mats, axis=0), dtype=BF16)


def _hg_gates(q, f, lb):
    forget = lb + (1.0 - lb) * jax.nn.sigmoid(f)
    return _silu(q), 1.0 - forget, jnp.log(forget)


def _hg_guard_kernel(f_ref, lb_ref, o_ref):
    @pl.when(pl.program_id(0) == 0)
    def _():
        o_ref[...] = jnp.zeros_like(o_ref)

    rows, width = f_ref.shape
    lb = lb_ref[...]

    def leaf(li, lowest):
        r = pl.ds(pl.multiple_of(li * HG_LEAF, HG_LEAF), HG_LEAF)
        lf = jnp.log(lb + (1.0 - lb) * jax.nn.sigmoid(f_ref[r, :].astype(F32)))
        return jnp.minimum(lowest, jnp.sum(lf, axis=0, keepdims=True))

    lowest = lax.fori_loop(0, rows // HG_LEAF, leaf, jnp.zeros((1, width), F32), unroll=ROW_CHUNK_UNROLL)
    o_ref[...] = jnp.minimum(o_ref[...], jnp.min(lowest))


def _hg_guard(proj, n_prompt, heads, lb):
    d = heads * HG_DH
    tile = PROJ_TILE
    return pl.pallas_call(
        _hg_guard_kernel,
        grid=(n_prompt // tile,),
        in_specs=[pl.BlockSpec((tile, d), lambda i: (i, 1)),
                  pl.BlockSpec((1, d), lambda i: (0, 0))],
        out_specs=pl.BlockSpec((8, 128), lambda i: (0, 0)),
        out_shape=jax.ShapeDtypeStruct((8, 128), F32),
        compiler_params=_params("arbitrary"),
        name="hg_guard",
    )(proj, lb)


def _hg_prompt_kernel(exact_leaf, q_ref, f_ref, i_ref, g_ref, lb_ref, gn_ref, pm_ref, o_ref, s_out_ref, s_ref):
    c = HG_CHUNK
    seq = q_ref.shape[0]
    n_heads = q_ref.shape[1] // HG_DH
    ti = lax.broadcasted_iota(I32, (c, c), 0)
    si = lax.broadcasted_iota(I32, (c, c), 1)
    leaf_shift = int(math.log2(HG_LEAF))
    mask_leaf = (si <= ti) & ((ti >> leaf_shift) == (si >> leaf_shift))
    level_masks = []
    for half in HG_LEVELS:
        sh = int(math.log2(2 * half))
        level_masks.append(((ti >> sh) == (si >> sh)) & ((ti & (2 * half - 1)) >= half) & ((si & (2 * half - 1)) < half))
    eye = ti == si
    leaf_pos = lax.broadcasted_iota(I32, (c, 1), 0) & (HG_LEAF - 1)
    s_ref[...] = jnp.zeros_like(s_ref)

    width = n_heads * HG_DH
    head_cols = [slice(hh * HG_DH, (hh + 1) * HG_DH) for hh in range(n_heads)]

    def body(ci, carry):
        rows = pl.ds(pl.multiple_of(ci * c, c), c)
        qh, kk, lf = _hg_gates(q_ref[rows, :].astype(F32), f_ref[rows, :].astype(F32), lb_ref[...])
        sums = _dot(pm_ref[...], jnp.concatenate(_split2(lf), axis=1))
        sums = sums[:, :width] + sums[:, width:]
        b = sums[0:c]
        if exact_leaf:
            a = [jnp.zeros((c, c), F32) for _ in head_cols]
            vf = i_ref[rows, :].astype(F32)
            o_leaf = [jnp.zeros((c, HG_DH), F32) for _ in head_cols]
            for dist in range(HG_LEAF):
                if dist == 0:
                    k_s, b_s, v_s = kk, b, vf
                else:
                    k_s, b_s, v_s = pltpu.roll(kk, dist, 0), pltpu.roll(b, dist, 0), pltpu.roll(vf, dist, 0)
                pair = qh * k_s * jnp.exp(jnp.minimum(b - b_s, 0.0))
                for hh, hc in enumerate(head_cols):
                    w = jnp.sum(pair[:, hc], axis=1, keepdims=True)
                    o_leaf[hh] = o_leaf[hh] + jnp.where(leaf_pos >= dist, w, 0.0) * v_s[:, hc]
        else:
            d_leaf = sums[c:2 * c]
            q_f = (qh * jnp.exp(d_leaf)).astype(BF16)
            k_f = (kk * jnp.exp(-d_leaf)).astype(BF16)
            a = [jnp.where(mask_leaf, _dot_nt(q_f[:, hc], k_f[:, hc]), 0.0) for hc in head_cols]
            o_leaf = [0.0 for _ in head_cols]
        for lvl, mask in enumerate(level_masks):
            w = jnp.exp(-jnp.abs(sums[(2 + lvl) * c:(3 + lvl) * c]))
            q_f = (qh * w).astype(BF16)
            k_f = (kk * w).astype(BF16)
            a = [a_h + jnp.where(mask, _dot_nt(q_f[:, hc], k_f[:, hc]), 0.0) for a_h, hc in zip(a, head_cols)]
        b_last = b[c - 1:c, :]
        q_b = (qh * jnp.exp(b)).astype(BF16)
        k_b = (kk * jnp.exp(b_last - b)).astype(BF16)
        e_last = jnp.exp(b_last)
        gate = gn_ref[...] * _silu(g_ref[rows, :].astype(F32))
        for hh, hc in enumerate(head_cols):
            v = i_ref[rows, hc]
            s = s_ref[hh]
            o = _dot(a[hh].astype(BF16), v) + _dot(q_b[:, hc], s.astype(BF16)) + o_leaf[hh]
            col = jnp.sum(jnp.where(eye, e_last[:, hc], 0.0), axis=1, keepdims=True)
            s_ref[hh] = col * s + _dot_tn(k_b[:, hc], v)
            o_ref[rows, hc] = (_rms(o) * gate[:, hc]).astype(BF16)
        return carry

    lax.fori_loop(0, seq // c, body, 0)
    s_out_ref[...] = s_ref[...]


def _hg_prompt(exact_leaf, proj, lb, out_norm, *, n_batch, seq, heads):
    pm = _hg_prefix_matrix()
    hp = HG_HEADS_PER_STEP
    width = hp * HG_DH
    groups = heads // hp
    col = lambda part: (lambda b, h: (b, part * groups + h))
    return pl.pallas_call(
        functools.partial(_hg_prompt_kernel, exact_leaf),
        grid=(n_batch, groups),
        in_specs=[pl.BlockSpec((seq, width), col(0)),
                  pl.BlockSpec((seq, width), col(1)),
                  pl.BlockSpec((seq, width), col(2)),
                  pl.BlockSpec((seq, width), col(3)),
                  pl.BlockSpec((1, width), lambda b, h: (0, h)),
                  pl.BlockSpec((1, width), lambda b, h: (0, h)),
                  pl.BlockSpec(pm.shape, lambda b, h: (0, 0))],
        out_specs=[pl.BlockSpec((seq, width), lambda b, h: (b, h)),
                   pl.BlockSpec((None, hp, HG_DH, HG_DH), lambda b, h: (b, h, 0, 0))],
        out_shape=[jax.ShapeDtypeStruct((n_batch * seq, heads * HG_DH), BF16),
                   jax.ShapeDtypeStruct((n_batch, heads, HG_DH, HG_DH), F32)],
        scratch_shapes=[pltpu.VMEM((hp, HG_DH, HG_DH), F32)],
        compiler_params=_params("arbitrary", "arbitrary"),
        name="hg_prompt",
    )(proj, proj, proj, proj, lb, out_norm, pm)


def _hg_sample_kernel(steps, heads, q_ref, f_ref, i_ref, g_ref, lb_ref, gn_ref, s_in_ref, o_ref, s_out_ref):
    rows = q_ref.shape[0]
    shift = int(math.log2(steps))
    ri = lax.broadcasted_iota(I32, (rows, rows), 0)
    ci = lax.broadcasted_iota(I32, (rows, rows), 1)
    prefix = (((ri >> shift) == (ci >> shift)) & (ci <= ri)).astype(BF16)
    rid = lax.broadcasted_iota(I32, (rows, 1), 0)
    step = rid & (steps - 1)
    row_batch = rid >> shift
    ki = lax.broadcasted_iota(I32, (HG_DH, HG_DH), 0)
    vi = lax.broadcasted_iota(I32, (HG_DH, HG_DH), 1)
    eye = ki == vi

    def head(h):
        cols = pl.ds(pl.multiple_of(h * HG_DH, HG_DH), HG_DH)
        qh, kk, lf = _hg_gates(q_ref[:, cols].astype(F32), f_ref[:, cols].astype(F32), lb_ref[:, cols])
        v = i_ref[:, cols]
        vf = v.astype(F32)
        hi, lo = _split2(lf)
        b = _dot(prefix, jnp.concatenate([hi, lo], axis=1))
        b = b[:, :HG_DH] + b[:, HG_DH:]
        o = jnp.zeros((rows, HG_DH), F32)
        for dist in range(steps):
            if dist == 0:
                k_s, b_s, v_s = kk, b, vf
            else:
                k_s, b_s, v_s = (pltpu.roll(kk, dist, 0), pltpu.roll(b, dist, 0), pltpu.roll(vf, dist, 0))
            w = jnp.sum(qh * k_s * jnp.exp(jnp.minimum(b - b_s, 0.0)), axis=1, keepdims=True)
            o = o + jnp.where(step >= dist, w, 0.0) * v_s
        qd = (qh * jnp.exp(b)).astype(BF16)
        for j in range(rows // steps):
            s = s_in_ref[j, h]
            o = o + jnp.where(row_batch == j, _dot(qd, s.astype(BF16)), 0.0)
            b_last = b[(j + 1) * steps - 1:(j + 1) * steps, :]
            col = jnp.sum(jnp.where(eye, jnp.exp(b_last), 0.0), axis=1, keepdims=True)
            kj = jnp.where(row_batch == j, kk * jnp.exp(jnp.minimum(b_last - b, 0.0)), 0.0).astype(BF16)
            s_out_ref[j, h] = col * s + _dot_tn(kj, v)
        o_ref[:, cols] = (_rms(o) * gn_ref[:, cols] * _silu(g_ref[:, cols].astype(F32))).astype(BF16)

    def group(gi, carry):
        for hh in range(HG_HEADS_PER_STEP):
            head(gi * HG_HEADS_PER_STEP + hh)
        return carry

    lax.fori_loop(0, heads // HG_HEADS_PER_STEP, group, 0)


def _hg_sample(proj, state, n_prompt, steps, heads, lb, out_norm):
    n_sample_batch = state.shape[0]
    d = heads * HG_DH
    per = SAMPLE_ROWS // steps
    base = n_prompt // SAMPLE_ROWS
    col = lambda part: (lambda b: (base + b, part))
    return pl.pallas_call(
        functools.partial(_hg_sample_kernel, steps, heads),
        grid=(n_sample_batch // per,),
        in_specs=[pl.BlockSpec((SAMPLE_ROWS, d), col(0)),
                  pl.BlockSpec((SAMPLE_ROWS, d), col(1)),
                  pl.BlockSpec((SAMPLE_ROWS, d), col(2)),
                  pl.BlockSpec((SAMPLE_ROWS, d), col(3)),
                  pl.BlockSpec((1, d), lambda b: (0, 0)),
                  pl.BlockSpec((1, d), lambda b: (0, 0)),
                  pl.BlockSpec((per, heads, HG_DH, HG_DH), lambda b: (b, 0, 0, 0))],
        out_specs=[pl.BlockSpec((SAMPLE_ROWS, d), lambda b: (b, 0)),
                   pl.BlockSpec((per, heads, HG_DH, HG_DH), lambda b: (b, 0, 0, 0))],
        out_shape=[jax.ShapeDtypeStruct((n_sample_batch * steps, d), BF16),
                   jax.ShapeDtypeStruct(state.shape, F32)],
        compiler_params=_params("arbitrary"),
        name="hg_sample",
    )(proj, proj, proj, proj, lb, out_norm, state)


def _first_max(vals, lane, width):
    m = jnp.max(vals, axis=1, keepdims=True)
    idx = jnp.min(jnp.where(vals == m, lane, width), axis=1, keepdims=True)
    return m, idx


def _route(logits, bias):
    neg = -jnp.inf
    lane = lax.broadcasted_iota(I32, logits.shape, 1)
    group = lane >> GROUP_SHIFT
    scores = jax.nn.sigmoid(logits)
    biased = scores + bias
    best = sel = None
    for gi in range(N_GROUPS):
        vals = jnp.where(group == gi, biased, neg)
        m1, i1 = _first_max(vals, lane, N_EXPERTS)
        m2 = jnp.max(jnp.where(lane == i1, neg, vals), axis=1, keepdims=True)
        total = m1 + m2
        if gi == 0:
            best, sel = total, jnp.zeros_like(i1)
        else:
            better = total > best
            sel = jnp.where(better, gi, sel)
            best = jnp.where(better, total, best)
    vals = jnp.where(group == sel, biased, neg)
    _, e1 = _first_max(vals, lane, N_EXPERTS)
    _, e2 = _first_max(jnp.where(lane == e1, neg, vals), lane, N_EXPERTS)
    w1 = jnp.sum(jnp.where(lane == e1, scores, 0.0), axis=1, keepdims=True)
    w2 = jnp.sum(jnp.where(lane == e2, scores, 0.0), axis=1, keepdims=True)
    tot = w1 + w2
    return e1, e2, w1 / tot, w2 / tot


def _mixout_kernel(prompt_tiles, o_p, o_s, w_ref, x_p, x_s, g1_ref, g2_ref, gm_p, gm_s, sh_p, sh_s, sc_p, sc_s,
                   rw_ref, rb_ref, xo_p, xo_s, h_ref, ridx_ref, rwt_ref, cnt_ref, carry_ref, y_ref):
    i = pl.program_id(0)
    tm = x_p.shape[0]

    @pl.when(i == 0)
    def _():
        carry_ref[...] = jnp.zeros_like(carry_ref)

    def tile(sample):
        o_ref, x_ref, xo_ref = (o_s, x_s, xo_s) if sample else (o_p, x_p, xo_p)
        y_ref[...] = _dot(o_ref[...], w_ref[...])

        def chunk(rows, _):
            x = x_ref[rows, :] + _mod_rows(sample, gm_p, gm_s, rows) * (_rms(y_ref[rows, :]) * g1_ref[...])
            xo_ref[rows, :] = x
            h_ref[rows, :] = (_rms(x) * g2_ref[...] * (1.0 + _mod_rows(sample, sc_p, sc_s, rows))
                              + _mod_rows(sample, sh_p, sh_s, rows))

        _row_chunks(tm, chunk)

    _by_group(i >= prompt_tiles, tile)

    h1, h2 = _split2(h_ref[...])
    w1, w2 = _split2(rw_ref[...])
    first = _dot(h1, jnp.concatenate([w1, w2], axis=1))
    logits = first[:, :N_EXPERTS] + (first[:, N_EXPERTS:] + _dot(h2, w1))
    e1, e2, p1, p2 = _route(logits, rb_ref[...])

    lane = lax.broadcasted_iota(I32, (tm, N_EXPERTS), 1)
    hot1, hot2 = lane == e1, lane == e2
    onehot = (hot1 | hot2).astype(BF16)
    ti = lax.broadcasted_iota(I32, (tm, tm), 0)
    si = lax.broadcasted_iota(I32, (tm, tm), 1)
    before = _dot((si < ti).astype(BF16), onehot) + carry_ref[...]
    r1 = jnp.sum(jnp.where(hot1, before, 0.0), axis=1, keepdims=True).astype(I32)
    r2 = jnp.sum(jnp.where(hot2, before, 0.0), axis=1, keepdims=True).astype(I32)
    carry = carry_ref[...] + jnp.sum(onehot.astype(F32), axis=0, keepdims=True)
    carry_ref[...] = carry

    wide = lax.broadcasted_iota(I32, (tm, 128), 1)
    ridx_ref[...] = jnp.where(wide == 0, e1, jnp.where(wide == 1, e2, jnp.where(wide == 2, r1, r2)))
    rwt_ref[...] = jnp.where(wide == 0, p1, p2)
    cnt_ref[...] = jnp.zeros_like(cnt_ref)
    cnt_ref[0:1, 0:N_EXPERTS] = carry


def _mixout(tok, o_prompt, o_sample, w_out, x_p, x_s, g1, g2, mod_p, mod_s, router_w, router_b):
    d = tok.d
    v = o_prompt.shape[1]
    return pl.pallas_call(
        functools.partial(_mixout_kernel, tok.prompt_tiles),
        grid=(tok.tiles,),
        in_specs=tok.split_specs(v)
                 + [pl.BlockSpec((v, d), lambda i: (0, 0), pipeline_mode=pl.Buffered(1))]
                 + tok.split_specs(d) + [_const_spec((1, d)), _const_spec((1, d))]
                 + tok.mod_specs(2) + tok.mod_specs(3) + tok.mod_specs(4)
                 + [_const_spec((d, N_EXPERTS)), _const_spec((1, N_EXPERTS))],
        out_specs=tok.split_specs(d, out=True) + [tok.row_spec(d), tok.row_spec(128), tok.row_spec(128),
                                        _const_spec((8, 128))],
        out_shape=tok.split_shapes(d, F32) + [jax.ShapeDtypeStruct((tok.n, d), F32),
                                              jax.ShapeDtypeStruct((tok.n, 128), I32),
                                              jax.ShapeDtypeStruct((tok.n, 128), F32),
                                              jax.ShapeDtypeStruct((8, 128), F32)],
        scratch_shapes=[pltpu.VMEM((1, N_EXPERTS), F32), pltpu.VMEM((TOKEN_TILE, d), F32)],
        compiler_params=_params("arbitrary"),
        name="mix_out",
    )(o_prompt, o_sample, w_out, x_p, x_s, g1, g2, mod_p, mod_s, mod_p, mod_s, mod_p, mod_s, router_w, router_b)


def _expert_plan(ridx, counts, n_tiles):
    n = ridx.shape[0]
    expert = ridx[:, 0:2]
    rank = ridx[:, 2:4]
    cnt = counts[0, :N_EXPERTS].astype(I32)
    padded = ((cnt + EXPERT_TILE - 1) // EXPERT_TILE) * EXPERT_TILE
    ends = jnp.cumsum(padded)
    starts = ends - padded
    pos = starts[expert] + rank
    tile = jnp.arange(n_tiles, dtype=I32)
    tile_start = tile * EXPERT_TILE
    used = tile_start < ends[-1]
    last_used = jnp.maximum(ends[-1] // EXPERT_TILE - 1, 0)
    tile_expert = jnp.sum((tile_start[:, None] >= ends[None, :]).astype(I32), axis=1)
    tile_expert = jnp.minimum(jnp.where(used, tile_expert, tile_expert[last_used]), N_EXPERTS - 1)
    valid = jnp.clip(cnt[tile_expert] - (tile_start - starts[tile_expert]), 0, EXPERT_TILE)
    valid = jnp.where(used, valid, 0)
    pos = pos.reshape(n // TOKEN_TILE, TOKEN_TILE, 2).transpose(0, 2, 1)
    sub = jnp.arange(EXPERT_F_SPLIT, dtype=I32)[None, :]
    snake = jnp.where((tile[:, None] & 1) == 0, sub, EXPERT_F_SPLIT - 1 - sub)
    part = jnp.where(used[:, None], snake, snake[last_used, EXPERT_F_SPLIT - 1]).reshape(-1)
    return tile_expert, valid, jnp.minimum(tile, last_used), part, pos


def _start_rows(groups, copy):
    def body(g, carry):
        for j in range(SUBLANES):
            copy(g, j).start()
        return carry
    lax.fori_loop(0, groups, body, 0)


def _dispatch_kernel(pos_ref, h_ref, xs_in, xs_hbm, sem):
    del xs_in
    groups = h_ref.shape[0]
    for choice in range(2):
        def row(g, j, choice=choice):
            p = pos_ref[0, choice, g * SUBLANES + j]
            return pltpu.make_async_copy(h_ref.at[g, pl.ds(j, 1)],
                                         xs_hbm.at[p >> 3, pl.ds(p & (SUBLANES - 1), 1)], sem.at[choice])
        _start_rows(groups, row)
    for choice in range(2):
        pltpu.make_async_copy(h_ref, xs_hbm.at[pl.ds(0, groups)], sem.at[choice]).wait()


def _dispatch(h, pos, n_rows, buf=None):
    n, d = h.shape
    if buf is None:
        buf = jnp.zeros((n_rows // SUBLANES, SUBLANES, d), F32)
    tile_groups = TOKEN_TILE // SUBLANES
    return pl.pallas_call(
        _dispatch_kernel,
        grid=(n // TOKEN_TILE,),
        in_specs=[pl.BlockSpec((1, 2, TOKEN_TILE), lambda i: (i, 0, 0), memory_space=pltpu.SMEM),
                  pl.BlockSpec((tile_groups, SUBLANES, d), lambda i: (i, 0, 0)),
                  pl.BlockSpec(memory_space=pl.ANY)],
        out_specs=pl.BlockSpec(memory_space=pl.ANY),
        out_shape=jax.ShapeDtypeStruct((n_rows // SUBLANES, SUBLANES, d), F32),
        scratch_shapes=[pltpu.SemaphoreType.DMA((2,))],
        input_output_aliases={2: 0},
        compiler_params=_params("arbitrary"),
        name="dispatch",
    )(pos, h.reshape(n // SUBLANES, SUBLANES, d), buf)


def _expert_kernel(te_ref, nv_ref, tin_ref, part_ref, x_ref, wg_ref, wu_ref, wd_ref, y_ref):
    del te_ref, tin_ref, part_ref
    n = nv_ref[pl.program_id(0)]
    step = pl.program_id(1)
    rows = x_ref.shape[0] * SUBLANES

    @pl.when(n > 0)
    def _():
        x = x_ref[...].reshape(rows, x_ref.shape[2]).astype(BF16)
        a = _dot(x, wg_ref[...].astype(BF16))
        u = _dot(x, wu_ref[...].astype(BF16))
        y = _dot((_silu(a) * u).astype(BF16), wd_ref[...].astype(BF16)).reshape(y_ref.shape)

        @pl.when(step == 0)
        def _():
            y_ref[...] = y

        @pl.when(step > 0)
        def _():
            y_ref[...] += y

    @pl.when((n == 0) & (step == 0))
    def _():
        y_ref[...] = jnp.zeros_like(y_ref)


def _experts(xs, plan, layer, wg, wu, wd):
    d, f = wg.shape[2], wg.shape[3]
    fs = f // EXPERT_F_SPLIT
    tile_expert, valid, tile_in, part = plan[:4]
    n_tiles = tile_expert.shape[0]
    tile_groups = EXPERT_TILE // SUBLANES
    which = lambda i, s, part: part[i * EXPERT_F_SPLIT + s]
    grid_spec = pltpu.PrefetchScalarGridSpec(
        num_scalar_prefetch=4,
        grid=(n_tiles, EXPERT_F_SPLIT),
        in_specs=[pl.BlockSpec((tile_groups, SUBLANES, d), lambda i, s, te, nv, tin, part: (tin[i], 0, 0)),
                  pl.BlockSpec((None, None, d, fs),
                               lambda i, s, te, nv, tin, part: (layer, te[i], 0, which(i, s, part))),
                  pl.BlockSpec((None, None, d, fs),
                               lambda i, s, te, nv, tin, part: (layer, te[i], 0, which(i, s, part))),
                  pl.BlockSpec((None, None, fs, d),
                               lambda i, s, te, nv, tin, part: (layer, te[i], which(i, s, part), 0))],
        out_specs=pl.BlockSpec((tile_groups, SUBLANES, d), lambda i, s, te, nv, tin, part: (i, 0, 0)),
    )
    return pl.pallas_call(
        _expert_kernel,
        grid_spec=grid_spec,
        out_shape=jax.ShapeDtypeStruct((n_tiles * tile_groups, SUBLANES, d), F32),
        compiler_params=_params("arbitrary", "arbitrary"),
        name="experts",
    )(tile_expert, valid, tile_in, part, xs, wg, wu, wd)


def _by_group(is_sample, body):
    @pl.when(is_sample)
    def _():
        body(True)

    @pl.when(jnp.logical_not(is_sample))
    def _():
        body(False)


def _row_chunks(n_rows, fn):
    per = ROW_CHUNK // SUBLANES

    def body(ci, carry):
        fn(pl.ds(pl.multiple_of(ci * ROW_CHUNK, ROW_CHUNK), ROW_CHUNK), pl.ds(pl.multiple_of(ci * per, per), per))
        return carry
    lax.fori_loop(0, n_rows // ROW_CHUNK, body, 0, unroll=ROW_CHUNK_UNROLL)


def _mod_rows(sample, p_ref, s_ref, rows):
    return s_ref[rows, :] if sample else p_ref[...]


def _moeout_kernel(prompt_tiles, with_next, pos_ref, pos_next_ref, ys_hbm, x_p, x_s, rwt_ref, g3_ref, gf_p, gf_s,
                   *rest):
    if with_next:
        gn_ref, sh_p, sh_s, sc_p, sc_s, xo_p, xo_s, h_ref, ybuf, sem = rest
    else:
        xo_p, xo_s, ybuf, sem = rest
    groups = ybuf.shape[2]
    n_rows = groups * SUBLANES
    d = ybuf.shape[4]
    i = pl.program_id(0)
    slot = i & 1

    def fetch(p_ref, to_slot):
        for choice in range(2):
            def row(g, j, choice=choice):
                p = p_ref[0, choice, g * SUBLANES + j]
                return pltpu.make_async_copy(ys_hbm.at[p >> 3, pl.ds(p & (SUBLANES - 1), 1)],
                                             ybuf.at[to_slot, choice, g, pl.ds(j, 1)], sem.at[to_slot, choice])
            _start_rows(groups, row)

    @pl.when(i == 0)
    def _():
        fetch(pos_ref, 0)

    @pl.when(i + 1 < pl.num_programs(0))
    def _():
        fetch(pos_next_ref, 1 - slot)

    for choice in range(2):
        pltpu.make_async_copy(ys_hbm.at[pl.ds(0, groups)], ybuf.at[slot, choice], sem.at[slot, choice]).wait()

    def tile(sample):
        x_ref, xo_ref = (x_s, xo_s) if sample else (x_p, xo_p)

        def chunk(rows, grp):
            rw = rwt_ref[rows, :]
            y = (rw[:, 0:1] * ybuf[slot, 0, grp].reshape(ROW_CHUNK, d)
                 + rw[:, 1:2] * ybuf[slot, 1, grp].reshape(ROW_CHUNK, d))
            x = x_ref[rows, :] + _mod_rows(sample, gf_p, gf_s, rows) * (_rms(y) * g3_ref[...])
            xo_ref[rows, :] = x
            if with_next:
                h = (_rms(x) * gn_ref[...] * (1.0 + _mod_rows(sample, sc_p, sc_s, rows))
                     + _mod_rows(sample, sh_p, sh_s, rows))
                h_ref[rows, :] = h.astype(BF16)

        _row_chunks(n_rows, chunk)

    _by_group(pl.program_id(0) >= prompt_tiles, tile)


def _moeout(tok, pos, ys, x_p, x_s, rwt, g3, mod_p, mod_s, nxt=None):
    d = tok.d
    last = tok.tiles - 1
    ins = ([pl.BlockSpec((1, 2, TOKEN_TILE), lambda i: (i, 0, 0), memory_space=pltpu.SMEM),
            pl.BlockSpec((1, 2, TOKEN_TILE), lambda i: (jnp.minimum(i + 1, last), 0, 0), memory_space=pltpu.SMEM),
            pl.BlockSpec(memory_space=pl.ANY)]
           + tok.split_specs(d) + [tok.row_spec(128), _const_spec((1, d))] + tok.mod_specs(5))
    args = [pos, pos, ys, x_p, x_s, rwt, g3, mod_p, mod_s]
    scratch = [pltpu.VMEM((2, 2, TOKEN_TILE // SUBLANES, SUBLANES, d), F32), pltpu.SemaphoreType.DMA((2, 2))]
    if nxt is None:
        return pl.pallas_call(
            functools.partial(_moeout_kernel, tok.prompt_tiles, False),
            grid=(tok.tiles,), in_specs=ins, out_specs=tok.split_specs(d, out=True),
            out_shape=tok.split_shapes(d, F32), scratch_shapes=scratch,
            compiler_params=_params("arbitrary"), name="moe_out_last",
        )(*args)
    gain_n, mod_pn, mod_sn = nxt
    ins = ins + [_const_spec((1, d))] + tok.mod_specs(0) + tok.mod_specs(1)
    args = args + [gain_n, mod_pn, mod_sn, mod_pn, mod_sn]
    return pl.pallas_call(
        functools.partial(_moeout_kernel, tok.prompt_tiles, True),
        grid=(tok.tiles,), in_specs=ins, out_specs=tok.split_specs(d, out=True) + [tok.row_spec(d)],
        out_shape=tok.split_shapes(d, F32) + [jax.ShapeDtypeStruct((tok.n, d), BF16)],
        scratch_shapes=scratch,
        compiler_params=_params("arbitrary"), name="moe_out_next",
    )(*args)


def kernel(x_prompt, x_sample, c_prompt, c_sample, state_ret, state_hgrn, ada_w, ada_b, norm_gains,
           ret_w_in, ret_w_out, hg_w_in, hg_w_out, hg_lower_bound, hg_out_norm, router_w, router_b,
           exp_w_gate, exp_w_up, exp_w_down):
    n_batch, seq, d = x_prompt.shape
    n_dec, steps, _ = x_sample.shape
    depth = ada_w.shape[0]
    assert depth == 2 and d % RET_DK == 0 and d % HG_DH == 0
    assert seq % RET_CHUNK == 0 and (n_dec * steps) % TOKEN_TILE == 0 and SAMPLE_ROWS % steps == 0
    ret_heads = d // RET_DK
    hg_heads = d // HG_DH
    n_prompt = n_batch * seq
    n_sample = n_dec * steps
    tok = _Tokens(n_batch, seq, n_sample, d)
    n = tok.n

    n_cond = n_batch + n_dec
    pad = (-n_cond) % 8
    c_all = jnp.concatenate([c_prompt, c_sample, jnp.zeros((pad, d), F32)], axis=0)
    mod = _ada(c_all, ada_w, ada_b)
    mods = []
    for l in range(depth):
        mod_p = mod[l, :n_batch].reshape(n_batch, 6, 1, d)
        mod_s = jnp.repeat(mod[l, n_batch:n_cond], steps, axis=0)
        mods.append((mod_p, mod_s))
    gain = lambda l, k: norm_gains[l, k].reshape(1, d)

    x_p = x_prompt.reshape(n_prompt, d)
    x_s = x_sample.reshape(n_sample, d)
    n_tiles = (2 * n + N_EXPERTS * (EXPERT_TILE - 1)) // EXPERT_TILE + 1
    rw = router_w.astype(F32)
    rb = router_b.astype(F32).reshape(1, N_EXPERTS)

    def channel_mixer(l, x_p, x_s, h, ridx, rwt, counts, nxt, buf):
        plan = _expert_plan(ridx, counts, n_tiles)
        pos = plan[4]
        xs = _dispatch(h, pos, n_tiles * EXPERT_TILE, buf)
        ys = _experts(xs, plan, l, exp_w_gate, exp_w_up, exp_w_down)
        return _moeout(tok, pos, ys, x_p, x_s, rwt, gain(l, 3), *mods[l], nxt=nxt), xs

    h = _prenorm(tok, x_p, x_s, gain(0, 0), *mods[0])
    proj = _proj(h, ret_w_in[0])
    log_gamma = jnp.log(1.0 - jnp.exp2(-5.0 - jnp.arange(ret_heads, dtype=F32)))
    log_gamma = jnp.broadcast_to(log_gamma[:, None, None], (ret_heads, 1, 128))
    cos_p, sin_p = _rope_tables(jnp.arange(seq))
    cos_s, sin_s = _rope_tables(PAST_LEN + jnp.arange(steps))
    reps = SAMPLE_ROWS // steps
    cos_s, sin_s = jnp.tile(cos_s, (reps, 1)), jnp.tile(sin_s, (reps, 1))
    o_p, ret_prompt = _ret_prompt(proj, n_batch, seq, ret_heads, cos_p, sin_p, log_gamma)
    o_s, ret_sample = _ret_sample(proj, state_ret[0], n_prompt, steps, ret_heads, cos_s, sin_s, log_gamma)
    x_p, x_s, hp, ridx, rwt, counts = _mixout(tok, o_p, o_s, ret_w_out[0].astype(BF16), x_p, x_s,
                                              gain(0, 1), gain(0, 2), *mods[0], rw, rb)
    (x_p, x_s, h), row_buf = channel_mixer(0, x_p, x_s, hp, ridx, rwt, counts, (gain(1, 0),) + mods[1], None)

    sm = jax.nn.softmax(hg_lower_bound.astype(F32), axis=0)
    lb = (jnp.cumsum(sm, axis=0) - sm[0])[1].reshape(1, d)
    proj = _proj(h, hg_w_in[0])
    out_norm = hg_out_norm[0].reshape(1, d)
    leaf_min = _hg_guard(proj, n_prompt, hg_heads, lb)[0, 0]
    o_p, hg_prompt = lax.cond(
        leaf_min > HG_LEAF_LOG_LIMIT,
        functools.partial(_hg_prompt, False, n_batch=n_batch, seq=seq, heads=hg_heads),
        functools.partial(_hg_prompt, True, n_batch=n_batch, seq=seq, heads=hg_heads),
        proj, lb, out_norm)
    o_s, hg_sample = _hg_sample(proj, state_hgrn[0], n_prompt, steps, hg_heads, lb, out_norm)
    x_p, x_s, hp, ridx, rwt, counts = _mixout(tok, o_p, o_s, hg_w_out[0].astype(BF16), x_p, x_s,
                                              gain(1, 1), gain(1, 2), *mods[1], rw, rb)
    (x_p, x_s), _ = channel_mixer(1, x_p, x_s, hp, ridx, rwt, counts, None, row_buf)

    return (x_p.reshape(n_batch, seq, d), x_s.reshape(n_dec, steps, d),
            ret_prompt[None], hg_prompt[None], ret_sample[None], hg_sample[None])
```

```python
import functools
import math

import numpy as np
import jax
import jax.numpy as jnp
from jax import lax
from jax.experimental import pallas as pl
from jax.experimental.pallas import tpu as pltpu

F32, BF16, I32 = jnp.float32, jnp.bfloat16, jnp.int32

EPS = 1e-6
ROPE_BASE = 10000.0
PAST_LEN = 16384
RET_DK = 256
RET_DV = 512
HG_DH = 128
N_EXPERTS = 16
GROUP_SHIFT = 2
N_GROUPS = N_EXPERTS >> GROUP_SHIFT

TOKEN_TILE = 256
PROJ_TILE = 512
PROJ_COLS = 1024
EXPERT_TILE = 384
RET_CHUNK = 256
HG_CHUNK = 128
HG_LEAF = 16
HG_LEVELS = (64, 32, 16)
HG_HEADS_PER_STEP = 4
HG_LEAF_LOG_LIMIT = -80.0
SAMPLE_ROWS = 16
SUBLANES = 8
EXPERT_F_SPLIT = 4
VISIT_TILES = 3
RET_SAMPLE_HEADS = 2
ROW_CHUNK = 16
ROW_CHUNK_UNROLL = 4
VMEM_LIMIT = 56 * 1024 * 1024


def _params(*sem):
    return pltpu.CompilerParams(dimension_semantics=sem, vmem_limit_bytes=VMEM_LIMIT)


def _rms(x):
    return x * lax.rsqrt(jnp.mean(x * x, axis=-1, keepdims=True) + EPS)


def _silu(x):
    return x * jax.nn.sigmoid(x)


def _dot(a, b):
    return jnp.dot(a, b, preferred_element_type=F32)


def _dot_nt(a, b):
    return lax.dot_general(a, b, (((1,), (1,)), ((), ())), preferred_element_type=F32)


def _dot_tn(a, b):
    return lax.dot_general(a, b, (((0,), (0,)), ((), ())), preferred_element_type=F32)


def _split2(x):
    hi = x.astype(BF16)
    lo = (x - hi.astype(F32)).astype(BF16)
    return hi, lo


def _split3(x):
    hi = x.astype(BF16)
    r = x - hi.astype(F32)
    mid = r.astype(BF16)
    lo = (r - mid.astype(F32)).astype(BF16)
    return hi, mid, lo


def _ada_kernel(c_ref, w_ref, b_ref, o_ref):
    s = _silu(c_ref[...]).astype(BF16)
    o_ref[...] = _dot(s, w_ref[...].astype(BF16)) + b_ref[...]


def _ada(c_all, ada_w, ada_b):
    depth, d, d6 = ada_w.shape
    m = c_all.shape[0]
    tn = min(1024, d6)
    return pl.pallas_call(
        _ada_kernel,
        grid=(depth, d6 // tn),
        in_specs=[pl.BlockSpec((m, d), lambda l, j: (0, 0)),
                  pl.BlockSpec((None, d, tn), lambda l, j: (l, 0, j)),
                  pl.BlockSpec((None, 1, tn), lambda l, j: (l, 0, j))],
        out_specs=pl.BlockSpec((None, m, tn), lambda l, j: (l, 0, j)),
        out_shape=jax.ShapeDtypeStruct((depth, m, d6), F32),
        compiler_params=_params("arbitrary", "arbitrary"),
        name="ada_mod",
    )(c_all, ada_w, ada_b.reshape(depth, 1, d6))


class _Tokens:
    def __init__(self, n_prompt_batch, seq, n_sample, d):
        self.d = d
        self.n_prompt = n_prompt_batch * seq
        self.n = self.n_prompt + n_sample
        self.prompt_tiles = self.n_prompt // TOKEN_TILE
        self.tiles_per_batch = seq // TOKEN_TILE
        self.n_batch = n_prompt_batch
        self.tiles = self.n // TOKEN_TILE

    def mod_specs(self, comp):
        d, tpb, nb, npt = self.d, self.tiles_per_batch, self.n_batch, self.prompt_tiles
        return [pl.BlockSpec((None, None, 1, d), lambda i: (jnp.minimum(i // tpb, nb - 1), comp, 0, 0)),
                pl.BlockSpec((TOKEN_TILE, d), lambda i: (jnp.maximum(i - npt, 0), comp),
                             pipeline_mode=pl.Buffered(1))]

    def row_spec(self, width):
        return pl.BlockSpec((TOKEN_TILE, width), lambda i: (i, 0))

    def split_specs(self, width, out=False):
        npt = self.prompt_tiles
        mode = {} if out else dict(pipeline_mode=pl.Buffered(1))
        return [pl.BlockSpec((TOKEN_TILE, width), lambda i: (jnp.minimum(i, npt - 1), 0)),
                pl.BlockSpec((TOKEN_TILE, width), lambda i: (jnp.maximum(i - npt, 0), 0), **mode)]

    def split_shapes(self, width, dtype):
        return [jax.ShapeDtypeStruct((self.n_prompt, width), dtype),
                jax.ShapeDtypeStruct((self.n - self.n_prompt, width), dtype)]


def _const_spec(shape):
    return pl.BlockSpec(shape, lambda i: (0,) * len(shape))


def _pick(is_sample, p_ref, s_ref):
    return jnp.where(is_sample, s_ref[...], p_ref[...])


def _store_split(is_sample, p_ref, s_ref, val):
    @pl.when(is_sample)
    def _():
        s_ref[...] = val

    @pl.when(jnp.logical_not(is_sample))
    def _():
        p_ref[...] = val


def _prenorm_kernel(prompt_tiles, x_p, x_s, g_ref, sh_p, sh_s, sc_p, sc_s, h_ref):
    is_sample = pl.program_id(0) >= prompt_tiles
    x = _pick(is_sample, x_p, x_s)
    h = _rms(x) * g_ref[...] * (1.0 + _pick(is_sample, sc_p, sc_s)) + _pick(is_sample, sh_p, sh_s)
    h_ref[...] = h.astype(BF16)


def _prenorm(tok, x_p, x_s, gain, mod_p, mod_s):
    d = tok.d
    return pl.pallas_call(
        functools.partial(_prenorm_kernel, tok.prompt_tiles),
        grid=(tok.tiles,),
        in_specs=tok.split_specs(d) + [_const_spec((1, d))] + tok.mod_specs(0) + tok.mod_specs(1),
        out_specs=tok.row_spec(d),
        out_shape=jax.ShapeDtypeStruct((tok.n, d), BF16),
        compiler_params=_params("arbitrary"),
        name="prenorm",
    )(x_p, x_s, gain, mod_p, mod_s, mod_p, mod_s)


def _proj_kernel(h_ref, w_ref, o_ref, wb_ref):
    @pl.when(pl.program_id(1) == 0)
    def _():
        wb_ref[...] = w_ref[...].astype(BF16)

    o_ref[...] = _dot(h_ref[...], wb_ref[...]).astype(BF16)


def _proj(h, w):
    n, d = h.shape
    p = w.shape[1]
    tm = PROJ_TILE if n % PROJ_TILE == 0 else TOKEN_TILE
    tn = min(PROJ_COLS, p)
    return pl.pallas_call(
        _proj_kernel,
        grid=(p // tn, n // tm),
        in_specs=[pl.BlockSpec((tm, d), lambda j, i: (i, 0)),
                  pl.BlockSpec((d, tn), lambda j, i: (0, j))],
        out_specs=pl.BlockSpec((tm, tn), lambda j, i: (i, j)),
        out_shape=jax.ShapeDtypeStruct((n, p), BF16),
        scratch_shapes=[pltpu.VMEM((d, tn), BF16)],
        compiler_params=_params("arbitrary", "arbitrary"),
        name="in_proj",
    )(h, w)


def _rope_tables(pos):
    half = RET_DK // 2
    theta = 1.0 / (ROPE_BASE ** jnp.linspace(0.0, 1.0, half, dtype=F32))
    ang = pos.astype(F32)[:, None] * theta[None, :]
    cos, sin = jnp.cos(ang), jnp.sin(ang)
    return (jnp.repeat(cos, 2, axis=1),
            jnp.stack([-sin, sin], axis=-1).reshape(pos.shape[0], RET_DK))


def _rot(x, cos, sin_signed):
    lane = lax.broadcasted_iota(I32, x.shape, 1)
    width = x.shape[1]
    nbr = jnp.where((lane & 1) == 0, pltpu.roll(x, width - 1, 1), pltpu.roll(x, 1, 1))
    return x * cos + nbr * sin_signed


def _gate_out(o, g_ref_val):
    return (_rms(o) * _silu(g_ref_val.astype(F32))).astype(BF16)


def _ret_prompt_kernel(q_ref, k_ref, v_ref, g_ref, cos_ref, sin_ref, lg_ref, o_ref, s_out_ref, s_ref):
    c = RET_CHUNK
    seq = q_ref.shape[0]
    lg = lg_ref[0:1, 0:1]
    ti = lax.broadcasted_iota(I32, (c, c), 0)
    si = lax.broadcasted_iota(I32, (c, c), 1)
    decay = jnp.where(ti >= si, jnp.exp((ti - si).astype(F32) * lg), 0.0)
    tcol = lax.broadcasted_iota(I32, (c, 1), 0).astype(F32)
    dq = jnp.exp((tcol + 1.0) * lg)
    dk = jnp.exp((float(c - 1) - tcol) * lg)
    dchunk = jnp.exp(float(c) * lg)
    s_ref[...] = jnp.zeros_like(s_ref)

    def body(ci, carry):
        r0 = pl.multiple_of(ci * c, c)
        rows = pl.ds(r0, c)
        cos, sin = cos_ref[rows, :], sin_ref[rows, :]
        q = _rot(q_ref[rows, :].astype(F32), cos, sin)
        k = _rot(k_ref[rows, :].astype(F32), cos, sin) * (RET_DK ** -0.5)
        v = v_ref[rows, :]
        s = s_ref[...]
        scores = _dot_nt(q.astype(BF16), k.astype(BF16)) * decay
        o = _dot(scores.astype(BF16), v) + _dot((q * dq).astype(BF16), s.astype(BF16))
        s_ref[...] = dchunk * s + _dot_tn((k * dk).astype(BF16), v)
        o_ref[rows, :] = _gate_out(o, g_ref[rows, :])
        return carry

    lax.fori_loop(0, seq // c, body, 0)
    s_out_ref[...] = s_ref[...]


def _ret_prompt(proj, n_batch, seq, heads, cos, sin, log_gamma):
    qk_blocks = heads
    return pl.pallas_call(
        _ret_prompt_kernel,
        grid=(n_batch, heads),
        in_specs=[pl.BlockSpec((seq, RET_DK), lambda b, h: (b, h)),
                  pl.BlockSpec((seq, RET_DK), lambda b, h: (b, qk_blocks + h)),
                  pl.BlockSpec((seq, RET_DV), lambda b, h: (b, heads + h)),
                  pl.BlockSpec((seq, RET_DV), lambda b, h: (b, 2 * heads + h)),
                  pl.BlockSpec((seq, RET_DK), lambda b, h: (0, 0)),
                  pl.BlockSpec((seq, RET_DK), lambda b, h: (0, 0)),
                  pl.BlockSpec((None, 1, 128), lambda b, h: (h, 0, 0))],
        out_specs=[pl.BlockSpec((seq, RET_DV), lambda b, h: (b, h)),
                   pl.BlockSpec((None, None, RET_DK, RET_DV), lambda b, h: (b, h, 0, 0))],
        out_shape=[jax.ShapeDtypeStruct((n_batch * seq, heads * RET_DV), BF16),
                   jax.ShapeDtypeStruct((n_batch, heads, RET_DK, RET_DV), F32)],
        scratch_shapes=[pltpu.VMEM((RET_DK, RET_DV), F32)],
        compiler_params=_params("arbitrary", "arbitrary"),
        name="ret_prompt",
    )(proj, proj, proj, proj, cos, sin, log_gamma)


def _ret_sample_kernel(steps, q_ref, k_ref, v_ref, g_ref, cos_ref, sin_ref, lg_ref, s_in_ref, o_ref, s_out_ref):
    rows = q_ref.shape[0]
    shift = int(math.log2(steps))
    cos, sin = cos_ref[...], sin_ref[...]
    ri = lax.broadcasted_iota(I32, (rows, rows), 0)
    ci = lax.broadcasted_iota(I32, (rows, rows), 1)
    pair = ((ri >> shift) == (ci >> shift)) & (ri >= ci)
    rid = lax.broadcasted_iota(I32, (rows, 1), 0)
    step = (rid & (steps - 1)).astype(F32)
    row_batch = rid >> shift
    for hh in range(RET_SAMPLE_HEADS):
        kc = slice(hh * RET_DK, (hh + 1) * RET_DK)
        vc = slice(hh * RET_DV, (hh + 1) * RET_DV)
        lg = lg_ref[hh, 0:1, 0:1]
        q = _rot(q_ref[:, kc].astype(F32), cos, sin)
        k = _rot(k_ref[:, kc].astype(F32), cos, sin) * (RET_DK ** -0.5)
        v = v_ref[:, vc]
        decay = jnp.where(pair, jnp.exp((ri - ci).astype(F32) * lg), 0.0)
        o = _dot((_dot_nt(q.astype(BF16), k.astype(BF16)) * decay).astype(BF16), v)
        qd = (q * jnp.exp((step + 1.0) * lg)).astype(BF16)
        kd = k * jnp.exp((float(steps - 1) - step) * lg)
        dall = jnp.exp(float(steps) * lg)
        for j in range(rows // steps):
            s = s_in_ref[j, hh]
            o = o + jnp.where(row_batch == j, _dot(qd, s.astype(BF16)), 0.0)
            kj = jnp.where(row_batch == j, kd, 0.0).astype(BF16)
            s_out_ref[j, hh] = dall * s + _dot_tn(kj, v)
        o_ref[:, vc] = _gate_out(o, g_ref[:, vc])


def _ret_sample(proj, state, n_prompt, steps, heads, cos, sin, log_gamma):
    n_sample_batch = state.shape[0]
    per = SAMPLE_ROWS // steps
    base = n_prompt // SAMPLE_ROWS
    hs = RET_SAMPLE_HEADS
    groups = heads // hs
    kw, vw = hs * RET_DK, hs * RET_DV
    return pl.pallas_call(
        functools.partial(_ret_sample_kernel, steps),
        grid=(n_sample_batch // per, groups),
        in_specs=[pl.BlockSpec((SAMPLE_ROWS, kw), lambda b, h: (base + b, h)),
                  pl.BlockSpec((SAMPLE_ROWS, kw), lambda b, h: (base + b, groups + h)),
                  pl.BlockSpec((SAMPLE_ROWS, vw), lambda b, h: (base + b, groups + h)),
                  pl.BlockSpec((SAMPLE_ROWS, vw), lambda b, h: (base + b, 2 * groups + h)),
                  pl.BlockSpec((SAMPLE_ROWS, RET_DK), lambda b, h: (0, 0)),
                  pl.BlockSpec((SAMPLE_ROWS, RET_DK), lambda b, h: (0, 0)),
                  pl.BlockSpec((hs, 1, 128), lambda b, h: (h, 0, 0)),
                  pl.BlockSpec((per, hs, RET_DK, RET_DV), lambda b, h: (b, h, 0, 0))],
        out_specs=[pl.BlockSpec((SAMPLE_ROWS, vw), lambda b, h: (b, h)),
                   pl.BlockSpec((per, hs, RET_DK, RET_DV), lambda b, h: (b, h, 0, 0))],
        out_shape=[jax.ShapeDtypeStruct((n_sample_batch * steps, heads * RET_DV), BF16),
                   jax.ShapeDtypeStruct(state.shape, F32)],
        compiler_params=_params("arbitrary", "arbitrary"),
        name="ret_sample",
    )(proj, proj, proj, proj, cos, sin, log_gamma, state)


def _hg_prefix_matrix():
    c = HG_CHUNK
    t = np.arange(c)[:, None]
    s = np.arange(c)[None, :]
    le = (s <= t).astype(np.float32)
    mats = [le, le * ((s // HG_LEAF) == (t // HG_LEAF))]
    for half in HG_LEVELS:
        mid = (t // (2 * half)) * (2 * half) + half - 1
        mats.append(le - (s <= mid).astype(np.float32))
    return jnp.asarray(np.concatenate(mats, axis=0), dtype=BF16)


def _hg_gates(q, f, lb):
    forget = lb + (1.0 - lb) * jax.nn.sigmoid(f)
    return _silu(q), 1.0 - forget, jnp.log(forget)


def _hg_guard_kernel(f_ref, lb_ref, o_ref):
    @pl.when(pl.program_id(0) == 0)
    def _():
        o_ref[...] = jnp.zeros_like(o_ref)

    rows, width = f_ref.shape
    lb = lb_ref[...]

    def leaf(li, lowest):
        r = pl.ds(pl.multiple_of(li * HG_LEAF, HG_LEAF), HG_LEAF)
        lf = jnp.log(lb + (1.0 - lb) * jax.nn.sigmoid(f_ref[r, :].astype(F32)))
        return jnp.minimum(lowest, jnp.sum(lf, axis=0, keepdims=True))

    lowest = lax.fori_loop(0, rows // HG_LEAF, leaf, jnp.zeros((1, width), F32), unroll=ROW_CHUNK_UNROLL)
    o_ref[...] = jnp.minimum(o_ref[...], jnp.min(lowest))


def _hg_guard(proj, n_prompt, heads, lb):
    d = heads * HG_DH
    tile = PROJ_TILE
    return pl.pallas_call(
        _hg_guard_kernel,
        grid=(n_prompt // tile,),
        in_specs=[pl.BlockSpec((tile, d), lambda i: (i, 1)),
                  pl.BlockSpec((1, d), lambda i: (0, 0))],
        out_specs=pl.BlockSpec((8, 128), lambda i: (0, 0)),
        out_shape=jax.ShapeDtypeStruct((8, 128), F32),
        compiler_params=_params("arbitrary"),
        name="hg_guard",
    )(proj, lb)


def _hg_prompt_kernel(exact_leaf, q_ref, f_ref, i_ref, g_ref, lb_ref, gn_ref, pm_ref, o_ref, s_out_ref, s_ref):
    c = HG_CHUNK
    seq = q_ref.shape[0]
    n_heads = q_ref.shape[1] // HG_DH
    ti = lax.broadcasted_iota(I32, (c, c), 0)
    si = lax.broadcasted_iota(I32, (c, c), 1)
    leaf_shift = int(math.log2(HG_LEAF))
    mask_leaf = (si <= ti) & ((ti >> leaf_shift) == (si >> leaf_shift))
    level_masks = []
    for half in HG_LEVELS:
        sh = int(math.log2(2 * half))
        level_masks.append(((ti >> sh) == (si >> sh)) & ((ti & (2 * half - 1)) >= half) & ((si & (2 * half - 1)) < half))
    eye = ti == si
    leaf_pos = lax.broadcasted_iota(I32, (c, 1), 0) & (HG_LEAF - 1)
    s_ref[...] = jnp.zeros_like(s_ref)

    width = n_heads * HG_DH
    head_cols = [slice(hh * HG_DH, (hh + 1) * HG_DH) for hh in range(n_heads)]

    def body(ci, carry):
        rows = pl.ds(pl.multiple_of(ci * c, c), c)
        qh, kk, lf = _hg_gates(q_ref[rows, :].astype(F32), f_ref[rows, :].astype(F32), lb_ref[...])
        sums = _dot(pm_ref[...], jnp.concatenate(_split2(lf), axis=1))
        sums = sums[:, :width] + sums[:, width:]
        b = sums[0:c]
        if exact_leaf:
            a = [jnp.zeros((c, c), F32) for _ in head_cols]
            vf = i_ref[rows, :].astype(F32)
            o_leaf = [jnp.zeros((c, HG_DH), F32) for _ in head_cols]
            for dist in range(HG_LEAF):
                if dist == 0:
                    k_s, b_s, v_s = kk, b, vf
                else:
                    k_s, b_s, v_s = pltpu.roll(kk, dist, 0), pltpu.roll(b, dist, 0), pltpu.roll(vf, dist, 0)
                pair = qh * k_s * jnp.exp(jnp.minimum(b - b_s, 0.0))
                for hh, hc in enumerate(head_cols):
                    w = jnp.sum(pair[:, hc], axis=1, keepdims=True)
                    o_leaf[hh] = o_leaf[hh] + jnp.where(leaf_pos >= dist, w, 0.0) * v_s[:, hc]
        else:
            d_leaf = sums[c:2 * c]
            q_f = (qh * jnp.exp(d_leaf)).astype(BF16)
            k_f = (kk * jnp.exp(-d_leaf)).astype(BF16)
            a = [jnp.where(mask_leaf, _dot_nt(q_f[:, hc], k_f[:, hc]), 0.0) for hc in head_cols]
            o_leaf = [0.0 for _ in head_cols]
        for lvl, mask in enumerate(level_masks):
            w = jnp.exp(-jnp.abs(sums[(2 + lvl) * c:(3 + lvl) * c]))
            q_f = (qh * w).astype(BF16)
            k_f = (kk * w).astype(BF16)
            a = [a_h + jnp.where(mask, _dot_nt(q_f[:, hc], k_f[:, hc]), 0.0) for a_h, hc in zip(a, head_cols)]
        b_last = b[c - 1:c, :]
        q_b = (qh * jnp.exp(b)).astype(BF16)
        k_b = (kk * jnp.exp(b_last - b)).astype(BF16)
        e_last = jnp.exp(b_last)
        gate = gn_ref[...] * _silu(g_ref[rows, :].astype(F32))
        for hh, hc in enumerate(head_cols):
            v = i_ref[rows, hc]
            s = s_ref[hh]
            o = _dot(a[hh].astype(BF16), v) + _dot(q_b[:, hc], s.astype(BF16)) + o_leaf[hh]
            col = jnp.sum(jnp.where(eye, e_last[:, hc], 0.0), axis=1, keepdims=True)
            s_ref[hh] = col * s + _dot_tn(k_b[:, hc], v)
            o_ref[rows, hc] = (_rms(o) * gate[:, hc]).astype(BF16)
        return carry

    lax.fori_loop(0, seq // c, body, 0)
    s_out_ref[...] = s_ref[...]


def _hg_prompt(exact_leaf, proj, lb, out_norm, *, n_batch, seq, heads):
    pm = _hg_prefix_matrix()
    hp = HG_HEADS_PER_STEP
    width = hp * HG_DH
    groups = heads // hp
    col = lambda part: (lambda b, h: (b, part * groups + h))
    return pl.pallas_call(
        functools.partial(_hg_prompt_kernel, exact_leaf),
        grid=(n_batch, groups),
        in_specs=[pl.BlockSpec((seq, width), col(0)),
                  pl.BlockSpec((seq, width), col(1)),
                  pl.BlockSpec((seq, width), col(2)),
                  pl.BlockSpec((seq, width), col(3)),
                  pl.BlockSpec((1, width), lambda b, h: (0, h)),
                  pl.BlockSpec((1, width), lambda b, h: (0, h)),
                  pl.BlockSpec(pm.shape, lambda b, h: (0, 0))],
        out_specs=[pl.BlockSpec((seq, width), lambda b, h: (b, h)),
                   pl.BlockSpec((None, hp, HG_DH, HG_DH), lambda b, h: (b, h, 0, 0))],
        out_shape=[jax.ShapeDtypeStruct((n_batch * seq, heads * HG_DH), BF16),
                   jax.ShapeDtypeStruct((n_batch, heads, HG_DH, HG_DH), F32)],
        scratch_shapes=[pltpu.VMEM((hp, HG_DH, HG_DH), F32)],
        compiler_params=_params("arbitrary", "arbitrary"),
        name="hg_prompt",
    )(proj, proj, proj, proj, lb, out_norm, pm)


def _hg_sample_kernel(steps, heads, q_ref, f_ref, i_ref, g_ref, lb_ref, gn_ref, s_in_ref, o_ref, s_out_ref):
    rows = q_ref.shape[0]
    shift = int(math.log2(steps))
    ri = lax.broadcasted_iota(I32, (rows, rows), 0)
    ci = lax.broadcasted_iota(I32, (rows, rows), 1)
    prefix = (((ri >> shift) == (ci >> shift)) & (ci <= ri)).astype(BF16)
    rid = lax.broadcasted_iota(I32, (rows, 1), 0)
    step = rid & (steps - 1)
    row_batch = rid >> shift
    ki = lax.broadcasted_iota(I32, (HG_DH, HG_DH), 0)
    vi = lax.broadcasted_iota(I32, (HG_DH, HG_DH), 1)
    eye = ki == vi

    def head(h):
        cols = pl.ds(pl.multiple_of(h * HG_DH, HG_DH), HG_DH)
        qh, kk, lf = _hg_gates(q_ref[:, cols].astype(F32), f_ref[:, cols].astype(F32), lb_ref[:, cols])
        v = i_ref[:, cols]
        vf = v.astype(F32)
        hi, lo = _split2(lf)
        b = _dot(prefix, jnp.concatenate([hi, lo], axis=1))
        b = b[:, :HG_DH] + b[:, HG_DH:]
        o = jnp.zeros((rows, HG_DH), F32)
        for dist in range(steps):
            if dist == 0:
                k_s, b_s, v_s = kk, b, vf
            else:
                k_s, b_s, v_s = (pltpu.roll(kk, dist, 0), pltpu.roll(b, dist, 0), pltpu.roll(vf, dist, 0))
            w = jnp.sum(qh * k_s * jnp.exp(jnp.minimum(b - b_s, 0.0)), axis=1, keepdims=True)
            o = o + jnp.where(step >= dist, w, 0.0) * v_s
        qd = (qh * jnp.exp(b)).astype(BF16)
        for j in range(rows // steps):
            s = s_in_ref[j, h]
            o = o + jnp.where(row_batch == j, _dot(qd, s.astype(BF16)), 0.0)
            b_last = b[(j + 1) * steps - 1:(j + 1) * steps, :]
            col = jnp.sum(jnp.where(eye, jnp.exp(b_last), 0.0), axis=1, keepdims=True)
            kj = jnp.where(row_batch == j, kk * jnp.exp(jnp.minimum(b_last - b, 0.0)), 0.0).astype(BF16)
            s_out_ref[j, h] = col * s + _dot_tn(kj, v)
        o_ref[:, cols] = (_rms(o) * gn_ref[:, cols] * _silu(g_ref[:, cols].astype(F32))).astype(BF16)

    def group(gi, carry):
        for hh in range(HG_HEADS_PER_STEP):
            head(gi * HG_HEADS_PER_STEP + hh)
        return carry

    lax.fori_loop(0, heads // HG_HEADS_PER_STEP, group, 0)


def _hg_sample(proj, state, n_prompt, steps, heads, lb, out_norm):
    n_sample_batch = state.shape[0]
    d = heads * HG_DH
    per = SAMPLE_ROWS // steps
    base = n_prompt // SAMPLE_ROWS
    col = lambda part: (lambda b: (base + b, part))
    return pl.pallas_call(
        functools.partial(_hg_sample_kernel, steps, heads),
        grid=(n_sample_batch // per,),
        in_specs=[pl.BlockSpec((SAMPLE_ROWS, d), col(0)),
                  pl.BlockSpec((SAMPLE_ROWS, d), col(1)),
                  pl.BlockSpec((SAMPLE_ROWS, d), col(2)),
                  pl.BlockSpec((SAMPLE_ROWS, d), col(3)),
                  pl.BlockSpec((1, d), lambda b: (0, 0)),
                  pl.BlockSpec((1, d), lambda b: (0, 0)),
                  pl.BlockSpec((per, heads, HG_DH, HG_DH), lambda b: (b, 0, 0, 0))],
        out_specs=[pl.BlockSpec((SAMPLE_ROWS, d), lambda b: (b, 0)),
                   pl.BlockSpec((per, heads, HG_DH, HG_DH), lambda b: (b, 0, 0, 0))],
        out_shape=[jax.ShapeDtypeStruct((n_sample_batch * steps, d), BF16),
                   jax.ShapeDtypeStruct(state.shape, F32)],
        compiler_params=_params("arbitrary"),
        name="hg_sample",
    )(proj, proj, proj, proj, lb, out_norm, state)


def _first_max(vals, lane, width):
    m = jnp.max(vals, axis=1, keepdims=True)
    idx = jnp.min(jnp.where(vals == m, lane, width), axis=1, keepdims=True)
    return m, idx


def _route(logits, bias):
    neg = -jnp.inf
    lane = lax.broadcasted_iota(I32, logits.shape, 1)
    group = lane >> GROUP_SHIFT
    scores = jax.nn.sigmoid(logits)
    biased = scores + bias
    best = sel = None
    for gi in range(N_GROUPS):
        vals = jnp.where(group == gi, biased, neg)
        m1, i1 = _first_max(vals, lane, N_EXPERTS)
        m2 = jnp.max(jnp.where(lane == i1, neg, vals), axis=1, keepdims=True)
        total = m1 + m2
        if gi == 0:
            best, sel = total, jnp.zeros_like(i1)
        else:
            better = total > best
            sel = jnp.where(better, gi, sel)
            best = jnp.where(better, total, best)
    vals = jnp.where(group == sel, biased, neg)
    _, e1 = _first_max(vals, lane, N_EXPERTS)
    _, e2 = _first_max(jnp.where(lane == e1, neg, vals), lane, N_EXPERTS)
    w1 = jnp.sum(jnp.where(lane == e1, scores, 0.0), axis=1, keepdims=True)
    w2 = jnp.sum(jnp.where(lane == e2, scores, 0.0), axis=1, keepdims=True)
    tot = w1 + w2
    return e1, e2, w1 / tot, w2 / tot


def _mixout_kernel(prompt_tiles, o_p, o_s, w_ref, x_p, x_s, g1_ref, g2_ref, gm_p, gm_s, sh_p, sh_s, sc_p, sc_s,
                   rw_ref, rb_ref, xo_p, xo_s, h_ref, ridx_ref, rwt_ref, cnt_ref, carry_ref, y_ref):
    i = pl.program_id(0)
    tm = x_p.shape[0]

    @pl.when(i == 0)
    def _():
        carry_ref[...] = jnp.zeros_like(carry_ref)

    def tile(sample):
        o_ref, x_ref, xo_ref = (o_s, x_s, xo_s) if sample else (o_p, x_p, xo_p)
        y_ref[...] = _dot(o_ref[...], w_ref[...])

        def chunk(rows, _):
            x = x_ref[rows, :] + _mod_rows(sample, gm_p, gm_s, rows) * (_rms(y_ref[rows, :]) * g1_ref[...])
            xo_ref[rows, :] = x
            h_ref[rows, :] = (_rms(x) * g2_ref[...] * (1.0 + _mod_rows(sample, sc_p, sc_s, rows))
                              + _mod_rows(sample, sh_p, sh_s, rows))

        _row_chunks(tm, chunk)

    _by_group(i >= prompt_tiles, tile)

    h1, h2 = _split2(h_ref[...])
    w1, w2 = _split2(rw_ref[...])
    first = _dot(h1, jnp.concatenate([w1, w2], axis=1))
    logits = first[:, :N_EXPERTS] + (first[:, N_EXPERTS:] + _dot(h2, w1))
    e1, e2, p1, p2 = _route(logits, rb_ref[...])

    lane = lax.broadcasted_iota(I32, (tm, N_EXPERTS), 1)
    hot1, hot2 = lane == e1, lane == e2
    onehot = (hot1 | hot2).astype(BF16)
    ti = lax.broadcasted_iota(I32, (tm, tm), 0)
    si = lax.broadcasted_iota(I32, (tm, tm), 1)
    before = _dot((si < ti).astype(BF16), onehot) + carry_ref[...]
    r1 = jnp.sum(jnp.where(hot1, before, 0.0), axis=1, keepdims=True).astype(I32)
    r2 = jnp.sum(jnp.where(hot2, before, 0.0), axis=1, keepdims=True).astype(I32)
    carry = carry_ref[...] + jnp.sum(onehot.astype(F32), axis=0, keepdims=True)
    carry_ref[...] = carry

    wide = lax.broadcasted_iota(I32, (tm, 128), 1)
    ridx_ref[...] = jnp.where(wide == 0, e1, jnp.where(wide == 1, e2, jnp.where(wide == 2, r1, r2)))
    rwt_ref[...] = jnp.where(wide == 0, p1, p2)
    cnt_ref[...] = jnp.zeros_like(cnt_ref)
    cnt_ref[0:1, 0:N_EXPERTS] = carry


def _mixout(tok, o_prompt, o_sample, w_out, x_p, x_s, g1, g2, mod_p, mod_s, router_w, router_b):
    d = tok.d
    v = o_prompt.shape[1]
    return pl.pallas_call(
        functools.partial(_mixout_kernel, tok.prompt_tiles),
        grid=(tok.tiles,),
        in_specs=tok.split_specs(v)
                 + [pl.BlockSpec((v, d), lambda i: (0, 0), pipeline_mode=pl.Buffered(1))]
                 + tok.split_specs(d) + [_const_spec((1, d)), _const_spec((1, d))]
                 + tok.mod_specs(2) + tok.mod_specs(3) + tok.mod_specs(4)
                 + [_const_spec((d, N_EXPERTS)), _const_spec((1, N_EXPERTS))],
        out_specs=tok.split_specs(d, out=True) + [tok.row_spec(d), tok.row_spec(128), tok.row_spec(128),
                                        _const_spec((8, 128))],
        out_shape=tok.split_shapes(d, F32) + [jax.ShapeDtypeStruct((tok.n, d), F32),
                                              jax.ShapeDtypeStruct((tok.n, 128), I32),
                                              jax.ShapeDtypeStruct((tok.n, 128), F32),
                                              jax.ShapeDtypeStruct((8, 128), F32)],
        scratch_shapes=[pltpu.VMEM((1, N_EXPERTS), F32), pltpu.VMEM((TOKEN_TILE, d), F32)],
        compiler_params=_params("arbitrary"),
        name="mix_out",
    )(o_prompt, o_sample, w_out, x_p, x_s, g1, g2, mod_p, mod_s, mod_p, mod_s, mod_p, mod_s, router_w, router_b)


def _expert_plan(ridx, counts, n_tiles, n_visits):
    n = ridx.shape[0]
    expert = ridx[:, 0:2]
    rank = ridx[:, 2:4]
    cnt = counts[0, :N_EXPERTS].astype(I32)
    padded = ((cnt + EXPERT_TILE - 1) // EXPERT_TILE) * EXPERT_TILE
    ends = jnp.cumsum(padded)
    starts = ends - padded
    pos = starts[expert] + rank
    pos = pos.reshape(n // TOKEN_TILE, TOKEN_TILE, 2).transpose(0, 2, 1)
    tiles_e = padded // EXPERT_TILE
    first_tile_e = starts // EXPERT_TILE
    visits_e = (tiles_e + VISIT_TILES - 1) // VISIT_TILES
    visit_end = jnp.cumsum(visits_e)
    visit_start = visit_end - visits_e
    used_visits = visit_end[-1]
    used_tiles = ends[-1] // EXPERT_TILE
    v = jnp.arange(n_visits, dtype=I32)
    e = jnp.minimum(jnp.sum((v[:, None] >= visit_end[None, :]).astype(I32), axis=1), N_EXPERTS - 1)
    k = v - visit_start[e]
    is_used = v < used_visits
    last = jnp.maximum(used_visits - 1, 0)
    tile_used = first_tile_e[e] + VISIT_TILES * k
    tile_zero = used_tiles + VISIT_TILES * (v - used_visits)
    n_compute = jnp.where(is_used, jnp.clip(tiles_e[e] - VISIT_TILES * k, 0, VISIT_TILES), 0)
    n_zero = jnp.where(is_used, 0, jnp.clip(n_tiles - tile_zero, 0, VISIT_TILES))
    out_tile = jnp.where(is_used, tile_used, jnp.clip(tile_zero, 0, n_tiles - 1))
    return (jnp.where(is_used, e, e[last]), out_tile, n_compute, n_zero,
            jnp.where(is_used, tile_used, tile_used[last]), pos)


def _start_rows(groups, copy):
    def body(g, carry):
        for j in range(SUBLANES):
            copy(g, j).start()
        return carry
    lax.fori_loop(0, groups, body, 0)


def _dispatch_kernel(pos_ref, h_ref, xs_in, xs_hbm, sem):
    del xs_in
    groups = h_ref.shape[0]
    for choice in range(2):
        def row(g, j, choice=choice):
            p = pos_ref[0, choice, g * SUBLANES + j]
            return pltpu.make_async_copy(h_ref.at[g, pl.ds(j, 1)],
                                         xs_hbm.at[p >> 3, pl.ds(p & (SUBLANES - 1), 1)], sem.at[choice])
        _start_rows(groups, row)
    for choice in range(2):
        pltpu.make_async_copy(h_ref, xs_hbm.at[pl.ds(0, groups)], sem.at[choice]).wait()


def _dispatch(h, pos, n_rows, buf=None):
    n, d = h.shape
    if buf is None:
        buf = jnp.zeros((n_rows // SUBLANES, SUBLANES, d), F32)
    tile_groups = TOKEN_TILE // SUBLANES
    return pl.pallas_call(
        _dispatch_kernel,
        grid=(n // TOKEN_TILE,),
        in_specs=[pl.BlockSpec((1, 2, TOKEN_TILE), lambda i: (i, 0, 0), memory_space=pltpu.SMEM),
                  pl.BlockSpec((tile_groups, SUBLANES, d), lambda i: (i, 0, 0)),
                  pl.BlockSpec(memory_space=pl.ANY)],
        out_specs=pl.BlockSpec(memory_space=pl.ANY),
        out_shape=jax.ShapeDtypeStruct((n_rows // SUBLANES, SUBLANES, d), F32),
        scratch_shapes=[pltpu.SemaphoreType.DMA((2,))],
        input_output_aliases={2: 0},
        compiler_params=_params("arbitrary"),
        name="dispatch",
    )(pos, h.reshape(n // SUBLANES, SUBLANES, d), buf)


def _expert_kernel(ve_ref, out_ref, nc_ref, nz_ref, in_ref, *refs):
    del ve_ref, in_ref
    x_refs = refs[:VISIT_TILES]
    wg_ref, wu_ref, wd_ref, ys_hbm, acc, sem = refs[VISIT_TILES:]
    visit, part = pl.program_id(0), pl.program_id(1)
    n_compute, n_zero, first = nc_ref[visit], nz_ref[visit], out_ref[visit]
    groups, _, d = x_refs[0].shape
    rows = groups * SUBLANES

    def flush(count):
        def copy(k):
            dst = ys_hbm.at[pl.ds(pl.multiple_of((first + k) * groups, groups), groups)]
            return pltpu.make_async_copy(acc.at[k], dst, sem.at[k])
        for k in range(VISIT_TILES):
            @pl.when(k < count)
            def _():
                copy(k).start()
        for k in range(VISIT_TILES):
            @pl.when(k < count)
            def _():
                copy(k).wait()

    @pl.when(n_compute > 0)
    def _():
        x = jnp.concatenate([r[...].reshape(rows, d).astype(BF16) for r in x_refs], axis=0)
        hidden = _silu(_dot(x, wg_ref[...].astype(BF16))) * _dot(x, wu_ref[...].astype(BF16))
        y = _dot(hidden.astype(BF16), wd_ref[...].astype(BF16)).reshape(acc.shape)

        @pl.when(part == 0)
        def _():
            acc[...] = y

        @pl.when(part > 0)
        def _():
            acc[...] += y

        @pl.when(part == EXPERT_F_SPLIT - 1)
        def _():
            flush(n_compute)

    @pl.when((n_zero > 0) & (part == 0))
    def _():
        acc[...] = jnp.zeros_like(acc)
        flush(n_zero)


def _experts(xs, plan, layer, wg, wu, wd):
    d, f = wg.shape[2], wg.shape[3]
    fs = f // EXPERT_F_SPLIT
    visit_expert, out_tile, n_compute, n_zero, in_tile = plan[:5]
    n_visits = visit_expert.shape[0]
    tile_groups = EXPERT_TILE // SUBLANES
    n_tiles = xs.shape[0] // tile_groups
    last_part = EXPERT_F_SPLIT - 1

    def x_spec(k):
        return pl.BlockSpec((tile_groups, SUBLANES, d),
                            lambda v, p, ve, ot, nc, nz, it: (jnp.minimum(it[v] + k, n_tiles - 1), 0, 0))

    def part_of(v, p, nc):
        return jnp.where(nc[v] > 0, p, last_part)

    grid_spec = pltpu.PrefetchScalarGridSpec(
        num_scalar_prefetch=5,
        grid=(n_visits, EXPERT_F_SPLIT),
        in_specs=[x_spec(k) for k in range(VISIT_TILES)] + [
            pl.BlockSpec((None, None, d, fs), lambda v, p, ve, ot, nc, nz, it: (layer, ve[v], 0, part_of(v, p, nc))),
            pl.BlockSpec((None, None, d, fs), lambda v, p, ve, ot, nc, nz, it: (layer, ve[v], 0, part_of(v, p, nc))),
            pl.BlockSpec((None, None, fs, d), lambda v, p, ve, ot, nc, nz, it: (layer, ve[v], part_of(v, p, nc), 0))],
        out_specs=pl.BlockSpec(memory_space=pl.ANY),
        scratch_shapes=[pltpu.VMEM((VISIT_TILES, tile_groups, SUBLANES, d), F32),
                        pltpu.SemaphoreType.DMA((VISIT_TILES,))],
    )
    return pl.pallas_call(
        _expert_kernel,
        grid_spec=grid_spec,
        out_shape=jax.ShapeDtypeStruct(xs.shape, F32),
        compiler_params=_params("arbitrary", "arbitrary"),
        name="experts",
    )(visit_expert, out_tile, n_compute, n_zero, in_tile, *([xs] * VISIT_TILES), wg, wu, wd)


def _by_group(is_sample, body):
    @pl.when(is_sample)
    def _():
        body(True)

    @pl.when(jnp.logical_not(is_sample))
    def _():
        body(False)


def _row_chunks(n_rows, fn):
    per = ROW_CHUNK // SUBLANES

    def body(ci, carry):
        fn(pl.ds(pl.multiple_of(ci * ROW_CHUNK, ROW_CHUNK), ROW_CHUNK), pl.ds(pl.multiple_of(ci * per, per), per))
        return carry
    lax.fori_loop(0, n_rows // ROW_CHUNK, body, 0, unroll=ROW_CHUNK_UNROLL)


def _mod_rows(sample, p_ref, s_ref, rows):
    return s_ref[rows, :] if sample else p_ref[...]


def _moeout_kernel(prompt_tiles, with_next, pos_ref, pos_next_ref, ys_hbm, x_p, x_s, rwt_ref, g3_ref, gf_p, gf_s,
                   *rest):
    if with_next:
        gn_ref, sh_p, sh_s, sc_p, sc_s, xo_p, xo_s, h_ref, ybuf, sem = rest
    else:
        xo_p, xo_s, ybuf, sem = rest
    groups = ybuf.shape[2]
    n_rows = groups * SUBLANES
    d = ybuf.shape[4]
    i = pl.program_id(0)
    slot = i & 1

    def fetch(p_ref, to_slot):
        for choice in range(2):
            def row(g, j, choice=choice):
                p = p_ref[0, choice, g * SUBLANES + j]
                return pltpu.make_async_copy(ys_hbm.at[p >> 3, pl.ds(p & (SUBLANES - 1), 1)],
                                             ybuf.at[to_slot, choice, g, pl.ds(j, 1)], sem.at[to_slot, choice])
            _start_rows(groups, row)

    @pl.when(i == 0)
    def _():
        fetch(pos_ref, 0)

    @pl.when(i + 1 < pl.num_programs(0))
    def _():
        fetch(pos_next_ref, 1 - slot)

    for choice in range(2):
        pltpu.make_async_copy(ys_hbm.at[pl.ds(0, groups)], ybuf.at[slot, choice], sem.at[slot, choice]).wait()

    def tile(sample):
        x_ref, xo_ref = (x_s, xo_s) if sample else (x_p, xo_p)

        def chunk(rows, grp):
            rw = rwt_ref[rows, :]
            y = (rw[:, 0:1] * ybuf[slot, 0, grp].reshape(ROW_CHUNK, d)
                 + rw[:, 1:2] * ybuf[slot, 1, grp].reshape(ROW_CHUNK, d))
            x = x_ref[rows, :] + _mod_rows(sample, gf_p, gf_s, rows) * (_rms(y) * g3_ref[...])
            xo_ref[rows, :] = x
            if with_next:
                h = (_rms(x) * gn_ref[...] * (1.0 + _mod_rows(sample, sc_p, sc_s, rows))
                     + _mod_rows(sample, sh_p, sh_s, rows))
                h_ref[rows, :] = h.astype(BF16)

        _row_chunks(n_rows, chunk)

    _by_group(pl.program_id(0) >= prompt_tiles, tile)


def _moeout(tok, pos, ys, x_p, x_s, rwt, g3, mod_p, mod_s, nxt=None):
    d = tok.d
    last = tok.tiles - 1
    ins = ([pl.BlockSpec((1, 2, TOKEN_TILE), lambda i: (i, 0, 0), memory_space=pltpu.SMEM),
            pl.BlockSpec((1, 2, TOKEN_TILE), lambda i: (jnp.minimum(i + 1, last), 0, 0), memory_space=pltpu.SMEM),
            pl.BlockSpec(memory_space=pl.ANY)]
           + tok.split_specs(d) + [tok.row_spec(128), _const_spec((1, d))] + tok.mod_specs(5))
    args = [pos, pos, ys, x_p, x_s, rwt, g3, mod_p, mod_s]
    scratch = [pltpu.VMEM((2, 2, TOKEN_TILE // SUBLANES, SUBLANES, d), F32), pltpu.SemaphoreType.DMA((2, 2))]
    if nxt is None:
        return pl.pallas_call(
            functools.partial(_moeout_kernel, tok.prompt_tiles, False),
            grid=(tok.tiles,), in_specs=ins, out_specs=tok.split_specs(d, out=True),
            out_shape=tok.split_shapes(d, F32), scratch_shapes=scratch,
            compiler_params=_params("arbitrary"), name="moe_out_last",
        )(*args)
    gain_n, mod_pn, mod_sn = nxt
    ins = ins + [_const_spec((1, d))] + tok.mod_specs(0) + tok.mod_specs(1)
    args = args + [gain_n, mod_pn, mod_sn, mod_pn, mod_sn]
    return pl.pallas_call(
        functools.partial(_moeout_kernel, tok.prompt_tiles, True),
        grid=(tok.tiles,), in_specs=ins, out_specs=tok.split_specs(d, out=True) + [tok.row_spec(d)],
        out_shape=tok.split_shapes(d, F32) + [jax.ShapeDtypeStruct((tok.n, d), BF16)],
        scratch_shapes=scratch,
        compiler_params=_params("arbitrary"), name="moe_out_next",
    )(*args)


def kernel(x_prompt, x_sample, c_prompt, c_sample, state_ret, state_hgrn, ada_w, ada_b, norm_gains,
           ret_w_in, ret_w_out, hg_w_in, hg_w_out, hg_lower_bound, hg_out_norm, router_w, router_b,
           exp_w_gate, exp_w_up, exp_w_down):
    n_batch, seq, d = x_prompt.shape
    n_dec, steps, _ = x_sample.shape
    depth = ada_w.shape[0]
    assert depth == 2 and d % RET_DK == 0 and d % HG_DH == 0
    assert seq % RET_CHUNK == 0 and (n_dec * steps) % TOKEN_TILE == 0 and SAMPLE_ROWS % steps == 0
    ret_heads = d // RET_DK
    hg_heads = d // HG_DH
    n_prompt = n_batch * seq
    n_sample = n_dec * steps
    tok = _Tokens(n_batch, seq, n_sample, d)
    n = tok.n

    n_cond = n_batch + n_dec
    pad = (-n_cond) % 8
    c_all = jnp.concatenate([c_prompt, c_sample, jnp.zeros((pad, d), F32)], axis=0)
    mod = _ada(c_all, ada_w, ada_b)
    mods = []
    for l in range(depth):
        mod_p = mod[l, :n_batch].reshape(n_batch, 6, 1, d)
        mod_s = jnp.repeat(mod[l, n_batch:n_cond], steps, axis=0)
        mods.append((mod_p, mod_s))
    gain = lambda l, k: norm_gains[l, k].reshape(1, d)

    x_p = x_prompt.reshape(n_prompt, d)
    x_s = x_sample.reshape(n_sample, d)
    n_tiles = (2 * n + N_EXPERTS * (EXPERT_TILE - 1)) // EXPERT_TILE + 1
    n_visits = (n_tiles + (VISIT_TILES - 1) * (N_EXPERTS + 1)) // VISIT_TILES + 1
    rw = router_w.astype(F32)
    rb = router_b.astype(F32).reshape(1, N_EXPERTS)

    def channel_mixer(l, x_p, x_s, h, ridx, rwt, counts, nxt, buf):
        plan = _expert_plan(ridx, counts, n_tiles, n_visits)
        pos = plan[5]
        xs = _dispatch(h, pos, n_tiles * EXPERT_TILE, buf)
        ys = _experts(xs, plan, l, exp_w_gate, exp_w_up, exp_w_down)
        return _moeout(tok, pos, ys, x_p, x_s, rwt, gain(l, 3), *mods[l], nxt=nxt), xs

    h = _prenorm(tok, x_p, x_s, gain(0, 0), *mods[0])
    proj = _proj(h, ret_w_in[0])
    log_gamma = jnp.log(1.0 - jnp.exp2(-5.0 - jnp.arange(ret_heads, dtype=F32)))
    log_gamma = jnp.broadcast_to(log_gamma[:, None, None], (ret_heads, 1, 128))
    cos_p, sin_p = _rope_tables(jnp.arange(seq))
    cos_s, sin_s = _rope_tables(PAST_LEN + jnp.arange(steps))
    reps = SAMPLE_ROWS // steps
    cos_s, sin_s = jnp.tile(cos_s, (reps, 1)), jnp.tile(sin_s, (reps, 1))
    o_p, ret_prompt = _ret_prompt(proj, n_batch, seq, ret_heads, cos_p, sin_p, log_gamma)
    o_s, ret_sample = _ret_sample(proj, state_ret[0], n_prompt, steps, ret_heads, cos_s, sin_s, log_gamma)
    x_p, x_s, hp, ridx, rwt, counts = _mixout(tok, o_p, o_s, ret_w_out[0].astype(BF16), x_p, x_s,
                                              gain(0, 1), gain(0, 2), *mods[0], rw, rb)
    (x_p, x_s, h), row_buf = channel_mixer(0, x_p, x_s, hp, ridx, rwt, counts, (gain(1, 0),) + mods[1], None)

    sm = jax.nn.softmax(hg_lower_bound.astype(F32), axis=0)
    lb = (jnp.cumsum(sm, axis=0) - sm[0])[1].reshape(1, d)
    proj = _proj(h, hg_w_in[0])
    out_norm = hg_out_norm[0].reshape(1, d)
    leaf_min = _hg_guard(proj, n_prompt, hg_heads, lb)[0, 0]
    o_p, hg_prompt = lax.cond(
        leaf_min > HG_LEAF_LOG_LIMIT,
        functools.partial(_hg_prompt, False, n_batch=n_batch, seq=seq, heads=hg_heads),
        functools.partial(_hg_prompt, True, n_batch=n_batch, seq=seq, heads=hg_heads),
        proj, lb, out_norm)
    o_s, hg_sample = _hg_sample(proj, state_hgrn[0], n_prompt, steps, hg_heads, lb, out_norm)
    x_p, x_s, hp, ridx, rwt, counts = _mixout(tok, o_p, o_s, hg_w_out[0].astype(BF16), x_p, x_s,
                                              gain(1, 1), gain(1, 2), *mods[1], rw, rb)
    (x_p, x_s), _ = channel_mixer(1, x_p, x_s, hp, ridx, rwt, counts, None, row_buf)

    return (x_p.reshape(n_batch, seq, d), x_s.reshape(n_dec, steps, d),
            ret_prompt[None], hg_prompt[None], ret_sample[None], hg_sample[None])
```

```python
import functools
import math

import numpy as np
import jax
import jax.numpy as jnp
from jax import lax
from jax.experimental import pallas as pl
from jax.experimental.pallas import tpu as pltpu

F32, BF16, I32 = jnp.float32, jnp.bfloat16, jnp.int32

EPS = 1e-6
ROPE_BASE = 10000.0
PAST_LEN = 16384
RET_DK = 256
RET_DV = 512
HG_DH = 128
N_EXPERTS = 16
GROUP_SHIFT = 2
N_GROUPS = N_EXPERTS >> GROUP_SHIFT

TOKEN_TILE = 256
PROJ_TILE = 512
PROJ_COLS = 2048
EXPERT_TILE = 384
RET_CHUNK = 256
HG_CHUNK = 128
HG_LEAF = 16
HG_LEVELS = (64, 32, 16)
HG_HEADS_PER_STEP = 4
HG_LEAF_LOG_LIMIT = -80.0
SAMPLE_ROWS = 16
SUBLANES = 8
EXPERT_F_SPLIT = 2
RET_SAMPLE_HEADS = 4
ROW_CHUNK = 16
ROW_CHUNK_UNROLL = 4
VMEM_LIMIT = 56 * 1024 * 1024


def _params(*sem):
    return pltpu.CompilerParams(dimension_semantics=sem, vmem_limit_bytes=VMEM_LIMIT)


def _rms(x):
    return x * lax.rsqrt(jnp.mean(x * x, axis=-1, keepdims=True) + EPS)


def _silu(x):
    return x * jax.nn.sigmoid(x)


def _dot(a, b):
    return jnp.dot(a, b, preferred_element_type=F32)


def _dot_nt(a, b):
    return lax.dot_general(a, b, (((1,), (1,)), ((), ())), preferred_element_type=F32)


def _dot_tn(a, b):
    return lax.dot_general(a, b, (((0,), (0,)), ((), ())), preferred_element_type=F32)


def _split2(x):
    hi = x.astype(BF16)
    lo = (x - hi.astype(F32)).astype(BF16)
    return hi, lo


def _split3(x):
    hi = x.astype(BF16)
    r = x - hi.astype(F32)
    mid = r.astype(BF16)
    lo = (r - mid.astype(F32)).astype(BF16)
    return hi, mid, lo


def _ada_kernel(c_ref, w_ref, b_ref, o_ref):
    s = _silu(c_ref[...]).astype(BF16)
    o_ref[...] = _dot(s, w_ref[...].astype(BF16)) + b_ref[...]


def _ada(c_all, ada_w, ada_b):
    depth, d, d6 = ada_w.shape
    m = c_all.shape[0]
    tn = min(1024, d6)
    return pl.pallas_call(
        _ada_kernel,
        grid=(depth, d6 // tn),
        in_specs=[pl.BlockSpec((m, d), lambda l, j: (0, 0)),
                  pl.BlockSpec((None, d, tn), lambda l, j: (l, 0, j)),
                  pl.BlockSpec((None, 1, tn), lambda l, j: (l, 0, j))],
        out_specs=pl.BlockSpec((None, m, tn), lambda l, j: (l, 0, j)),
        out_shape=jax.ShapeDtypeStruct((depth, m, d6), F32),
        compiler_params=_params("arbitrary", "arbitrary"),
        name="ada_mod",
    )(c_all, ada_w, ada_b.reshape(depth, 1, d6))


class _Tokens:
    def __init__(self, n_prompt_batch, seq, n_sample, d):
        self.d = d
        self.n_prompt = n_prompt_batch * seq
        self.n = self.n_prompt + n_sample
        self.prompt_tiles = self.n_prompt // TOKEN_TILE
        self.tiles_per_batch = seq // TOKEN_TILE
        self.n_batch = n_prompt_batch
        self.tiles = self.n // TOKEN_TILE

    def mod_specs(self, comp):
        d, tpb, nb, npt = self.d, self.tiles_per_batch, self.n_batch, self.prompt_tiles
        return [pl.BlockSpec((None, None, 1, d), lambda i: (jnp.minimum(i // tpb, nb - 1), comp, 0, 0)),
                pl.BlockSpec((TOKEN_TILE, d), lambda i: (jnp.maximum(i - npt, 0), comp),
                             pipeline_mode=pl.Buffered(1))]

    def row_spec(self, width):
        return pl.BlockSpec((TOKEN_TILE, width), lambda i: (i, 0))

    def split_specs(self, width, out=False):
        npt = self.prompt_tiles
        mode = {} if out else dict(pipeline_mode=pl.Buffered(1))
        return [pl.BlockSpec((TOKEN_TILE, width), lambda i: (jnp.minimum(i, npt - 1), 0)),
                pl.BlockSpec((TOKEN_TILE, width), lambda i: (jnp.maximum(i - npt, 0), 0), **mode)]

    def split_shapes(self, width, dtype):
        return [jax.ShapeDtypeStruct((self.n_prompt, width), dtype),
                jax.ShapeDtypeStruct((self.n - self.n_prompt, width), dtype)]


def _const_spec(shape):
    return pl.BlockSpec(shape, lambda i: (0,) * len(shape))


def _pick(is_sample, p_ref, s_ref):
    return jnp.where(is_sample, s_ref[...], p_ref[...])


def _store_split(is_sample, p_ref, s_ref, val):
    @pl.when(is_sample)
    def _():
        s_ref[...] = val

    @pl.when(jnp.logical_not(is_sample))
    def _():
        p_ref[...] = val


def _prenorm_kernel(prompt_tiles, x_p, x_s, g_ref, sh_p, sh_s, sc_p, sc_s, h_ref):
    def tile(sample):
        x_ref = x_s if sample else x_p

        def chunk(rows, _):
            h = (_rms(x_ref[rows, :]) * g_ref[...] * (1.0 + _mod_rows(sample, sc_p, sc_s, rows))
                 + _mod_rows(sample, sh_p, sh_s, rows))
            h_ref[rows, :] = h.astype(BF16)

        _row_chunks(x_p.shape[0], chunk)

    _by_group(pl.program_id(0) >= prompt_tiles, tile)


def _prenorm(tok, x_p, x_s, gain, mod_p, mod_s):
    d = tok.d
    return pl.pallas_call(
        functools.partial(_prenorm_kernel, tok.prompt_tiles),
        grid=(tok.tiles,),
        in_specs=tok.split_specs(d) + [_const_spec((1, d))] + tok.mod_specs(0) + tok.mod_specs(1),
        out_specs=tok.row_spec(d),
        out_shape=jax.ShapeDtypeStruct((tok.n, d), BF16),
        compiler_params=_params("arbitrary"),
        name="prenorm",
    )(x_p, x_s, gain, mod_p, mod_s, mod_p, mod_s)


def _proj_kernel(h_ref, w_ref, o_ref, wb_ref):
    @pl.when(pl.program_id(1) == 0)
    def _():
        wb_ref[...] = w_ref[...].astype(BF16)

    o_ref[...] = _dot(h_ref[...], wb_ref[...]).astype(BF16)


def _proj(h, w):
    n, d = h.shape
    p = w.shape[1]
    tm = PROJ_TILE if n % PROJ_TILE == 0 else TOKEN_TILE
    tn = PROJ_COLS if p % PROJ_COLS == 0 else PROJ_COLS // 2
    assert n % tm == 0 and p % tn == 0
    return pl.pallas_call(
        _proj_kernel,
        grid=(p // tn, n // tm),
        in_specs=[pl.BlockSpec((tm, d), lambda j, i: (i, 0)),
                  pl.BlockSpec((d, tn), lambda j, i: (0, j))],
        out_specs=pl.BlockSpec((tm, tn), lambda j, i: (i, j)),
        out_shape=jax.ShapeDtypeStruct((n, p), BF16),
        scratch_shapes=[pltpu.VMEM((d, tn), BF16)],
        compiler_params=_params("arbitrary", "arbitrary"),
        name="in_proj",
    )(h, w)


def _rope_tables(pos):
    half = RET_DK // 2
    theta = 1.0 / (ROPE_BASE ** jnp.linspace(0.0, 1.0, half, dtype=F32))
    ang = pos.astype(F32)[:, None] * theta[None, :]
    cos, sin = jnp.cos(ang), jnp.sin(ang)
    return (jnp.repeat(cos, 2, axis=1),
            jnp.stack([-sin, sin], axis=-1).reshape(pos.shape[0], RET_DK))


def _rot(x, cos, sin_signed):
    lane = lax.broadcasted_iota(I32, x.shape, 1)
    width = x.shape[1]
    nbr = jnp.where((lane & 1) == 0, pltpu.roll(x, width - 1, 1), pltpu.roll(x, 1, 1))
    return x * cos + nbr * sin_signed


def _gate_out(o, g_ref_val):
    return (_rms(o) * _silu(g_ref_val.astype(F32))).astype(BF16)


def _ret_prompt_kernel(q_ref, k_ref, v_ref, g_ref, cos_ref, sin_ref, lg_ref, o_ref, s_out_ref, s_ref):
    c = RET_CHUNK
    seq = q_ref.shape[0]
    lg = lg_ref[0:1, 0:1]
    ti = lax.broadcasted_iota(I32, (c, c), 0)
    si = lax.broadcasted_iota(I32, (c, c), 1)
    decay = jnp.where(ti >= si, jnp.exp((ti - si).astype(F32) * lg), 0.0)
    tcol = lax.broadcasted_iota(I32, (c, 1), 0).astype(F32)
    dq = jnp.exp((tcol + 1.0) * lg)
    dk = jnp.exp((float(c - 1) - tcol) * lg)
    dchunk = jnp.exp(float(c) * lg)
    s_ref[...] = jnp.zeros_like(s_ref)

    def body(ci, carry):
        r0 = pl.multiple_of(ci * c, c)
        rows = pl.ds(r0, c)
        cos, sin = cos_ref[rows, :], sin_ref[rows, :]
        q = _rot(q_ref[rows, :].astype(F32), cos, sin)
        k = _rot(k_ref[rows, :].astype(F32), cos, sin) * (RET_DK ** -0.5)
        v = v_ref[rows, :]
        s = s_ref[...]
        scores = _dot_nt(q.astype(BF16), k.astype(BF16)) * decay
        o = _dot(scores.astype(BF16), v) + _dot((q * dq).astype(BF16), s.astype(BF16))
        s_ref[...] = dchunk * s + _dot_tn((k * dk).astype(BF16), v)
        o_ref[rows, :] = _gate_out(o, g_ref[rows, :])
        return carry

    lax.fori_loop(0, seq // c, body, 0)
    s_out_ref[...] = s_ref[...]


def _ret_prompt(proj, n_batch, seq, heads, cos, sin, log_gamma):
    qk_blocks = heads
    return pl.pallas_call(
        _ret_prompt_kernel,
        grid=(n_batch, heads),
        in_specs=[pl.BlockSpec((seq, RET_DK), lambda b, h: (b, h)),
                  pl.BlockSpec((seq, RET_DK), lambda b, h: (b, qk_blocks + h)),
                  pl.BlockSpec((seq, RET_DV), lambda b, h: (b, heads + h)),
                  pl.BlockSpec((seq, RET_DV), lambda b, h: (b, 2 * heads + h)),
                  pl.BlockSpec((seq, RET_DK), lambda b, h: (0, 0)),
                  pl.BlockSpec((seq, RET_DK), lambda b, h: (0, 0)),
                  pl.BlockSpec((None, 1, 128), lambda b, h: (h, 0, 0))],
        out_specs=[pl.BlockSpec((seq, RET_DV), lambda b, h: (b, h)),
                   pl.BlockSpec((None, None, RET_DK, RET_DV), lambda b, h: (b, h, 0, 0))],
        out_shape=[jax.ShapeDtypeStruct((n_batch * seq, heads * RET_DV), BF16),
                   jax.ShapeDtypeStruct((n_batch, heads, RET_DK, RET_DV), F32)],
        scratch_shapes=[pltpu.VMEM((RET_DK, RET_DV), F32)],
        compiler_params=_params("arbitrary", "arbitrary"),
        name="ret_prompt",
    )(proj, proj, proj, proj, cos, sin, log_gamma)


def _ret_sample_kernel(steps, q_ref, k_ref, v_ref, g_ref, cos_ref, sin_ref, lg_ref, s_in_ref, o_ref, s_out_ref):
    rows = q_ref.shape[0]
    shift = int(math.log2(steps))
    cos, sin = cos_ref[...], sin_ref[...]
    ri = lax.broadcasted_iota(I32, (rows, rows), 0)
    ci = lax.broadcasted_iota(I32, (rows, rows), 1)
    pair = ((ri >> shift) == (ci >> shift)) & (ri >= ci)
    rid = lax.broadcasted_iota(I32, (rows, 1), 0)
    step = (rid & (steps - 1)).astype(F32)
    row_batch = rid >> shift
    for hh in range(q_ref.shape[1] // RET_DK):
        kc = slice(hh * RET_DK, (hh + 1) * RET_DK)
        vc = slice(hh * RET_DV, (hh + 1) * RET_DV)
        lg = lg_ref[hh, 0:1, 0:1]
        q = _rot(q_ref[:, kc].astype(F32), cos, sin)
        k = _rot(k_ref[:, kc].astype(F32), cos, sin) * (RET_DK ** -0.5)
        v = v_ref[:, vc]
        decay = jnp.where(pair, jnp.exp((ri - ci).astype(F32) * lg), 0.0)
        o = _dot((_dot_nt(q.astype(BF16), k.astype(BF16)) * decay).astype(BF16), v)
        qd = (q * jnp.exp((step + 1.0) * lg)).astype(BF16)
        kd = k * jnp.exp((float(steps - 1) - step) * lg)
        dall = jnp.exp(float(steps) * lg)
        for j in range(rows // steps):
            s = s_in_ref[j, hh]
            o = o + jnp.where(row_batch == j, _dot(qd, s.astype(BF16)), 0.0)
            kj = jnp.where(row_batch == j, kd, 0.0).astype(BF16)
            s_out_ref[j, hh] = dall * s + _dot_tn(kj, v)
        o_ref[:, vc] = _gate_out(o, g_ref[:, vc])


def _ret_sample(proj, state, n_prompt, steps, heads, cos, sin, log_gamma):
    n_sample_batch = state.shape[0]
    per = SAMPLE_ROWS // steps
    base = n_prompt // SAMPLE_ROWS
    hs = min(RET_SAMPLE_HEADS, heads)
    groups = heads // hs
    kw, vw = hs * RET_DK, hs * RET_DV
    return pl.pallas_call(
        functools.partial(_ret_sample_kernel, steps),
        grid=(n_sample_batch // per, groups),
        in_specs=[pl.BlockSpec((SAMPLE_ROWS, kw), lambda b, h: (base + b, h)),
                  pl.BlockSpec((SAMPLE_ROWS, kw), lambda b, h: (base + b, groups + h)),
                  pl.BlockSpec((SAMPLE_ROWS, vw), lambda b, h: (base + b, groups + h)),
                  pl.BlockSpec((SAMPLE_ROWS, vw), lambda b, h: (base + b, 2 * groups + h)),
                  pl.BlockSpec((SAMPLE_ROWS, RET_DK), lambda b, h: (0, 0)),
                  pl.BlockSpec((SAMPLE_ROWS, RET_DK), lambda b, h: (0, 0)),
                  pl.BlockSpec((hs, 1, 128), lambda b, h: (h, 0, 0)),
                  pl.BlockSpec((per, hs, RET_DK, RET_DV), lambda b, h: (b, h, 0, 0))],
        out_specs=[pl.BlockSpec((SAMPLE_ROWS, vw), lambda b, h: (b, h)),
                   pl.BlockSpec((per, hs, RET_DK, RET_DV), lambda b, h: (b, h, 0, 0))],
        out_shape=[jax.ShapeDtypeStruct((n_sample_batch * steps, heads * RET_DV), BF16),
                   jax.ShapeDtypeStruct(state.shape, F32)],
        compiler_params=_params("arbitrary", "arbitrary"),
        name="ret_sample",
    )(proj, proj, proj, proj, cos, sin, log_gamma, state)


def _hg_prefix_matrix():
    c = HG_CHUNK
    t = np.arange(c)[:, None]
    s = np.arange(c)[None, :]
    le = (s <= t).astype(np.float32)
    mats = [le, le * ((s // HG_LEAF) == (t // HG_LEAF))]
    for half in HG_LEVELS:
        mid = (t // (2 * half)) * (2 * half) + half - 1
        mats.append(le - (s <= mid).astype(np.float32))
    return jnp.asarray(np.concatenate(mats, axis=0), dtype=BF16)


def _hg_gates(q, f, lb):
    forget = lb + (1.0 - lb) * jax.nn.sigmoid(f)
    return _silu(q), 1.0 - forget, jnp.log(forget)


def _hg_guard_kernel(f_ref, lb_ref, o_ref):
    @pl.when(pl.program_id(0) == 0)
    def _():
        o_ref[...] = jnp.zeros_like(o_ref)

    rows, width = f_ref.shape
    lb = lb_ref[...]

    def leaf(li, lowest):
        r = pl.ds(pl.multiple_of(li * HG_LEAF, HG_LEAF), HG_LEAF)
        lf = jnp.log(lb + (1.0 - lb) * jax.nn.sigmoid(f_ref[r, :].astype(F32)))
        return jnp.minimum(lowest, jnp.sum(lf, axis=0, keepdims=True))

    lowest = lax.fori_loop(0, rows // HG_LEAF, leaf, jnp.zeros((1, width), F32), unroll=ROW_CHUNK_UNROLL)
    o_ref[...] = jnp.minimum(o_ref[...], jnp.min(lowest))


def _hg_guard(proj, n_prompt, heads, lb):
    d = heads * HG_DH
    tile = PROJ_TILE
    return pl.pallas_call(
        _hg_guard_kernel,
        grid=(n_prompt // tile,),
        in_specs=[pl.BlockSpec((tile, d), lambda i: (i, 1)),
                  pl.BlockSpec((1, d), lambda i: (0, 0))],
        out_specs=pl.BlockSpec((8, 128), lambda i: (0, 0)),
        out_shape=jax.ShapeDtypeStruct((8, 128), F32),
        compiler_params=_params("arbitrary"),
        name="hg_guard",
    )(proj, lb)


def _hg_prompt_kernel(exact_leaf, q_ref, f_ref, i_ref, g_ref, lb_ref, gn_ref, pm_ref, o_ref, s_out_ref, s_ref):
    c = HG_CHUNK
    seq = q_ref.shape[0]
    n_heads = q_ref.shape[1] // HG_DH
    ti = lax.broadcasted_iota(I32, (c, c), 0)
    si = lax.broadcasted_iota(I32, (c, c), 1)
    leaf_shift = int(math.log2(HG_LEAF))
    mask_leaf = (si <= ti) & ((ti >> leaf_shift) == (si >> leaf_shift))
    level_masks = []
    for half in HG_LEVELS:
        sh = int(math.log2(2 * half))
        level_masks.append(((ti >> sh) == (si >> sh)) & ((ti & (2 * half - 1)) >= half) & ((si & (2 * half - 1)) < half))
    eye = ti == si
    leaf_pos = lax.broadcasted_iota(I32, (c, 1), 0) & (HG_LEAF - 1)
    s_ref[...] = jnp.zeros_like(s_ref)

    width = n_heads * HG_DH
    head_cols = [slice(hh * HG_DH, (hh + 1) * HG_DH) for hh in range(n_heads)]

    def body(ci, carry):
        rows = pl.ds(pl.multiple_of(ci * c, c), c)
        qh, kk, lf = _hg_gates(q_ref[rows, :].astype(F32), f_ref[rows, :].astype(F32), lb_ref[...])
        sums = _dot(pm_ref[...], jnp.concatenate(_split2(lf), axis=1))
        sums = sums[:, :width] + sums[:, width:]
        b = sums[0:c]
        if exact_leaf:
            a = [jnp.zeros((c, c), F32) for _ in head_cols]
            vf = i_ref[rows, :].astype(F32)
            o_leaf = [jnp.zeros((c, HG_DH), F32) for _ in head_cols]
            for dist in range(HG_LEAF):
                if dist == 0:
                    k_s, b_s, v_s = kk, b, vf
                else:
                    k_s, b_s, v_s = pltpu.roll(kk, dist, 0), pltpu.roll(b, dist, 0), pltpu.roll(vf, dist, 0)
                pair = qh * k_s * jnp.exp(jnp.minimum(b - b_s, 0.0))
                for hh, hc in enumerate(head_cols):
                    w = jnp.sum(pair[:, hc], axis=1, keepdims=True)
                    o_leaf[hh] = o_leaf[hh] + jnp.where(leaf_pos >= dist, w, 0.0) * v_s[:, hc]
        else:
            d_leaf = sums[c:2 * c]
            q_f = (qh * jnp.exp(d_leaf)).astype(BF16)
            k_f = (kk * jnp.exp(-d_leaf)).astype(BF16)
            a = [jnp.where(mask_leaf, _dot_nt(q_f[:, hc], k_f[:, hc]), 0.0) for hc in head_cols]
            o_leaf = [0.0 for _ in head_cols]
        for lvl, mask in enumerate(level_masks):
            w = jnp.exp(-jnp.abs(sums[(2 + lvl) * c:(3 + lvl) * c]))
            q_f = (qh * w).astype(BF16)
            k_f = (kk * w).astype(BF16)
            a = [a_h + jnp.where(mask, _dot_nt(q_f[:, hc], k_f[:, hc]), 0.0) for a_h, hc in zip(a, head_cols)]
        b_last = b[c - 1:c, :]
        q_b = (qh * jnp.exp(b)).astype(BF16)
        k_b = (kk * jnp.exp(b_last - b)).astype(BF16)
        e_last = jnp.exp(b_last)
        gate = gn_ref[...] * _silu(g_ref[rows, :].astype(F32))
        for hh, hc in enumerate(head_cols):
            v = i_ref[rows, hc]
            s = s_ref[hh]
            o = _dot(a[hh].astype(BF16), v) + _dot(q_b[:, hc], s.astype(BF16)) + o_leaf[hh]
            col = jnp.sum(jnp.where(eye, e_last[:, hc], 0.0), axis=1, keepdims=True)
            s_ref[hh] = col * s + _dot_tn(k_b[:, hc], v)
            o_ref[rows, hc] = (_rms(o) * gate[:, hc]).astype(BF16)
        return carry

    lax.fori_loop(0, seq // c, body, 0)
    s_out_ref[...] = s_ref[...]


def _hg_prompt(exact_leaf, proj, lb, out_norm, *, n_batch, seq, heads):
    pm = _hg_prefix_matrix()
    hp = HG_HEADS_PER_STEP
    width = hp * HG_DH
    groups = heads // hp
    col = lambda part: (lambda b, h: (b, part * groups + h))
    return pl.pallas_call(
        functools.partial(_hg_prompt_kernel, exact_leaf),
        grid=(n_batch, groups),
        in_specs=[pl.BlockSpec((seq, width), col(0)),
                  pl.BlockSpec((seq, width), col(1)),
                  pl.BlockSpec((seq, width), col(2)),
                  pl.BlockSpec((seq, width), col(3)),
                  pl.BlockSpec((1, width), lambda b, h: (0, h)),
                  pl.BlockSpec((1, width), lambda b, h: (0, h)),
                  pl.BlockSpec(pm.shape, lambda b, h: (0, 0))],
        out_specs=[pl.BlockSpec((seq, width), lambda b, h: (b, h)),
                   pl.BlockSpec((None, hp, HG_DH, HG_DH), lambda b, h: (b, h, 0, 0))],
        out_shape=[jax.ShapeDtypeStruct((n_batch * seq, heads * HG_DH), BF16),
                   jax.ShapeDtypeStruct((n_batch, heads, HG_DH, HG_DH), F32)],
        scratch_shapes=[pltpu.VMEM((hp, HG_DH, HG_DH), F32)],
        compiler_params=_params("arbitrary", "arbitrary"),
        name="hg_prompt",
    )(proj, proj, proj, proj, lb, out_norm, pm)


def _hg_sample_kernel(steps, heads, q_ref, f_ref, i_ref, g_ref, lb_ref, gn_ref, s_in_ref, o_ref, s_out_ref):
    rows = q_ref.shape[0]
    shift = int(math.log2(steps))
    ri = lax.broadcasted_iota(I32, (rows, rows), 0)
    ci = lax.broadcasted_iota(I32, (rows, rows), 1)
    prefix = (((ri >> shift) == (ci >> shift)) & (ci <= ri)).astype(BF16)
    rid = lax.broadcasted_iota(I32, (rows, 1), 0)
    step = rid & (steps - 1)
    row_batch = rid >> shift
    ki = lax.broadcasted_iota(I32, (HG_DH, HG_DH), 0)
    vi = lax.broadcasted_iota(I32, (HG_DH, HG_DH), 1)
    eye = ki == vi

    def head(h):
        cols = pl.ds(pl.multiple_of(h * HG_DH, HG_DH), HG_DH)
        qh, kk, lf = _hg_gates(q_ref[:, cols].astype(F32), f_ref[:, cols].astype(F32), lb_ref[:, cols])
        v = i_ref[:, cols]
        vf = v.astype(F32)
        hi, lo = _split2(lf)
        b = _dot(prefix, jnp.concatenate([hi, lo], axis=1))
        b = b[:, :HG_DH] + b[:, HG_DH:]
        o = jnp.zeros((rows, HG_DH), F32)
        for dist in range(steps):
            if dist == 0:
                k_s, b_s, v_s = kk, b, vf
            else:
                k_s, b_s, v_s = (pltpu.roll(kk, dist, 0), pltpu.roll(b, dist, 0), pltpu.roll(vf, dist, 0))
            w = jnp.sum(qh * k_s * jnp.exp(jnp.minimum(b - b_s, 0.0)), axis=1, keepdims=True)
            o = o + jnp.where(step >= dist, w, 0.0) * v_s
        qd = (qh * jnp.exp(b)).astype(BF16)
        for j in range(rows // steps):
            s = s_in_ref[j, h]
            o = o + jnp.where(row_batch == j, _dot(qd, s.astype(BF16)), 0.0)
            b_last = b[(j + 1) * steps - 1:(j + 1) * steps, :]
            col = jnp.sum(jnp.where(eye, jnp.exp(b_last), 0.0), axis=1, keepdims=True)
            kj = jnp.where(row_batch == j, kk * jnp.exp(jnp.minimum(b_last - b, 0.0)), 0.0).astype(BF16)
            s_out_ref[j, h] = col * s + _dot_tn(kj, v)
        o_ref[:, cols] = (_rms(o) * gn_ref[:, cols] * _silu(g_ref[:, cols].astype(F32))).astype(BF16)

    def group(gi, carry):
        for hh in range(HG_HEADS_PER_STEP):
            head(gi * HG_HEADS_PER_STEP + hh)
        return carry

    lax.fori_loop(0, heads // HG_HEADS_PER_STEP, group, 0)


def _hg_sample(proj, state, n_prompt, steps, heads, lb, out_norm):
    n_sample_batch = state.shape[0]
    d = heads * HG_DH
    per = SAMPLE_ROWS // steps
    base = n_prompt // SAMPLE_ROWS
    col = lambda part: (lambda b: (base + b, part))
    return pl.pallas_call(
        functools.partial(_hg_sample_kernel, steps, heads),
        grid=(n_sample_batch // per,),
        in_specs=[pl.BlockSpec((SAMPLE_ROWS, d), col(0)),
                  pl.BlockSpec((SAMPLE_ROWS, d), col(1)),
                  pl.BlockSpec((SAMPLE_ROWS, d), col(2)),
                  pl.BlockSpec((SAMPLE_ROWS, d), col(3)),
                  pl.BlockSpec((1, d), lambda b: (0, 0)),
                  pl.BlockSpec((1, d), lambda b: (0, 0)),
                  pl.BlockSpec((per, heads, HG_DH, HG_DH), lambda b: (b, 0, 0, 0))],
        out_specs=[pl.BlockSpec((SAMPLE_ROWS, d), lambda b: (b, 0)),
                   pl.BlockSpec((per, heads, HG_DH, HG_DH), lambda b: (b, 0, 0, 0))],
        out_shape=[jax.ShapeDtypeStruct((n_sample_batch * steps, d), BF16),
                   jax.ShapeDtypeStruct(state.shape, F32)],
        compiler_params=_params("arbitrary"),
        name="hg_sample",
    )(proj, proj, proj, proj, lb, out_norm, state)


def _first_max(vals, lane, width):
    m = jnp.max(vals, axis=1, keepdims=True)
    idx = jnp.min(jnp.where(vals == m, lane, width), axis=1, keepdims=True)
    return m, idx


def _route(logits, bias):
    neg = -jnp.inf
    lane = lax.broadcasted_iota(I32, logits.shape, 1)
    group = lane >> GROUP_SHIFT
    scores = jax.nn.sigmoid(logits)
    biased = scores + bias
    best = sel = None
    for gi in range(N_GROUPS):
        vals = jnp.where(group == gi, biased, neg)
        m1, i1 = _first_max(vals, lane, N_EXPERTS)
        m2 = jnp.max(jnp.where(lane == i1, neg, vals), axis=1, keepdims=True)
        total = m1 + m2
        if gi == 0:
            best, sel = total, jnp.zeros_like(i1)
        else:
            better = total > best
            sel = jnp.where(better, gi, sel)
            best = jnp.where(better, total, best)
    vals = jnp.where(group == sel, biased, neg)
    _, e1 = _first_max(vals, lane, N_EXPERTS)
    _, e2 = _first_max(jnp.where(lane == e1, neg, vals), lane, N_EXPERTS)
    w1 = jnp.sum(jnp.where(lane == e1, scores, 0.0), axis=1, keepdims=True)
    w2 = jnp.sum(jnp.where(lane == e2, scores, 0.0), axis=1, keepdims=True)
    tot = w1 + w2
    return e1, e2, w1 / tot, w2 / tot


def _mixout_kernel(prompt_tiles, o_p, o_s, w_ref, x_p, x_s, g1_ref, g2_ref, gm_p, gm_s, sh_p, sh_s, sc_p, sc_s,
                   rw_ref, rb_ref, xo_p, xo_s, h_ref, ridx_ref, rwt_ref, cnt_ref, carry_ref, y_ref):
    i = pl.program_id(0)
    tm = x_p.shape[0]

    @pl.when(i == 0)
    def _():
        carry_ref[...] = jnp.zeros_like(carry_ref)

    def tile(sample):
        o_ref, x_ref, xo_ref = (o_s, x_s, xo_s) if sample else (o_p, x_p, xo_p)
        y_ref[...] = _dot(o_ref[...], w_ref[...])

        def chunk(rows, _):
            x = x_ref[rows, :] + _mod_rows(sample, gm_p, gm_s, rows) * (_rms(y_ref[rows, :]) * g1_ref[...])
            xo_ref[rows, :] = x
            h_ref[rows, :] = (_rms(x) * g2_ref[...] * (1.0 + _mod_rows(sample, sc_p, sc_s, rows))
                              + _mod_rows(sample, sh_p, sh_s, rows))

        _row_chunks(tm, chunk)

    _by_group(i >= prompt_tiles, tile)

    h1, h2 = _split2(h_ref[...])
    w1, w2 = _split2(rw_ref[...])
    first = _dot(h1, jnp.concatenate([w1, w2], axis=1))
    logits = first[:, :N_EXPERTS] + (first[:, N_EXPERTS:] + _dot(h2, w1))
    e1, e2, p1, p2 = _route(logits, rb_ref[...])

    lane = lax.broadcasted_iota(I32, (tm, N_EXPERTS), 1)
    hot1, hot2 = lane == e1, lane == e2
    onehot = (hot1 | hot2).astype(BF16)
    ti = lax.broadcasted_iota(I32, (tm, tm), 0)
    si = lax.broadcasted_iota(I32, (tm, tm), 1)
    before = _dot((si < ti).astype(BF16), onehot) + carry_ref[...]
    r1 = jnp.sum(jnp.where(hot1, before, 0.0), axis=1, keepdims=True).astype(I32)
    r2 = jnp.sum(jnp.where(hot2, before, 0.0), axis=1, keepdims=True).astype(I32)
    carry = carry_ref[...] + jnp.sum(onehot.astype(F32), axis=0, keepdims=True)
    carry_ref[...] = carry

    wide = lax.broadcasted_iota(I32, (tm, 128), 1)
    ridx_ref[...] = jnp.where(wide == 0, e1, jnp.where(wide == 1, e2, jnp.where(wide == 2, r1, r2)))
    rwt_ref[...] = jnp.where(wide == 0, p1, p2)
    cnt_ref[...] = jnp.zeros_like(cnt_ref)
    cnt_ref[0:1, 0:N_EXPERTS] = carry


def _mixout(tok, o_prompt, o_sample, w_out, x_p, x_s, g1, g2, mod_p, mod_s, router_w, router_b):
    d = tok.d
    v = o_prompt.shape[1]
    return pl.pallas_call(
        functools.partial(_mixout_kernel, tok.prompt_tiles),
        grid=(tok.tiles,),
        in_specs=tok.split_specs(v)
                 + [pl.BlockSpec((v, d), lambda i: (0, 0), pipeline_mode=pl.Buffered(1))]
                 + tok.split_specs(d) + [_const_spec((1, d)), _const_spec((1, d))]
                 + tok.mod_specs(2) + tok.mod_specs(3) + tok.mod_specs(4)
                 + [_const_spec((d, N_EXPERTS)), _const_spec((1, N_EXPERTS))],
        out_specs=tok.split_specs(d, out=True) + [tok.row_spec(d), tok.row_spec(128), tok.row_spec(128),
                                        _const_spec((8, 128))],
        out_shape=tok.split_shapes(d, F32) + [jax.ShapeDtypeStruct((tok.n, d), F32),
                                              jax.ShapeDtypeStruct((tok.n, 128), I32),
                                              jax.ShapeDtypeStruct((tok.n, 128), F32),
                                              jax.ShapeDtypeStruct((8, 128), F32)],
        scratch_shapes=[pltpu.VMEM((1, N_EXPERTS), F32), pltpu.VMEM((TOKEN_TILE, d), F32)],
        compiler_params=_params("arbitrary"),
        name="mix_out",
    )(o_prompt, o_sample, w_out, x_p, x_s, g1, g2, mod_p, mod_s, mod_p, mod_s, mod_p, mod_s, router_w, router_b)


def _expert_plan(ridx, counts, n_tiles):
    n = ridx.shape[0]
    expert = ridx[:, 0:2]
    rank = ridx[:, 2:4]
    cnt = counts[0, :N_EXPERTS].astype(I32)
    padded = ((cnt + EXPERT_TILE - 1) // EXPERT_TILE) * EXPERT_TILE
    ends = jnp.cumsum(padded)
    starts = ends - padded
    pos = starts[expert] + rank
    tile = jnp.arange(n_tiles, dtype=I32)
    tile_start = tile * EXPERT_TILE
    used = tile_start < ends[-1]
    last_used = jnp.maximum(ends[-1] // EXPERT_TILE - 1, 0)
    tile_expert = jnp.sum((tile_start[:, None] >= ends[None, :]).astype(I32), axis=1)
    tile_expert = jnp.minimum(jnp.where(used, tile_expert, tile_expert[last_used]), N_EXPERTS - 1)
    valid = jnp.clip(cnt[tile_expert] - (tile_start - starts[tile_expert]), 0, EXPERT_TILE)
    valid = jnp.where(used, valid, 0)
    pos = pos.reshape(n // TOKEN_TILE, TOKEN_TILE, 2).transpose(0, 2, 1)
    sub = jnp.arange(EXPERT_F_SPLIT, dtype=I32)[None, :]
    snake = jnp.where((tile[:, None] & 1) == 0, sub, EXPERT_F_SPLIT - 1 - sub)
    part = jnp.where(used[:, None], snake, snake[last_used, EXPERT_F_SPLIT - 1]).reshape(-1)
    return tile_expert, valid, jnp.minimum(tile, last_used), part, pos


def _start_rows(groups, copy):
    def body(g, carry):
        for j in range(SUBLANES):
            copy(g, j).start()
        return carry
    lax.fori_loop(0, groups, body, 0)


def _dispatch_kernel(pos_ref, h_ref, xs_in, xs_hbm, sem):
    del xs_in
    groups = h_ref.shape[0]
    for choice in range(2):
        def row(g, j, choice=choice):
            p = pos_ref[0, choice, g * SUBLANES + j]
            return pltpu.make_async_copy(h_ref.at[g, pl.ds(j, 1)],
                                         xs_hbm.at[p >> 3, pl.ds(p & (SUBLANES - 1), 1)], sem.at[choice])
        _start_rows(groups, row)
    for choice in range(2):
        pltpu.make_async_copy(h_ref, xs_hbm.at[pl.ds(0, groups)], sem.at[choice]).wait()


def _dispatch(h, pos, n_rows, buf=None):
    n, d = h.shape
    if buf is None:
        buf = jnp.zeros((n_rows // SUBLANES, SUBLANES, d), F32)
    tile_groups = TOKEN_TILE // SUBLANES
    return pl.pallas_call(
        _dispatch_kernel,
        grid=(n // TOKEN_TILE,),
        in_specs=[pl.BlockSpec((1, 2, TOKEN_TILE), lambda i: (i, 0, 0), memory_space=pltpu.SMEM),
                  pl.BlockSpec((tile_groups, SUBLANES, d), lambda i: (i, 0, 0)),
                  pl.BlockSpec(memory_space=pl.ANY)],
        out_specs=pl.BlockSpec(memory_space=pl.ANY),
        out_shape=jax.ShapeDtypeStruct((n_rows // SUBLANES, SUBLANES, d), F32),
        scratch_shapes=[pltpu.SemaphoreType.DMA((2,))],
        input_output_aliases={2: 0},
        compiler_params=_params("arbitrary"),
        name="dispatch",
    )(pos, h.reshape(n // SUBLANES, SUBLANES, d), buf)


def _expert_kernel(te_ref, nv_ref, tin_ref, part_ref, x_ref, wg_ref, wu_ref, wd_ref, y_ref):
    del te_ref, tin_ref, part_ref
    n = nv_ref[pl.program_id(0)]
    step = pl.program_id(1)
    rows = x_ref.shape[0] * SUBLANES

    @pl.when(n > 0)
    def _():
        x = x_ref[...].reshape(rows, x_ref.shape[2]).astype(BF16)
        a = _dot(x, wg_ref[...].astype(BF16))
        u = _dot(x, wu_ref[...].astype(BF16))
        y = _dot((_silu(a) * u).astype(BF16), wd_ref[...].astype(BF16)).reshape(y_ref.shape)

        @pl.when(step == 0)
        def _():
            y_ref[...] = y

        @pl.when(step > 0)
        def _():
            y_ref[...] += y

    @pl.when((n == 0) & (step == 0))
    def _():
        y_ref[...] = jnp.zeros_like(y_ref)


def _experts(xs, plan, layer, wg, wu, wd):
    d, f = wg.shape[2], wg.shape[3]
    fs = f // EXPERT_F_SPLIT
    tile_expert, valid, tile_in, part = plan[:4]
    n_tiles = tile_expert.shape[0]
    tile_groups = EXPERT_TILE // SUBLANES
    which = lambda i, s, part: part[i * EXPERT_F_SPLIT + s]
    grid_spec = pltpu.PrefetchScalarGridSpec(
        num_scalar_prefetch=4,
        grid=(n_tiles, EXPERT_F_SPLIT),
        in_specs=[pl.BlockSpec((tile_groups, SUBLANES, d), lambda i, s, te, nv, tin, part: (tin[i], 0, 0)),
                  pl.BlockSpec((None, None, d, fs),
                               lambda i, s, te, nv, tin, part: (layer, te[i], 0, which(i, s, part))),
                  pl.BlockSpec((None, None, d, fs),
                               lambda i, s, te, nv, tin, part: (layer, te[i], 0, which(i, s, part))),
                  pl.BlockSpec((None, None, fs, d),
                               lambda i, s, te, nv, tin, part: (layer, te[i], which(i, s, part), 0))],
        out_specs=pl.BlockSpec((tile_groups, SUBLANES, d), lambda i, s, te, nv, tin, part: (i, 0, 0)),
    )
    return pl.pallas_call(
        _expert_kernel,
        grid_spec=grid_spec,
        out_shape=jax.ShapeDtypeStruct((n_tiles * tile_groups, SUBLANES, d), F32),
        compiler_params=_params("arbitrary", "arbitrary"),
        name="experts",
    )(tile_expert, valid, tile_in, part, xs, wg, wu, wd)


def _by_group(is_sample, body):
    @pl.when(is_sample)
    def _():
        body(True)

    @pl.when(jnp.logical_not(is_sample))
    def _():
        body(False)


def _row_chunks(n_rows, fn):
    per = ROW_CHUNK // SUBLANES

    def body(ci, carry):
        fn(pl.ds(pl.multiple_of(ci * ROW_CHUNK, ROW_CHUNK), ROW_CHUNK), pl.ds(pl.multiple_of(ci * per, per), per))
        return carry
    lax.fori_loop(0, n_rows // ROW_CHUNK, body, 0, unroll=ROW_CHUNK_UNROLL)


def _mod_rows(sample, p_ref, s_ref, rows):
    return s_ref[rows, :] if sample else p_ref[...]


def _moeout_kernel(prompt_tiles, with_next, pos_ref, pos_next_ref, ys_hbm, x_p, x_s, rwt_ref, g3_ref, gf_p, gf_s,
                   *rest):
    if with_next:
        gn_ref, sh_p, sh_s, sc_p, sc_s, xo_p, xo_s, h_ref, ybuf, sem = rest
    else:
        xo_p, xo_s, ybuf, sem = rest
    groups = ybuf.shape[2]
    n_rows = groups * SUBLANES
    d = ybuf.shape[4]
    i = pl.program_id(0)
    slot = i & 1

    def fetch(p_ref, to_slot):
        for choice in range(2):
            def row(g, j, choice=choice):
                p = p_ref[0, choice, g * SUBLANES + j]
                return pltpu.make_async_copy(ys_hbm.at[p >> 3, pl.ds(p & (SUBLANES - 1), 1)],
                                             ybuf.at[to_slot, choice, g, pl.ds(j, 1)], sem.at[to_slot, choice])
            _start_rows(groups, row)

    @pl.when(i == 0)
    def _():
        fetch(pos_ref, 0)

    @pl.when(i + 1 < pl.num_programs(0))
    def _():
        fetch(pos_next_ref, 1 - slot)

    for choice in range(2):
        pltpu.make_async_copy(ys_hbm.at[pl.ds(0, groups)], ybuf.at[slot, choice], sem.at[slot, choice]).wait()

    def tile(sample):
        x_ref, xo_ref = (x_s, xo_s) if sample else (x_p, xo_p)

        def chunk(rows, grp):
            rw = rwt_ref[rows, :]
            y = (rw[:, 0:1] * ybuf[slot, 0, grp].reshape(ROW_CHUNK, d)
                 + rw[:, 1:2] * ybuf[slot, 1, grp].reshape(ROW_CHUNK, d))
            x = x_ref[rows, :] + _mod_rows(sample, gf_p, gf_s, rows) * (_rms(y) * g3_ref[...])
            xo_ref[rows, :] = x
            if with_next:
                h = (_rms(x) * gn_ref[...] * (1.0 + _mod_rows(sample, sc_p, sc_s, rows))
                     + _mod_rows(sample, sh_p, sh_s, rows))
                h_ref[rows, :] = h.astype(BF16)

        _row_chunks(n_rows, chunk)

    _by_group(pl.program_id(0) >= prompt_tiles, tile)


def _moeout(tok, pos, ys, x_p, x_s, rwt, g3, mod_p, mod_s, nxt=None):
    d = tok.d
    last = tok.tiles - 1
    ins = ([pl.BlockSpec((1, 2, TOKEN_TILE), lambda i: (i, 0, 0), memory_space=pltpu.SMEM),
            pl.BlockSpec((1, 2, TOKEN_TILE), lambda i: (jnp.minimum(i + 1, last), 0, 0), memory_space=pltpu.SMEM),
            pl.BlockSpec(memory_space=pl.ANY)]
           + tok.split_specs(d) + [tok.row_spec(128), _const_spec((1, d))] + tok.mod_specs(5))
    args = [pos, pos, ys, x_p, x_s, rwt, g3, mod_p, mod_s]
    scratch = [pltpu.VMEM((2, 2, TOKEN_TILE // SUBLANES, SUBLANES, d), F32), pltpu.SemaphoreType.DMA((2, 2))]
    if nxt is None:
        return pl.pallas_call(
            functools.partial(_moeout_kernel, tok.prompt_tiles, False),
            grid=(tok.tiles,), in_specs=ins, out_specs=tok.split_specs(d, out=True),
            out_shape=tok.split_shapes(d, F32), scratch_shapes=scratch,
            compiler_params=_params("arbitrary"), name="moe_out_last",
        )(*args)
    gain_n, mod_pn, mod_sn = nxt
    ins = ins + [_const_spec((1, d))] + tok.mod_specs(0) + tok.mod_specs(1)
    args = args + [gain_n, mod_pn, mod_sn, mod_pn, mod_sn]
    return pl.pallas_call(
        functools.partial(_moeout_kernel, tok.prompt_tiles, True),
        grid=(tok.tiles,), in_specs=ins, out_specs=tok.split_specs(d, out=True) + [tok.row_spec(d)],
        out_shape=tok.split_shapes(d, F32) + [jax.ShapeDtypeStruct((tok.n, d), BF16)],
        scratch_shapes=scratch,
        compiler_params=_params("arbitrary"), name="moe_out_next",
    )(*args)


def kernel(x_prompt, x_sample, c_prompt, c_sample, state_ret, state_hgrn, ada_w, ada_b, norm_gains,
           ret_w_in, ret_w_out, hg_w_in, hg_w_out, hg_lower_bound, hg_out_norm, router_w, router_b,
           exp_w_gate, exp_w_up, exp_w_down):
    n_batch, seq, d = x_prompt.shape
    n_dec, steps, _ = x_sample.shape
    depth = ada_w.shape[0]
    assert depth == 2 and d % RET_DK == 0 and d % HG_DH == 0
    assert seq % RET_CHUNK == 0 and (n_dec * steps) % TOKEN_TILE == 0 and SAMPLE_ROWS % steps == 0
    ret_heads = d // RET_DK
    hg_heads = d // HG_DH
    n_prompt = n_batch * seq
    n_sample = n_dec * steps
    tok = _Tokens(n_batch, seq, n_sample, d)
    n = tok.n

    n_cond = n_batch + n_dec
    pad = (-n_cond) % 8
    c_all = jnp.concatenate([c_prompt, c_sample, jnp.zeros((pad, d), F32)], axis=0)
    mod = _ada(c_all, ada_w, ada_b)
    mods = []
    for l in range(depth):
        mod_p = mod[l, :n_batch].reshape(n_batch, 6, 1, d)
        mod_s = jnp.repeat(mod[l, n_batch:n_cond], steps, axis=0)
        mods.append((mod_p, mod_s))
    gain = lambda l, k: norm_gains[l, k].reshape(1, d)

    x_p = x_prompt.reshape(n_prompt, d)
    x_s = x_sample.reshape(n_sample, d)
    n_tiles = (2 * n + N_EXPERTS * (EXPERT_TILE - 1)) // EXPERT_TILE + 1
    rw = router_w.astype(F32)
    rb = router_b.astype(F32).reshape(1, N_EXPERTS)

    def channel_mixer(l, x_p, x_s, h, ridx, rwt, counts, nxt, buf):
        plan = _expert_plan(ridx, counts, n_tiles)
        pos = plan[4]
        xs = _dispatch(h, pos, n_tiles * EXPERT_TILE, buf)
        ys = _experts(xs, plan, l, exp_w_gate, exp_w_up, exp_w_down)
        return _moeout(tok, pos, ys, x_p, x_s, rwt, gain(l, 3), *mods[l], nxt=nxt), xs

    h = _prenorm(tok, x_p, x_s, gain(0, 0), *mods[0])
    proj = _proj(h, ret_w_in[0])
    log_gamma = jnp.log(1.0 - jnp.exp2(-5.0 - jnp.arange(ret_heads, dtype=F32)))
    log_gamma = jnp.broadcast_to(log_gamma[:, None, None], (ret_heads, 1, 128))
    cos_p, sin_p = _rope_tables(jnp.arange(seq))
    cos_s, sin_s = _rope_tables(PAST_LEN + jnp.arange(steps))
    reps = SAMPLE_ROWS // steps
    cos_s, sin_s = jnp.tile(cos_s, (reps, 1)), jnp.tile(sin_s, (reps, 1))
    o_p, ret_prompt = _ret_prompt(proj, n_batch, seq, ret_heads, cos_p, sin_p, log_gamma)
    o_s, ret_sample = _ret_sample(proj, state_ret[0], n_prompt, steps, ret_heads, cos_s, sin_s, log_gamma)
    x_p, x_s, hp, ridx, rwt, counts = _mixout(tok, o_p, o_s, ret_w_out[0].astype(BF16), x_p, x_s,
                                              gain(0, 1), gain(0, 2), *mods[0], rw, rb)
    (x_p, x_s, h), row_buf = channel_mixer(0, x_p, x_s, hp, ridx, rwt, counts, (gain(1, 0),) + mods[1], None)

    sm = jax.nn.softmax(hg_lower_bound.astype(F32), axis=0)
    lb = (jnp.cumsum(sm, axis=0) - sm[0])[1].reshape(1, d)
    proj = _proj(h, hg_w_in[0])
    out_norm = hg_out_norm[0].reshape(1, d)
    leaf_min = _hg_guard(proj, n_prompt, hg_heads, lb)[0, 0]
    o_p, hg_prompt = lax.cond(
        leaf_min > HG_LEAF_LOG_LIMIT,
        functools.partial(_hg_prompt, False, n_batch=n_batch, seq=seq, heads=hg_heads),
        functools.partial(_hg_prompt, True, n_batch=n_batch, seq=seq, heads=hg_heads),
        proj, lb, out_norm)
    o_s, hg_sample = _hg_sample(proj, state_hgrn[0], n_prompt, steps, hg_heads, lb, out_norm)
    x_p, x_s, hp, ridx, rwt, counts = _mixout(tok, o_p, o_s, hg_w_out[0].astype(BF16), x_p, x_s,
                                              gain(1, 1), gain(1, 2), *mods[1], rw, rb)
    (x_p, x_s), _ = channel_mixer(1, x_p, x_s, hp, ridx, rwt, counts, None, row_buf)

    return (x_p.reshape(n_batch, seq, d), x_s.reshape(n_dec, steps, d),
            ret_prompt[None], hg_prompt[None], ret_sample[None], hg_sample[None])
```

```python
import functools
import math

import numpy as np
import jax
import jax.numpy as jnp
from jax import lax
from jax.experimental import pallas as pl
from jax.experimental.pallas import tpu as pltpu

F32, BF16, I32 = jnp.float32, jnp.bfloat16, jnp.int32

EPS = 1e-6
ROPE_BASE = 10000.0
PAST_LEN = 16384
RET_DK = 256
RET_DV = 512
HG_DH = 128
N_EXPERTS = 16
GROUP_SHIFT = 2
N_GROUPS = N_EXPERTS >> GROUP_SHIFT

TOKEN_TILE = 256
PROJ_TILE = 512
PROJ_COLS = 2048
EXPERT_TILE = 384
RET_CHUNK = 256
HG_CHUNK = 128
HG_LEAF = 16
HG_LEVELS = (64, 32, 16)
HG_HEADS_PER_STEP = 4
HG_LEAF_LOG_LIMIT = -80.0
SAMPLE_ROWS = 16
SUBLANES = 8
EXPERT_F_SPLIT = 2
RET_SAMPLE_HEADS = 4
ROW_CHUNK = 16
ROW_CHUNK_UNROLL = 4
VMEM_LIMIT = 56 * 1024 * 1024


def _params(*sem):
    return pltpu.CompilerParams(dimension_semantics=sem, vmem_limit_bytes=VMEM_LIMIT)


def _rms(x):
    return x * lax.rsqrt(jnp.mean(x * x, axis=-1, keepdims=True) + EPS)


def _silu(x):
    return x * jax.nn.sigmoid(x)


def _dot(a, b):
    return jnp.dot(a, b, preferred_element_type=F32)


def _dot_nt(a, b):
    return lax.dot_general(a, b, (((1,), (1,)), ((), ())), preferred_element_type=F32)


def _dot_tn(a, b):
    return lax.dot_general(a, b, (((0,), (0,)), ((), ())), preferred_element_type=F32)


def _split2(x):
    hi = x.astype(BF16)
    lo = (x - hi.astype(F32)).astype(BF16)
    return hi, lo


def _split3(x):
    hi = x.astype(BF16)
    r = x - hi.astype(F32)
    mid = r.astype(BF16)
    lo = (r - mid.astype(F32)).astype(BF16)
    return hi, mid, lo


def _ada_kernel(c_ref, w_ref, b_ref, o_ref):
    s = _silu(c_ref[...]).astype(BF16)
    o_ref[...] = _dot(s, w_ref[...].astype(BF16)) + b_ref[...]


def _ada(c_all, ada_w, ada_b):
    depth, d, d6 = ada_w.shape
    m = c_all.shape[0]
    tn = min(1024, d6)
    return pl.pallas_call(
        _ada_kernel,
        grid=(depth, d6 // tn),
        in_specs=[pl.BlockSpec((m, d), lambda l, j: (0, 0)),
                  pl.BlockSpec((None, d, tn), lambda l, j: (l, 0, j)),
                  pl.BlockSpec((None, 1, tn), lambda l, j: (l, 0, j))],
        out_specs=pl.BlockSpec((None, m, tn), lambda l, j: (l, 0, j)),
        out_shape=jax.ShapeDtypeStruct((depth, m, d6), F32),
        compiler_params=_params("arbitrary", "arbitrary"),
        name="ada_mod",
    )(c_all, ada_w, ada_b.reshape(depth, 1, d6))


class _Tokens:
    def __init__(self, n_prompt_batch, seq, n_sample, d):
        self.d = d
        self.n_prompt = n_prompt_batch * seq
        self.n = self.n_prompt + n_sample
        self.prompt_tiles = self.n_prompt // TOKEN_TILE
        self.tiles_per_batch = seq // TOKEN_TILE
        self.n_batch = n_prompt_batch
        self.tiles = self.n // TOKEN_TILE

    def mod_specs(self, comp):
        d, tpb, nb, npt = self.d, self.tiles_per_batch, self.n_batch, self.prompt_tiles
        return [pl.BlockSpec((None, None, 1, d), lambda i: (jnp.minimum(i // tpb, nb - 1), comp, 0, 0)),
                pl.BlockSpec((TOKEN_TILE, d), lambda i: (jnp.maximum(i - npt, 0), comp),
                             pipeline_mode=pl.Buffered(1))]

    def row_spec(self, width):
        return pl.BlockSpec((TOKEN_TILE, width), lambda i: (i, 0))

    def split_specs(self, width, out=False):
        npt = self.prompt_tiles
        mode = {} if out else dict(pipeline_mode=pl.Buffered(1))
        return [pl.BlockSpec((TOKEN_TILE, width), lambda i: (jnp.minimum(i, npt - 1), 0)),
                pl.BlockSpec((TOKEN_TILE, width), lambda i: (jnp.maximum(i - npt, 0), 0), **mode)]

    def split_shapes(self, width, dtype):
        return [jax.ShapeDtypeStruct((self.n_prompt, width), dtype),
                jax.ShapeDtypeStruct((self.n - self.n_prompt, width), dtype)]


def _const_spec(shape):
    return pl.BlockSpec(shape, lambda i: (0,) * len(shape))


def _pick(is_sample, p_ref, s_ref):
    return jnp.where(is_sample, s_ref[...], p_ref[...])


def _store_split(is_sample, p_ref, s_ref, val):
    @pl.when(is_sample)
    def _():
        s_ref[...] = val

    @pl.when(jnp.logical_not(is_sample))
    def _():
        p_ref[...] = val


def _prenorm_kernel(prompt_tiles, x_p, x_s, g_ref, sh_p, sh_s, sc_p, sc_s, h_ref):
    def tile(sample):
        x_ref = x_s if sample else x_p

        def chunk(rows, _):
            h = (_rms(x_ref[rows, :]) * g_ref[...] * (1.0 + _mod_rows(sample, sc_p, sc_s, rows))
                 + _mod_rows(sample, sh_p, sh_s, rows))
            h_ref[rows, :] = h.astype(BF16)

        _row_chunks(x_p.shape[0], chunk)

    _by_group(pl.program_id(0) >= prompt_tiles, tile)


def _prenorm(tok, x_p, x_s, gain, mod_p, mod_s):
    d = tok.d
    return pl.pallas_call(
        functools.partial(_prenorm_kernel, tok.prompt_tiles),
        grid=(tok.tiles,),
        in_specs=tok.split_specs(d) + [_const_spec((1, d))] + tok.mod_specs(0) + tok.mod_specs(1),
        out_specs=tok.row_spec(d),
        out_shape=jax.ShapeDtypeStruct((tok.n, d), BF16),
        compiler_params=_params("arbitrary"),
        name="prenorm",
    )(x_p, x_s, gain, mod_p, mod_s, mod_p, mod_s)


def _proj_kernel(h_ref, w_ref, o_ref, wb_ref):
    @pl.when(pl.program_id(1) == 0)
    def _():
        wb_ref[...] = w_ref[...].astype(BF16)

    o_ref[...] = _dot(h_ref[...], wb_ref[...]).astype(BF16)


def _proj(h, w):
    n, d = h.shape
    p = w.shape[1]
    tm = PROJ_TILE if n % PROJ_TILE == 0 else TOKEN_TILE
    tn = PROJ_COLS if p % PROJ_COLS == 0 else PROJ_COLS // 2
    assert n % tm == 0 and p % tn == 0
    return pl.pallas_call(
        _proj_kernel,
        grid=(p // tn, n // tm),
        in_specs=[pl.BlockSpec((tm, d), lambda j, i: (i, 0)),
                  pl.BlockSpec((d, tn), lambda j, i: (0, j))],
        out_specs=pl.BlockSpec((tm, tn), lambda j, i: (i, j)),
        out_shape=jax.ShapeDtypeStruct((n, p), BF16),
        scratch_shapes=[pltpu.VMEM((d, tn), BF16)],
        compiler_params=_params("arbitrary", "arbitrary"),
        name="in_proj",
    )(h, w)


def _rope_tables(pos):
    half = RET_DK // 2
    theta = 1.0 / (ROPE_BASE ** jnp.linspace(0.0, 1.0, half, dtype=F32))
    ang = pos.astype(F32)[:, None] * theta[None, :]
    cos, sin = jnp.cos(ang), jnp.sin(ang)
    return (jnp.repeat(cos, 2, axis=1),
            jnp.stack([-sin, sin], axis=-1).reshape(pos.shape[0], RET_DK))


def _rot(x, cos, sin_signed):
    lane = lax.broadcasted_iota(I32, x.shape, 1)
    width = x.shape[1]
    nbr = jnp.where((lane & 1) == 0, pltpu.roll(x, width - 1, 1), pltpu.roll(x, 1, 1))
    return x * cos + nbr * sin_signed


def _gate_out(o, g_ref_val):
    return (_rms(o) * _silu(g_ref_val.astype(F32))).astype(BF16)


def _ret_prompt_kernel(q_ref, k_ref, v_ref, g_ref, cos_ref, sin_ref, lg_ref, o_ref, s_out_ref, s_ref):
    c = RET_CHUNK
    seq = q_ref.shape[0]
    lg = lg_ref[0:1, 0:1]
    ti = lax.broadcasted_iota(I32, (c, c), 0)
    si = lax.broadcasted_iota(I32, (c, c), 1)
    decay = jnp.where(ti >= si, jnp.exp((ti - si).astype(F32) * lg), 0.0)
    tcol = lax.broadcasted_iota(I32, (c, 1), 0).astype(F32)
    dq = jnp.exp((tcol + 1.0) * lg)
    dk = jnp.exp((float(c - 1) - tcol) * lg)
    dchunk = jnp.exp(float(c) * lg)
    s_ref[...] = jnp.zeros_like(s_ref)

    def body(ci, carry):
        r0 = pl.multiple_of(ci * c, c)
        rows = pl.ds(r0, c)
        cos, sin = cos_ref[rows, :], sin_ref[rows, :]
        q = _rot(q_ref[rows, :].astype(F32), cos, sin)
        k = _rot(k_ref[rows, :].astype(F32), cos, sin) * (RET_DK ** -0.5)
        v = v_ref[rows, :]
        s = s_ref[...]
        scores = _dot_nt(q.astype(BF16), k.astype(BF16)) * decay
        o = _dot(scores.astype(BF16), v) + _dot((q * dq).astype(BF16), s.astype(BF16))
        s_ref[...] = dchunk * s + _dot_tn((k * dk).astype(BF16), v)
        o_ref[rows, :] = _gate_out(o, g_ref[rows, :])
        return carry

    lax.fori_loop(0, seq // c, body, 0)
    s_out_ref[...] = s_ref[...]


def _ret_prompt(proj, n_batch, seq, heads, cos, sin, log_gamma):
    qk_blocks = heads
    return pl.pallas_call(
        _ret_prompt_kernel,
        grid=(n_batch, heads),
        in_specs=[pl.BlockSpec((seq, RET_DK), lambda b, h: (b, h)),
                  pl.BlockSpec((seq, RET_DK), lambda b, h: (b, qk_blocks + h)),
                  pl.BlockSpec((seq, RET_DV), lambda b, h: (b, heads + h)),
                  pl.BlockSpec((seq, RET_DV), lambda b, h: (b, 2 * heads + h)),
                  pl.BlockSpec((seq, RET_DK), lambda b, h: (0, 0)),
                  pl.BlockSpec((seq, RET_DK), lambda b, h: (0, 0)),
                  pl.BlockSpec((None, 1, 128), lambda b, h: (h, 0, 0))],
        out_specs=[pl.BlockSpec((seq, RET_DV), lambda b, h: (b, h)),
                   pl.BlockSpec((None, None, RET_DK, RET_DV), lambda b, h: (b, h, 0, 0))],
        out_shape=[jax.ShapeDtypeStruct((n_batch * seq, heads * RET_DV), BF16),
                   jax.ShapeDtypeStruct((n_batch, heads, RET_DK, RET_DV), F32)],
        scratch_shapes=[pltpu.VMEM((RET_DK, RET_DV), F32)],
        compiler_params=_params("arbitrary", "arbitrary"),
        name="ret_prompt",
    )(proj, proj, proj, proj, cos, sin, log_gamma)


def _ret_sample_kernel(steps, q_ref, k_ref, v_ref, g_ref, cos_ref, sin_ref, lg_ref, s_in_ref, o_ref, s_out_ref):
    rows = q_ref.shape[0]
    shift = int(math.log2(steps))
    cos, sin = cos_ref[...], sin_ref[...]
    ri = lax.broadcasted_iota(I32, (rows, rows), 0)
    ci = lax.broadcasted_iota(I32, (rows, rows), 1)
    pair = ((ri >> shift) == (ci >> shift)) & (ri >= ci)
    rid = lax.broadcasted_iota(I32, (rows, 1), 0)
    step = (rid & (steps - 1)).astype(F32)
    row_batch = rid >> shift
    for hh in range(q_ref.shape[1] // RET_DK):
        kc = slice(hh * RET_DK, (hh + 1) * RET_DK)
        vc = slice(hh * RET_DV, (hh + 1) * RET_DV)
        lg = lg_ref[hh, 0:1, 0:1]
        q = _rot(q_ref[:, kc].astype(F32), cos, sin)
        k = _rot(k_ref[:, kc].astype(F32), cos, sin) * (RET_DK ** -0.5)
        v = v_ref[:, vc]
        decay = jnp.where(pair, jnp.exp((ri - ci).astype(F32) * lg), 0.0)
        o = _dot((_dot_nt(q.astype(BF16), k.astype(BF16)) * decay).astype(BF16), v)
        qd = (q * jnp.exp((step + 1.0) * lg)).astype(BF16)
        kd = k * jnp.exp((float(steps - 1) - step) * lg)
        dall = jnp.exp(float(steps) * lg)
        for j in range(rows // steps):
            s = s_in_ref[j, hh]
            o = o + jnp.where(row_batch == j, _dot(qd, s.astype(BF16)), 0.0)
            kj = jnp.where(row_batch == j, kd, 0.0).astype(BF16)
            s_out_ref[j, hh] = dall * s + _dot_tn(kj, v)
        o_ref[:, vc] = _gate_out(o, g_ref[:, vc])


def _ret_sample(proj, state, n_prompt, steps, heads, cos, sin, log_gamma):
    n_sample_batch = state.shape[0]
    per = SAMPLE_ROWS // steps
    base = n_prompt // SAMPLE_ROWS
    hs = min(RET_SAMPLE_HEADS, heads)
    groups = heads // hs
    kw, vw = hs * RET_DK, hs * RET_DV
    return pl.pallas_call(
        functools.partial(_ret_sample_kernel, steps),
        grid=(n_sample_batch // per, groups),
        in_specs=[pl.BlockSpec((SAMPLE_ROWS, kw), lambda b, h: (base + b, h)),
                  pl.BlockSpec((SAMPLE_ROWS, kw), lambda b, h: (base + b, groups + h)),
                  pl.BlockSpec((SAMPLE_ROWS, vw), lambda b, h: (base + b, groups + h)),
                  pl.BlockSpec((SAMPLE_ROWS, vw), lambda b, h: (base + b, 2 * groups + h)),
                  pl.BlockSpec((SAMPLE_ROWS, RET_DK), lambda b, h: (0, 0)),
                  pl.BlockSpec((SAMPLE_ROWS, RET_DK), lambda b, h: (0, 0)),
                  pl.BlockSpec((hs, 1, 128), lambda b, h: (h, 0, 0)),
                  pl.BlockSpec((per, hs, RET_DK, RET_DV), lambda b, h: (b, h, 0, 0))],
        out_specs=[pl.BlockSpec((SAMPLE_ROWS, vw), lambda b, h: (b, h)),
                   pl.BlockSpec((per, hs, RET_DK, RET_DV), lambda b, h: (b, h, 0, 0))],
        out_shape=[jax.ShapeDtypeStruct((n_sample_batch * steps, heads * RET_DV), BF16),
                   jax.ShapeDtypeStruct(state.shape, F32)],
        compiler_params=_params("arbitrary", "arbitrary"),
        name="ret_sample",
    )(proj, proj, proj, proj, cos, sin, log_gamma, state)


def _hg_prefix_matrix():
    c = HG_CHUNK
    t = np.arange(c)[:, None]
    s = np.arange(c)[None, :]
    le = (s <= t).astype(np.float32)
    mats = [le, le * ((s // HG_LEAF) == (t // HG_LEAF))]
    for half in HG_LEVELS:
        mid = (t // (2 * half)) * (2 * half) + half - 1
        mats.append(le - (s <= mid).astype(np.float32))
    return jnp.asarray(np.concatenate(mats, axis=0), dtype=BF16)


def _hg_gates(q, f, lb):
    forget = lb + (1.0 - lb) * jax.nn.sigmoid(f)
    return _silu(q), 1.0 - forget, jnp.log(forget)


def _hg_guard_kernel(f_ref, lb_ref, o_ref):
    @pl.when(pl.program_id(0) == 0)
    def _():
        o_ref[...] = jnp.zeros_like(o_ref)

    rows, width = f_ref.shape
    lb = lb_ref[...]

    def leaf(li, lowest):
        r = pl.ds(pl.multiple_of(li * HG_LEAF, HG_LEAF), HG_LEAF)
        lf = jnp.log(lb + (1.0 - lb) * jax.nn.sigmoid(f_ref[r, :].astype(F32)))
        return jnp.minimum(lowest, jnp.sum(lf, axis=0, keepdims=True))

    lowest = lax.fori_loop(0, rows // HG_LEAF, leaf, jnp.zeros((1, width), F32), unroll=ROW_CHUNK_UNROLL)
    o_ref[...] = jnp.minimum(o_ref[...], jnp.min(lowest))


def _hg_guard(proj, n_prompt, heads, lb):
    d = heads * HG_DH
    tile = PROJ_TILE
    return pl.pallas_call(
        _hg_guard_kernel,
        grid=(n_prompt // tile,),
        in_specs=[pl.BlockSpec((tile, d), lambda i: (i, 1)),
                  pl.BlockSpec((1, d), lambda i: (0, 0))],
        out_specs=pl.BlockSpec((8, 128), lambda i: (0, 0)),
        out_shape=jax.ShapeDtypeStruct((8, 128), F32),
        compiler_params=_params("arbitrary"),
        name="hg_guard",
    )(proj, lb)


def _hg_prompt_kernel(exact_leaf, q_ref, f_ref, i_ref, g_ref, lb_ref, gn_ref, pm_ref, o_ref, s_out_ref, s_ref):
    c = HG_CHUNK
    seq = q_ref.shape[0]
    n_heads = q_ref.shape[1] // HG_DH
    ti = lax.broadcasted_iota(I32, (c, c), 0)
    si = lax.broadcasted_iota(I32, (c, c), 1)
    leaf_shift = int(math.log2(HG_LEAF))
    mask_leaf = (si <= ti) & ((ti >> leaf_shift) == (si >> leaf_shift))
    level_masks = []
    for half in HG_LEVELS:
        sh = int(math.log2(2 * half))
        level_masks.append(((ti >> sh) == (si >> sh)) & ((ti & (2 * half - 1)) >= half) & ((si & (2 * half - 1)) < half))
    eye = ti == si
    leaf_pos = lax.broadcasted_iota(I32, (c, 1), 0) & (HG_LEAF - 1)
    s_ref[...] = jnp.zeros_like(s_ref)

    width = n_heads * HG_DH
    head_cols = [slice(hh * HG_DH, (hh + 1) * HG_DH) for hh in range(n_heads)]

    def body(ci, carry):
        rows = pl.ds(pl.multiple_of(ci * c, c), c)
        qh, kk, lf = _hg_gates(q_ref[rows, :].astype(F32), f_ref[rows, :].astype(F32), lb_ref[...])
        sums = _dot(pm_ref[...], jnp.concatenate(_split2(lf), axis=1))
        sums = sums[:, :width] + sums[:, width:]
        b = sums[0:c]
        if exact_leaf:
            a = [jnp.zeros((c, c), F32) for _ in head_cols]
            vf = i_ref[rows, :].astype(F32)
            o_leaf = [jnp.zeros((c, HG_DH), F32) for _ in head_cols]
            for dist in range(HG_LEAF):
                if dist == 0:
                    k_s, b_s, v_s = kk, b, vf
                else:
                    k_s, b_s, v_s = pltpu.roll(kk, dist, 0), pltpu.roll(b, dist, 0), pltpu.roll(vf, dist, 0)
                pair = qh * k_s * jnp.exp(jnp.minimum(b - b_s, 0.0))
                for hh, hc in enumerate(head_cols):
                    w = jnp.sum(pair[:, hc], axis=1, keepdims=True)
                    o_leaf[hh] = o_leaf[hh] + jnp.where(leaf_pos >= dist, w, 0.0) * v_s[:, hc]
        else:
            d_leaf = sums[c:2 * c]
            q_f = (qh * jnp.exp(d_leaf)).astype(BF16)
            k_f = (kk * jnp.exp(-d_leaf)).astype(BF16)
            a = [jnp.where(mask_leaf, _dot_nt(q_f[:, hc], k_f[:, hc]), 0.0) for hc in head_cols]
            o_leaf = [0.0 for _ in head_cols]
        for lvl, mask in enumerate(level_masks):
            w = jnp.exp(-jnp.abs(sums[(2 + lvl) * c:(3 + lvl) * c]))
            q_f = (qh * w).astype(BF16)
            k_f = (kk * w).astype(BF16)
            a = [a_h + jnp.where(mask, _dot_nt(q_f[:, hc], k_f[:, hc]), 0.0) for a_h, hc in zip(a, head_cols)]
        b_last = b[c - 1:c, :]
        q_b = (qh * jnp.exp(b)).astype(BF16)
        k_b = (kk * jnp.exp(b_last - b)).astype(BF16)
        e_last = jnp.exp(b_last)
        gate = gn_ref[...] * _silu(g_ref[rows, :].astype(F32))
        for hh, hc in enumerate(head_cols):
            v = i_ref[rows, hc]
            s = s_ref[hh]
            o = _dot(a[hh].astype(BF16), v) + _dot(q_b[:, hc], s.astype(BF16)) + o_leaf[hh]
            col = jnp.sum(jnp.where(eye, e_last[:, hc], 0.0), axis=1, keepdims=True)
            s_ref[hh] = col * s + _dot_tn(k_b[:, hc], v)
            o_ref[rows, hc] = (_rms(o) * gate[:, hc]).astype(BF16)
        return carry

    lax.fori_loop(0, seq // c, body, 0)
    s_out_ref[...] = s_ref[...]


def _hg_prompt(exact_leaf, proj, lb, out_norm, *, n_batch, seq, heads):
    pm = _hg_prefix_matrix()
    hp = HG_HEADS_PER_STEP
    width = hp * HG_DH
    groups = heads // hp
    col = lambda part: (lambda b, h: (b, part * groups + h))
    return pl.pallas_call(
        functools.partial(_hg_prompt_kernel, exact_leaf),
        grid=(n_batch, groups),
        in_specs=[pl.BlockSpec((seq, width), col(0)),
                  pl.BlockSpec((seq, width), col(1)),
                  pl.BlockSpec((seq, width), col(2)),
                  pl.BlockSpec((seq, width), col(3)),
                  pl.BlockSpec((1, width), lambda b, h: (0, h)),
                  pl.BlockSpec((1, width), lambda b, h: (0, h)),
                  pl.BlockSpec(pm.shape, lambda b, h: (0, 0))],
        out_specs=[pl.BlockSpec((seq, width), lambda b, h: (b, h)),
                   pl.BlockSpec((None, hp, HG_DH, HG_DH), lambda b, h: (b, h, 0, 0))],
        out_shape=[jax.ShapeDtypeStruct((n_batch * seq, heads * HG_DH), BF16),
                   jax.ShapeDtypeStruct((n_batch, heads, HG_DH, HG_DH), F32)],
        scratch_shapes=[pltpu.VMEM((hp, HG_DH, HG_DH), F32)],
        compiler_params=_params("arbitrary", "arbitrary"),
        name="hg_prompt",
    )(proj, proj, proj, proj, lb, out_norm, pm)


def _hg_sample_kernel(steps, heads, q_ref, f_ref, i_ref, g_ref, lb_ref, gn_ref, s_in_ref, o_ref, s_out_ref):
    rows = q_ref.shape[0]
    shift = int(math.log2(steps))
    ri = lax.broadcasted_iota(I32, (rows, rows), 0)
    ci = lax.broadcasted_iota(I32, (rows, rows), 1)
    prefix = (((ri >> shift) == (ci >> shift)) & (ci <= ri)).astype(BF16)
    rid = lax.broadcasted_iota(I32, (rows, 1), 0)
    step = rid & (steps - 1)
    row_batch = rid >> shift
    ki = lax.broadcasted_iota(I32, (HG_DH, HG_DH), 0)
    vi = lax.broadcasted_iota(I32, (HG_DH, HG_DH), 1)
    eye = ki == vi

    def head(h):
        cols = pl.ds(pl.multiple_of(h * HG_DH, HG_DH), HG_DH)
        qh, kk, lf = _hg_gates(q_ref[:, cols].astype(F32), f_ref[:, cols].astype(F32), lb_ref[:, cols])
        v = i_ref[:, cols]
        vf = v.astype(F32)
        hi, lo = _split2(lf)
        b = _dot(prefix, jnp.concatenate([hi, lo], axis=1))
        b = b[:, :HG_DH] + b[:, HG_DH:]
        o = jnp.zeros((rows, HG_DH), F32)
        for dist in range(steps):
            if dist == 0:
                k_s, b_s, v_s = kk, b, vf
            else:
                k_s, b_s, v_s = (pltpu.roll(kk, dist, 0), pltpu.roll(b, dist, 0), pltpu.roll(vf, dist, 0))
            w = jnp.sum(qh * k_s * jnp.exp(jnp.minimum(b - b_s, 0.0)), axis=1, keepdims=True)
            o = o + jnp.where(step >= dist, w, 0.0) * v_s
        qd = (qh * jnp.exp(b)).astype(BF16)
        for j in range(rows // steps):
            s = s_in_ref[j, h]
            o = o + jnp.where(row_batch == j, _dot(qd, s.astype(BF16)), 0.0)
            b_last = b[(j + 1) * steps - 1:(j + 1) * steps, :]
            col = jnp.sum(jnp.where(eye, jnp.exp(b_last), 0.0), axis=1, keepdims=True)
            kj = jnp.where(row_batch == j, kk * jnp.exp(jnp.minimum(b_last - b, 0.0)), 0.0).astype(BF16)
            s_out_ref[j, h] = col * s + _dot_tn(kj, v)
        o_ref[:, cols] = (_rms(o) * gn_ref[:, cols] * _silu(g_ref[:, cols].astype(F32))).astype(BF16)

    def group(gi, carry):
        for hh in range(HG_HEADS_PER_STEP):
            head(gi * HG_HEADS_PER_STEP + hh)
        return carry

    lax.fori_loop(0, heads // HG_HEADS_PER_STEP, group, 0)


def _hg_sample(proj, state, n_prompt, steps, heads, lb, out_norm):
    n_sample_batch = state.shape[0]
    d = heads * HG_DH
    per = SAMPLE_ROWS // steps
    base = n_prompt // SAMPLE_ROWS
    col = lambda part: (lambda b: (base + b, part))
    return pl.pallas_call(
        functools.partial(_hg_sample_kernel, steps, heads),
        grid=(n_sample_batch // per,),
        in_specs=[pl.BlockSpec((SAMPLE_ROWS, d), col(0)),
                  pl.BlockSpec((SAMPLE_ROWS, d), col(1)),
                  pl.BlockSpec((SAMPLE_ROWS, d), col(2)),
                  pl.BlockSpec((SAMPLE_ROWS, d), col(3)),
                  pl.BlockSpec((1, d), lambda b: (0, 0)),
                  pl.BlockSpec((1, d), lambda b: (0, 0)),
                  pl.BlockSpec((per, heads, HG_DH, HG_DH), lambda b: (b, 0, 0, 0))],
        out_specs=[pl.BlockSpec((SAMPLE_ROWS, d), lambda b: (b, 0)),
                   pl.BlockSpec((per, heads, HG_DH, HG_DH), lambda b: (b, 0, 0, 0))],
        out_shape=[jax.ShapeDtypeStruct((n_sample_batch * steps, d), BF16),
                   jax.ShapeDtypeStruct(state.shape, F32)],
        compiler_params=_params("arbitrary"),
        name="hg_sample",
    )(proj, proj, proj, proj, lb, out_norm, state)


def _first_max(vals, lane, width):
    m = jnp.max(vals, axis=1, keepdims=True)
    idx = jnp.min(jnp.where(vals == m, lane, width), axis=1, keepdims=True)
    return m, idx


def _route(logits, bias):
    neg = -jnp.inf
    lane = lax.broadcasted_iota(I32, logits.shape, 1)
    group = lane >> GROUP_SHIFT
    scores = jax.nn.sigmoid(logits)
    biased = scores + bias
    best = sel = None
    for gi in range(N_GROUPS):
        vals = jnp.where(group == gi, biased, neg)
        m1, i1 = _first_max(vals, lane, N_EXPERTS)
        m2 = jnp.max(jnp.where(lane == i1, neg, vals), axis=1, keepdims=True)
        total = m1 + m2
        if gi == 0:
            best, sel = total, jnp.zeros_like(i1)
        else:
            better = total > best
            sel = jnp.where(better, gi, sel)
            best = jnp.where(better, total, best)
    vals = jnp.where(group == sel, biased, neg)
    _, e1 = _first_max(vals, lane, N_EXPERTS)
    _, e2 = _first_max(jnp.where(lane == e1, neg, vals), lane, N_EXPERTS)
    w1 = jnp.sum(jnp.where(lane == e1, scores, 0.0), axis=1, keepdims=True)
    w2 = jnp.sum(jnp.where(lane == e2, scores, 0.0), axis=1, keepdims=True)
    tot = w1 + w2
    return e1, e2, w1 / tot, w2 / tot


def _mixout_kernel(prompt_tiles, o_p, o_s, w_ref, x_p, x_s, g1_ref, g2_ref, gm_p, gm_s, sh_p, sh_s, sc_p, sc_s,
                   rw_ref, rb_ref, xo_p, xo_s, h_ref, ridx_ref, rwt_ref, cnt_ref, carry_ref, y_ref):
    i = pl.program_id(0)
    tm = x_p.shape[0]

    @pl.when(i == 0)
    def _():
        carry_ref[...] = jnp.zeros_like(carry_ref)

    def tile(sample):
        o_ref, x_ref, xo_ref = (o_s, x_s, xo_s) if sample else (o_p, x_p, xo_p)
        y_ref[...] = _dot(o_ref[...], w_ref[...])

        def chunk(rows, grp):
            x = x_ref[rows, :] + _mod_rows(sample, gm_p, gm_s, rows) * (_rms(y_ref[rows, :]) * g1_ref[...])
            xo_ref[rows, :] = x
            h = (_rms(x) * g2_ref[...] * (1.0 + _mod_rows(sample, sc_p, sc_s, rows))
                 + _mod_rows(sample, sh_p, sh_s, rows))
            h_ref[grp] = h.reshape(ROW_CHUNK // SUBLANES, SUBLANES, h.shape[1])

        _row_chunks(tm, chunk)

    _by_group(i >= prompt_tiles, tile)

    h1, h2 = _split2(h_ref[...].reshape(tm, h_ref.shape[2]))
    w1, w2 = _split2(rw_ref[...])
    first = _dot(h1, jnp.concatenate([w1, w2], axis=1))
    logits = first[:, :N_EXPERTS] + (first[:, N_EXPERTS:] + _dot(h2, w1))
    e1, e2, p1, p2 = _route(logits, rb_ref[...])

    lane = lax.broadcasted_iota(I32, (tm, N_EXPERTS), 1)
    hot1, hot2 = lane == e1, lane == e2
    onehot = (hot1 | hot2).astype(BF16)
    ti = lax.broadcasted_iota(I32, (tm, tm), 0)
    si = lax.broadcasted_iota(I32, (tm, tm), 1)
    before = _dot((si < ti).astype(BF16), onehot) + carry_ref[...]
    r1 = jnp.sum(jnp.where(hot1, before, 0.0), axis=1, keepdims=True).astype(I32)
    r2 = jnp.sum(jnp.where(hot2, before, 0.0), axis=1, keepdims=True).astype(I32)
    carry = carry_ref[...] + jnp.sum(onehot.astype(F32), axis=0, keepdims=True)
    carry_ref[...] = carry

    wide = lax.broadcasted_iota(I32, (tm, 128), 1)
    ridx_ref[...] = jnp.where(wide == 0, e1, jnp.where(wide == 1, e2, jnp.where(wide == 2, r1, r2)))
    rwt_ref[...] = jnp.where(wide == 0, p1, p2)
    cnt_ref[...] = jnp.zeros_like(cnt_ref)
    cnt_ref[0:1, 0:N_EXPERTS] = carry


def _mixout(tok, o_prompt, o_sample, w_out, x_p, x_s, g1, g2, mod_p, mod_s, router_w, router_b):
    d = tok.d
    v = o_prompt.shape[1]
    return pl.pallas_call(
        functools.partial(_mixout_kernel, tok.prompt_tiles),
        grid=(tok.tiles,),
        in_specs=tok.split_specs(v)
                 + [pl.BlockSpec((v, d), lambda i: (0, 0), pipeline_mode=pl.Buffered(1))]
                 + tok.split_specs(d) + [_const_spec((1, d)), _const_spec((1, d))]
                 + tok.mod_specs(2) + tok.mod_specs(3) + tok.mod_specs(4)
                 + [_const_spec((d, N_EXPERTS)), _const_spec((1, N_EXPERTS))],
        out_specs=tok.split_specs(d, out=True) + [
            pl.BlockSpec((TOKEN_TILE // SUBLANES, SUBLANES, d), lambda i: (i, 0, 0)),
            tok.row_spec(128), tok.row_spec(128), _const_spec((8, 128))],
        out_shape=tok.split_shapes(d, F32) + [jax.ShapeDtypeStruct((tok.n // SUBLANES, SUBLANES, d), F32),
                                              jax.ShapeDtypeStruct((tok.n, 128), I32),
                                              jax.ShapeDtypeStruct((tok.n, 128), F32),
                                              jax.ShapeDtypeStruct((8, 128), F32)],
        scratch_shapes=[pltpu.VMEM((1, N_EXPERTS), F32), pltpu.VMEM((TOKEN_TILE, d), F32)],
        compiler_params=_params("arbitrary"),
        name="mix_out",
    )(o_prompt, o_sample, w_out, x_p, x_s, g1, g2, mod_p, mod_s, mod_p, mod_s, mod_p, mod_s, router_w, router_b)


def _expert_plan(ridx, counts, n_tiles):
    n = ridx.shape[0]
    expert = ridx[:, 0:2]
    rank = ridx[:, 2:4]
    cnt = counts[0, :N_EXPERTS].astype(I32)
    padded = ((cnt + EXPERT_TILE - 1) // EXPERT_TILE) * EXPERT_TILE
    ends = jnp.cumsum(padded)
    starts = ends - padded
    pos = starts[expert] + rank
    tile = jnp.arange(n_tiles, dtype=I32)
    tile_start = tile * EXPERT_TILE
    used = tile_start < ends[-1]
    last_used = jnp.maximum(ends[-1] // EXPERT_TILE - 1, 0)
    tile_expert = jnp.sum((tile_start[:, None] >= ends[None, :]).astype(I32), axis=1)
    tile_expert = jnp.minimum(jnp.where(used, tile_expert, tile_expert[last_used]), N_EXPERTS - 1)
    valid = jnp.clip(cnt[tile_expert] - (tile_start - starts[tile_expert]), 0, EXPERT_TILE)
    valid = jnp.where(used, valid, 0)
    pos = pos.reshape(n // TOKEN_TILE, TOKEN_TILE, 2).transpose(0, 2, 1)
    sub = jnp.arange(EXPERT_F_SPLIT, dtype=I32)[None, :]
    snake = jnp.where((tile[:, None] & 1) == 0, sub, EXPERT_F_SPLIT - 1 - sub)
    part = jnp.where(used[:, None], snake, snake[last_used, EXPERT_F_SPLIT - 1]).reshape(-1)
    return tile_expert, valid, jnp.minimum(tile, last_used), part, pos


def _start_rows(groups, copy):
    def body(g, carry):
        for j in range(SUBLANES):
            copy(g, j).start()
        return carry
    lax.fori_loop(0, groups, body, 0)


def _dispatch_kernel(pos_ref, h_ref, xs_in, xs_hbm, sem):
    del xs_in
    groups = h_ref.shape[0]
    for choice in range(2):
        def row(g, j, choice=choice):
            p = pos_ref[0, choice, g * SUBLANES + j]
            return pltpu.make_async_copy(h_ref.at[g, pl.ds(j, 1)],
                                         xs_hbm.at[p >> 3, pl.ds(p & (SUBLANES - 1), 1)], sem.at[choice])
        _start_rows(groups, row)
    for choice in range(2):
        pltpu.make_async_copy(h_ref, xs_hbm.at[pl.ds(0, groups)], sem.at[choice]).wait()


def _dispatch(h, pos, n_rows, buf=None):
    d = h.shape[2]
    n = h.shape[0] * SUBLANES
    if buf is None:
        buf = jnp.zeros((n_rows // SUBLANES, SUBLANES, d), F32)
    tile_groups = TOKEN_TILE // SUBLANES
    return pl.pallas_call(
        _dispatch_kernel,
        grid=(n // TOKEN_TILE,),
        in_specs=[pl.BlockSpec((1, 2, TOKEN_TILE), lambda i: (i, 0, 0), memory_space=pltpu.SMEM),
                  pl.BlockSpec((tile_groups, SUBLANES, d), lambda i: (i, 0, 0)),
                  pl.BlockSpec(memory_space=pl.ANY)],
        out_specs=pl.BlockSpec(memory_space=pl.ANY),
        out_shape=jax.ShapeDtypeStruct((n_rows // SUBLANES, SUBLANES, d), F32),
        scratch_shapes=[pltpu.SemaphoreType.DMA((2,))],
        input_output_aliases={2: 0},
        compiler_params=_params("arbitrary"),
        name="dispatch",
    )(pos, h, buf)


def _expert_kernel(te_ref, nv_ref, tin_ref, part_ref, x_ref, wg_ref, wu_ref, wd_ref, y_ref):
    del te_ref, tin_ref, part_ref
    n = nv_ref[pl.program_id(0)]
    step = pl.program_id(1)
    rows = x_ref.shape[0] * SUBLANES

    @pl.when(n > 0)
    def _():
        x = x_ref[...].reshape(rows, x_ref.shape[2]).astype(BF16)
        a = _dot(x, wg_ref[...].astype(BF16))
        u = _dot(x, wu_ref[...].astype(BF16))
        y = _dot((_silu(a) * u).astype(BF16), wd_ref[...].astype(BF16)).reshape(y_ref.shape)

        @pl.when(step == 0)
        def _():
            y_ref[...] = y

        @pl.when(step > 0)
        def _():
            y_ref[...] += y

    @pl.when((n == 0) & (step == 0))
    def _():
        y_ref[...] = jnp.zeros_like(y_ref)


def _experts(xs, plan, layer, wg, wu, wd):
    d, f = wg.shape[2], wg.shape[3]
    fs = f // EXPERT_F_SPLIT
    tile_expert, valid, tile_in, part = plan[:4]
    n_tiles = tile_expert.shape[0]
    tile_groups = EXPERT_TILE // SUBLANES
    which = lambda i, s, part: part[i * EXPERT_F_SPLIT + s]
    grid_spec = pltpu.PrefetchScalarGridSpec(
        num_scalar_prefetch=4,
        grid=(n_tiles, EXPERT_F_SPLIT),
        in_specs=[pl.BlockSpec((tile_groups, SUBLANES, d), lambda i, s, te, nv, tin, part: (tin[i], 0, 0)),
                  pl.BlockSpec((None, None, d, fs),
                               lambda i, s, te, nv, tin, part: (layer, te[i], 0, which(i, s, part))),
                  pl.BlockSpec((None, None, d, fs),
                               lambda i, s, te, nv, tin, part: (layer, te[i], 0, which(i, s, part))),
                  pl.BlockSpec((None, None, fs, d),
                               lambda i, s, te, nv, tin, part: (layer, te[i], which(i, s, part), 0))],
        out_specs=pl.BlockSpec((tile_groups, SUBLANES, d), lambda i, s, te, nv, tin, part: (i, 0, 0)),
    )
    return pl.pallas_call(
        _expert_kernel,
        grid_spec=grid_spec,
        out_shape=jax.ShapeDtypeStruct((n_tiles * tile_groups, SUBLANES, d), F32),
        compiler_params=_params("arbitrary", "arbitrary"),
        name="experts",
    )(tile_expert, valid, tile_in, part, xs, wg, wu, wd)


def _by_group(is_sample, body):
    @pl.when(is_sample)
    def _():
        body(True)

    @pl.when(jnp.logical_not(is_sample))
    def _():
        body(False)


def _row_chunks(n_rows, fn):
    per = ROW_CHUNK // SUBLANES

    def body(ci, carry):
        fn(pl.ds(pl.multiple_of(ci * ROW_CHUNK, ROW_CHUNK), ROW_CHUNK), pl.ds(pl.multiple_of(ci * per, per), per))
        return carry
    lax.fori_loop(0, n_rows // ROW_CHUNK, body, 0, unroll=ROW_CHUNK_UNROLL)


def _mod_rows(sample, p_ref, s_ref, rows):
    return s_ref[rows, :] if sample else p_ref[...]


def _moeout_kernel(prompt_tiles, with_next, pos_ref, pos_next_ref, ys_hbm, x_p, x_s, rwt_ref, g3_ref, gf_p, gf_s,
                   *rest):
    if with_next:
        gn_ref, sh_p, sh_s, sc_p, sc_s, xo_p, xo_s, h_ref, ybuf, sem = rest
    else:
        xo_p, xo_s, ybuf, sem = rest
    groups = ybuf.shape[2]
    n_rows = groups * SUBLANES
    d = ybuf.shape[4]
    i = pl.program_id(0)
    slot = i & 1

    def fetch(p_ref, to_slot):
        for choice in range(2):
            def row(g, j, choice=choice):
                p = p_ref[0, choice, g * SUBLANES + j]
                return pltpu.make_async_copy(ys_hbm.at[p >> 3, pl.ds(p & (SUBLANES - 1), 1)],
                                             ybuf.at[to_slot, choice, g, pl.ds(j, 1)], sem.at[to_slot, choice])
            _start_rows(groups, row)

    @pl.when(i == 0)
    def _():
        fetch(pos_ref, 0)

    @pl.when(i + 1 < pl.num_programs(0))
    def _():
        fetch(pos_next_ref, 1 - slot)

    for choice in range(2):
        pltpu.make_async_copy(ys_hbm.at[pl.ds(0, groups)], ybuf.at[slot, choice], sem.at[slot, choice]).wait()

    def tile(sample):
        x_ref, xo_ref = (x_s, xo_s) if sample else (x_p, xo_p)

        def chunk(rows, grp):
            rw = rwt_ref[rows, :]
            y = (rw[:, 0:1] * ybuf[slot, 0, grp].reshape(ROW_CHUNK, d)
                 + rw[:, 1:2] * ybuf[slot, 1, grp].reshape(ROW_CHUNK, d))
            x = x_ref[rows, :] + _mod_rows(sample, gf_p, gf_s, rows) * (_rms(y) * g3_ref[...])
            xo_ref[rows, :] = x
            if with_next:
                h = (_rms(x) * gn_ref[...] * (1.0 + _mod_rows(sample, sc_p, sc_s, rows))
                     + _mod_rows(sample, sh_p, sh_s, rows))
                h_ref[rows, :] = h.astype(BF16)

        _row_chunks(n_rows, chunk)

    _by_group(pl.program_id(0) >= prompt_tiles, tile)


def _moeout(tok, pos, ys, x_p, x_s, rwt, g3, mod_p, mod_s, nxt=None):
    d = tok.d
    last = tok.tiles - 1
    ins = ([pl.BlockSpec((1, 2, TOKEN_TILE), lambda i: (i, 0, 0), memory_space=pltpu.SMEM),
            pl.BlockSpec((1, 2, TOKEN_TILE), lambda i: (jnp.minimum(i + 1, last), 0, 0), memory_space=pltpu.SMEM),
            pl.BlockSpec(memory_space=pl.ANY)]
           + tok.split_specs(d) + [tok.row_spec(128), _const_spec((1, d))] + tok.mod_specs(5))
    args = [pos, pos, ys, x_p, x_s, rwt, g3, mod_p, mod_s]
    scratch = [pltpu.VMEM((2, 2, TOKEN_TILE // SUBLANES, SUBLANES, d), F32), pltpu.SemaphoreType.DMA((2, 2))]
    if nxt is None:
        return pl.pallas_call(
            functools.partial(_moeout_kernel, tok.prompt_tiles, False),
            grid=(tok.tiles,), in_specs=ins, out_specs=tok.split_specs(d, out=True),
            out_shape=tok.split_shapes(d, F32), scratch_shapes=scratch,
            compiler_params=_params("arbitrary"), name="moe_out_last",
        )(*args)
    gain_n, mod_pn, mod_sn = nxt
    ins = ins + [_const_spec((1, d))] + tok.mod_specs(0) + tok.mod_specs(1)
    args = args + [gain_n, mod_pn, mod_sn, mod_pn, mod_sn]
    return pl.pallas_call(
        functools.partial(_moeout_kernel, tok.prompt_tiles, True),
        grid=(tok.tiles,), in_specs=ins, out_specs=tok.split_specs(d, out=True) + [tok.row_spec(d)],
        out_shape=tok.split_shapes(d, F32) + [jax.ShapeDtypeStruct((tok.n, d), BF16)],
        scratch_shapes=scratch,
        compiler_params=_params("arbitrary"), name="moe_out_next",
    )(*args)


def kernel(x_prompt, x_sample, c_prompt, c_sample, state_ret, state_hgrn, ada_w, ada_b, norm_gains,
           ret_w_in, ret_w_out, hg_w_in, hg_w_out, hg_lower_bound, hg_out_norm, router_w, router_b,
           exp_w_gate, exp_w_up, exp_w_down):
    n_batch, seq, d = x_prompt.shape
    n_dec, steps, _ = x_sample.shape
    depth = ada_w.shape[0]
    assert depth == 2 and d % RET_DK == 0 and d % HG_DH == 0
    assert seq % RET_CHUNK == 0 and (n_dec * steps) % TOKEN_TILE == 0 and SAMPLE_ROWS % steps == 0
    ret_heads = d // RET_DK
    hg_heads = d // HG_DH
    n_prompt = n_batch * seq
    n_sample = n_dec * steps
    tok = _Tokens(n_batch, seq, n_sample, d)
    n = tok.n

    n_cond = n_batch + n_dec
    pad = (-n_cond) % 8
    c_all = jnp.concatenate([c_prompt, c_sample, jnp.zeros((pad, d), F32)], axis=0)
    mod = _ada(c_all, ada_w, ada_b)
    mods = []
    for l in range(depth):
        mod_p = mod[l, :n_batch].reshape(n_batch, 6, 1, d)
        mod_s = jnp.repeat(mod[l, n_batch:n_cond], steps, axis=0)
        mods.append((mod_p, mod_s))
    gain = lambda l, k: norm_gains[l, k].reshape(1, d)

    x_p = x_prompt.reshape(n_prompt, d)
    x_s = x_sample.reshape(n_sample, d)
    n_tiles = (2 * n + N_EXPERTS * (EXPERT_TILE - 1)) // EXPERT_TILE + 1
    rw = router_w.astype(F32)
    rb = router_b.astype(F32).reshape(1, N_EXPERTS)

    def channel_mixer(l, x_p, x_s, h, ridx, rwt, counts, nxt, buf):
        plan = _expert_plan(ridx, counts, n_tiles)
        pos = plan[4]
        xs = _dispatch(h, pos, n_tiles * EXPERT_TILE, buf)
        ys = _experts(xs, plan, l, exp_w_gate, exp_w_up, exp_w_down)
        return _moeout(tok, pos, ys, x_p, x_s, rwt, gain(l, 3), *mods[l], nxt=nxt), xs

    h = _prenorm(tok, x_p, x_s, gain(0, 0), *mods[0])
    proj = _proj(h, ret_w_in[0])
    log_gamma = jnp.log(1.0 - jnp.exp2(-5.0 - jnp.arange(ret_heads, dtype=F32)))
    log_gamma = jnp.broadcast_to(log_gamma[:, None, None], (ret_heads, 1, 128))
    cos_p, sin_p = _rope_tables(jnp.arange(seq))
    cos_s, sin_s = _rope_tables(PAST_LEN + jnp.arange(steps))
    reps = SAMPLE_ROWS // steps
    cos_s, sin_s = jnp.tile(cos_s, (reps, 1)), jnp.tile(sin_s, (reps, 1))
    o_p, ret_prompt = _ret_prompt(proj, n_batch, seq, ret_heads, cos_p, sin_p, log_gamma)
    o_s, ret_sample = _ret_sample(proj, state_ret[0], n_prompt, steps, ret_heads, cos_s, sin_s, log_gamma)
    x_p, x_s, hp, ridx, rwt, counts = _mixout(tok, o_p, o_s, ret_w_out[0].astype(BF16), x_p, x_s,
                                              gain(0, 1), gain(0, 2), *mods[0], rw, rb)
    (x_p, x_s, h), row_buf = channel_mixer(0, x_p, x_s, hp, ridx, rwt, counts, (gain(1, 0),) + mods[1], None)

    sm = jax.nn.softmax(hg_lower_bound.astype(F32), axis=0)
    lb = (jnp.cumsum(sm, axis=0) - sm[0])[1].reshape(1, d)
    proj = _proj(h, hg_w_in[0])
    out_norm = hg_out_norm[0].reshape(1, d)
    leaf_min = _hg_guard(proj, n_prompt, hg_heads, lb)[0, 0]
    o_p, hg_prompt = lax.cond(
        leaf_min > HG_LEAF_LOG_LIMIT,
        functools.partial(_hg_prompt, False, n_batch=n_batch, seq=seq, heads=hg_heads),
        functools.partial(_hg_prompt, True, n_batch=n_batch, seq=seq, heads=hg_heads),
        proj, lb, out_norm)
    o_s, hg_sample = _hg_sample(proj, state_hgrn[0], n_prompt, steps, hg_heads, lb, out_norm)
    x_p, x_s, hp, ridx, rwt, counts = _mixout(tok, o_p, o_s, hg_w_out[0].astype(BF16), x_p, x_s,
                                              gain(1, 1), gain(1, 2), *mods[1], rw, rb)
    (x_p, x_s), _ = channel_mixer(1, x_p, x_s, hp, ridx, rwt, counts, None, row_buf)

    return (x_p.reshape(n_batch, seq, d), x_s.reshape(n_dec, steps, d),
            ret_prompt[None], hg_prompt[None], ret_sample[None], hg_sample[None])
```

```python
import functools
import math

import numpy as np
import jax
import jax.numpy as jnp
from jax import lax
from jax.experimental import pallas as pl
from jax.experimental.pallas import tpu as pltpu

F32, BF16, I32 = jnp.float32, jnp.bfloat16, jnp.int32

EPS = 1e-6
ROPE_BASE = 10000.0
PAST_LEN = 16384
RET_DK = 256
RET_DV = 512
HG_DH = 128
N_EXPERTS = 16
GROUP_SHIFT = 2
N_GROUPS = N_EXPERTS >> GROUP_SHIFT

TOKEN_TILE = 256
PROJ_TILE = 512
PROJ_COLS = 2048
EXPERT_TILE = 384
RET_CHUNK = 256
HG_CHUNK = 128
HG_LEAF = 16
HG_LEVELS = (64, 32, 16)
HG_HEADS_PER_STEP = 4
HG_LEAF_LOG_LIMIT = -80.0
SAMPLE_ROWS = 16
SUBLANES = 8
EXPERT_F_SPLIT = 2
RET_SAMPLE_HEADS = 4
ROW_CHUNK = 16
ROW_CHUNK_UNROLL = 4
VMEM_LIMIT = 56 * 1024 * 1024


def _params(*sem):
    return pltpu.CompilerParams(dimension_semantics=sem, vmem_limit_bytes=VMEM_LIMIT)


def _rms(x):
    return x * lax.rsqrt(jnp.mean(x * x, axis=-1, keepdims=True) + EPS)


def _silu(x):
    return x * jax.nn.sigmoid(x)


def _dot(a, b):
    return jnp.dot(a, b, preferred_element_type=F32)


def _dot_nt(a, b):
    return lax.dot_general(a, b, (((1,), (1,)), ((), ())), preferred_element_type=F32)


def _dot_tn(a, b):
    return lax.dot_general(a, b, (((0,), (0,)), ((), ())), preferred_element_type=F32)


def _split2(x):
    hi = x.astype(BF16)
    lo = (x - hi.astype(F32)).astype(BF16)
    return hi, lo


def _split3(x):
    hi = x.astype(BF16)
    r = x - hi.astype(F32)
    mid = r.astype(BF16)
    lo = (r - mid.astype(F32)).astype(BF16)
    return hi, mid, lo


def _ada_kernel(c_ref, w_ref, b_ref, o_ref):
    s = _silu(c_ref[...]).astype(BF16)
    o_ref[...] = _dot(s, w_ref[...].astype(BF16)) + b_ref[...]


def _ada(c_all, ada_w, ada_b):
    depth, d, d6 = ada_w.shape
    m = c_all.shape[0]
    tn = min(1024, d6)
    return pl.pallas_call(
        _ada_kernel,
        grid=(depth, d6 // tn),
        in_specs=[pl.BlockSpec((m, d), lambda l, j: (0, 0)),
                  pl.BlockSpec((None, d, tn), lambda l, j: (l, 0, j)),
                  pl.BlockSpec((None, 1, tn), lambda l, j: (l, 0, j))],
        out_specs=pl.BlockSpec((None, m, tn), lambda l, j: (l, 0, j)),
        out_shape=jax.ShapeDtypeStruct((depth, m, d6), F32),
        compiler_params=_params("arbitrary", "arbitrary"),
        name="ada_mod",
    )(c_all, ada_w, ada_b.reshape(depth, 1, d6))


class _Tokens:
    def __init__(self, n_prompt_batch, seq, n_sample, d):
        self.d = d
        self.n_prompt = n_prompt_batch * seq
        self.n = self.n_prompt + n_sample
        self.prompt_tiles = self.n_prompt // TOKEN_TILE
        self.tiles_per_batch = seq // TOKEN_TILE
        self.n_batch = n_prompt_batch
        self.tiles = self.n // TOKEN_TILE

    def mod_specs(self, comp):
        d, tpb, nb, npt = self.d, self.tiles_per_batch, self.n_batch, self.prompt_tiles
        return [pl.BlockSpec((None, None, 1, d), lambda i: (jnp.minimum(i // tpb, nb - 1), comp, 0, 0)),
                pl.BlockSpec((TOKEN_TILE, d), lambda i: (jnp.maximum(i - npt, 0), comp),
                             pipeline_mode=pl.Buffered(1))]

    def row_spec(self, width):
        return pl.BlockSpec((TOKEN_TILE, width), lambda i: (i, 0))

    def split_specs(self, width, out=False):
        npt = self.prompt_tiles
        mode = {} if out else dict(pipeline_mode=pl.Buffered(1))
        return [pl.BlockSpec((TOKEN_TILE, width), lambda i: (jnp.minimum(i, npt - 1), 0)),
                pl.BlockSpec((TOKEN_TILE, width), lambda i: (jnp.maximum(i - npt, 0), 0), **mode)]

    def split_shapes(self, width, dtype):
        return [jax.ShapeDtypeStruct((self.n_prompt, width), dtype),
                jax.ShapeDtypeStruct((self.n - self.n_prompt, width), dtype)]


def _const_spec(shape):
    return pl.BlockSpec(shape, lambda i: (0,) * len(shape))


def _pick(is_sample, p_ref, s_ref):
    return jnp.where(is_sample, s_ref[...], p_ref[...])


def _store_split(is_sample, p_ref, s_ref, val):
    @pl.when(is_sample)
    def _():
        s_ref[...] = val

    @pl.when(jnp.logical_not(is_sample))
    def _():
        p_ref[...] = val


def _prenorm_kernel(prompt_tiles, x_p, x_s, g_ref, sh_p, sh_s, sc_p, sc_s, h_ref):
    def tile(sample):
        x_ref = x_s if sample else x_p

        def chunk(rows, _):
            h = (_rms(x_ref[rows, :]) * g_ref[...] * (1.0 + _mod_rows(sample, sc_p, sc_s, rows))
                 + _mod_rows(sample, sh_p, sh_s, rows))
            h_ref[rows, :] = h.astype(BF16)

        _row_chunks(x_p.shape[0], chunk)

    _by_group(pl.program_id(0) >= prompt_tiles, tile)


def _prenorm(tok, x_p, x_s, gain, mod_p, mod_s):
    d = tok.d
    return pl.pallas_call(
        functools.partial(_prenorm_kernel, tok.prompt_tiles),
        grid=(tok.tiles,),
        in_specs=tok.split_specs(d) + [_const_spec((1, d))] + tok.mod_specs(0) + tok.mod_specs(1),
        out_specs=tok.row_spec(d),
        out_shape=jax.ShapeDtypeStruct((tok.n, d), BF16),
        compiler_params=_params("arbitrary"),
        name="prenorm",
    )(x_p, x_s, gain, mod_p, mod_s, mod_p, mod_s)


def _proj_kernel(h_ref, w_ref, o_ref, wb_ref):
    @pl.when(pl.program_id(1) == 0)
    def _():
        wb_ref[...] = w_ref[...].astype(BF16)

    o_ref[...] = _dot(h_ref[...], wb_ref[...]).astype(BF16)


def _proj(h, w):
    n, d = h.shape
    p = w.shape[1]
    tm = PROJ_TILE if n % PROJ_TILE == 0 else TOKEN_TILE
    tn = PROJ_COLS if p % PROJ_COLS == 0 else PROJ_COLS // 2
    assert n % tm == 0 and p % tn == 0
    return pl.pallas_call(
        _proj_kernel,
        grid=(p // tn, n // tm),
        in_specs=[pl.BlockSpec((tm, d), lambda j, i: (i, 0)),
                  pl.BlockSpec((d, tn), lambda j, i: (0, j))],
        out_specs=pl.BlockSpec((tm, tn), lambda j, i: (i, j)),
        out_shape=jax.ShapeDtypeStruct((n, p), BF16),
        scratch_shapes=[pltpu.VMEM((d, tn), BF16)],
        compiler_params=_params("arbitrary", "arbitrary"),
        name="in_proj",
    )(h, w)


def _rope_tables(pos):
    half = RET_DK // 2
    theta = 1.0 / (ROPE_BASE ** jnp.linspace(0.0, 1.0, half, dtype=F32))
    ang = pos.astype(F32)[:, None] * theta[None, :]
    cos, sin = jnp.cos(ang), jnp.sin(ang)
    return (jnp.repeat(cos, 2, axis=1),
            jnp.stack([-sin, sin], axis=-1).reshape(pos.shape[0], RET_DK))


def _rot(x, cos, sin_signed):
    lane = lax.broadcasted_iota(I32, x.shape, 1)
    width = x.shape[1]
    nbr = jnp.where((lane & 1) == 0, pltpu.roll(x, width - 1, 1), pltpu.roll(x, 1, 1))
    return x * cos + nbr * sin_signed


def _gate_out(o, g_ref_val):
    return (_rms(o) * _silu(g_ref_val.astype(F32))).astype(BF16)


def _ret_prompt_kernel(q_ref, k_ref, v_ref, g_ref, cos_ref, sin_ref, lg_ref, o_ref, s_out_ref, s_ref):
    c = RET_CHUNK
    seq = q_ref.shape[0]
    lg = lg_ref[0:1, 0:1]
    ti = lax.broadcasted_iota(I32, (c, c), 0)
    si = lax.broadcasted_iota(I32, (c, c), 1)
    decay = jnp.where(ti >= si, jnp.exp((ti - si).astype(F32) * lg), 0.0)
    tcol = lax.broadcasted_iota(I32, (c, 1), 0).astype(F32)
    dq = jnp.exp((tcol + 1.0) * lg)
    dk = jnp.exp((float(c - 1) - tcol) * lg)
    dchunk = jnp.exp(float(c) * lg)
    s_ref[...] = jnp.zeros_like(s_ref)

    def body(ci, carry):
        r0 = pl.multiple_of(ci * c, c)
        rows = pl.ds(r0, c)
        cos, sin = cos_ref[rows, :], sin_ref[rows, :]
        q = _rot(q_ref[rows, :].astype(F32), cos, sin)
        k = _rot(k_ref[rows, :].astype(F32), cos, sin) * (RET_DK ** -0.5)
        v = v_ref[rows, :]
        s = s_ref[...]
        scores = _dot_nt(q.astype(BF16), k.astype(BF16)) * decay
        o = _dot(scores.astype(BF16), v) + _dot((q * dq).astype(BF16), s.astype(BF16))
        s_ref[...] = dchunk * s + _dot_tn((k * dk).astype(BF16), v)
        o_ref[rows, :] = _gate_out(o, g_ref[rows, :])
        return carry

    lax.fori_loop(0, seq // c, body, 0)
    s_out_ref[...] = s_ref[...]


def _ret_prompt(proj, n_batch, seq, heads, cos, sin, log_gamma):
    qk_blocks = heads
    return pl.pallas_call(
        _ret_prompt_kernel,
        grid=(n_batch, heads),
        in_specs=[pl.BlockSpec((seq, RET_DK), lambda b, h: (b, h)),
                  pl.BlockSpec((seq, RET_DK), lambda b, h: (b, qk_blocks + h)),
                  pl.BlockSpec((seq, RET_DV), lambda b, h: (b, heads + h)),
                  pl.BlockSpec((seq, RET_DV), lambda b, h: (b, 2 * heads + h)),
                  pl.BlockSpec((seq, RET_DK), lambda b, h: (0, 0)),
                  pl.BlockSpec((seq, RET_DK), lambda b, h: (0, 0)),
                  pl.BlockSpec((None, 1, 128), lambda b, h: (h, 0, 0))],
        out_specs=[pl.BlockSpec((seq, RET_DV), lambda b, h: (b, h)),
                   pl.BlockSpec((None, None, RET_DK, RET_DV), lambda b, h: (b, h, 0, 0))],
        out_shape=[jax.ShapeDtypeStruct((n_batch * seq, heads * RET_DV), BF16),
                   jax.ShapeDtypeStruct((n_batch, heads, RET_DK, RET_DV), F32)],
        scratch_shapes=[pltpu.VMEM((RET_DK, RET_DV), F32)],
        compiler_params=_params("arbitrary", "arbitrary"),
        name="ret_prompt",
    )(proj, proj, proj, proj, cos, sin, log_gamma)


def _ret_sample_kernel(steps, q_ref, k_ref, v_ref, g_ref, cos_ref, sin_ref, lg_ref, s_in_ref, o_ref, s_out_ref):
    rows = q_ref.shape[0]
    shift = int(math.log2(steps))
    cos, sin = cos_ref[...], sin_ref[...]
    ri = lax.broadcasted_iota(I32, (rows, rows), 0)
    ci = lax.broadcasted_iota(I32, (rows, rows), 1)
    pair = ((ri >> shift) == (ci >> shift)) & (ri >= ci)
    rid = lax.broadcasted_iota(I32, (rows, 1), 0)
    step = (rid & (steps - 1)).astype(F32)
    row_batch = rid >> shift
    for hh in range(q_ref.shape[1] // RET_DK):
        kc = slice(hh * RET_DK, (hh + 1) * RET_DK)
        vc = slice(hh * RET_DV, (hh + 1) * RET_DV)
        lg = lg_ref[hh, 0:1, 0:1]
        q = _rot(q_ref[:, kc].astype(F32), cos, sin)
        k = _rot(k_ref[:, kc].astype(F32), cos, sin) * (RET_DK ** -0.5)
        v = v_ref[:, vc]
        decay = jnp.where(pair, jnp.exp((ri - ci).astype(F32) * lg), 0.0)
        o = _dot((_dot_nt(q.astype(BF16), k.astype(BF16)) * decay).astype(BF16), v)
        qd = (q * jnp.exp((step + 1.0) * lg)).astype(BF16)
        kd = k * jnp.exp((float(steps - 1) - step) * lg)
        dall = jnp.exp(float(steps) * lg)
        for j in range(rows // steps):
            s = s_in_ref[j, hh]
            o = o + jnp.where(row_batch == j, _dot(qd, s.astype(BF16)), 0.0)
            kj = jnp.where(row_batch == j, kd, 0.0).astype(BF16)
            s_out_ref[j, hh] = dall * s + _dot_tn(kj, v)
        o_ref[:, vc] = _gate_out(o, g_ref[:, vc])


def _ret_sample(proj, state, n_prompt, steps, heads, cos, sin, log_gamma):
    n_sample_batch = state.shape[0]
    per = SAMPLE_ROWS // steps
    base = n_prompt // SAMPLE_ROWS
    hs = min(RET_SAMPLE_HEADS, heads)
    groups = heads // hs
    kw, vw = hs * RET_DK, hs * RET_DV
    return pl.pallas_call(
        functools.partial(_ret_sample_kernel, steps),
        grid=(n_sample_batch // per, groups),
        in_specs=[pl.BlockSpec((SAMPLE_ROWS, kw), lambda b, h: (base + b, h)),
                  pl.BlockSpec((SAMPLE_ROWS, kw), lambda b, h: (base + b, groups + h)),
                  pl.BlockSpec((SAMPLE_ROWS, vw), lambda b, h: (base + b, groups + h)),
                  pl.BlockSpec((SAMPLE_ROWS, vw), lambda b, h: (base + b, 2 * groups + h)),
                  pl.BlockSpec((SAMPLE_ROWS, RET_DK), lambda b, h: (0, 0)),
                  pl.BlockSpec((SAMPLE_ROWS, RET_DK), lambda b, h: (0, 0)),
                  pl.BlockSpec((hs, 1, 128), lambda b, h: (h, 0, 0)),
                  pl.BlockSpec((per, hs, RET_DK, RET_DV), lambda b, h: (b, h, 0, 0))],
        out_specs=[pl.BlockSpec((SAMPLE_ROWS, vw), lambda b, h: (b, h)),
                   pl.BlockSpec((per, hs, RET_DK, RET_DV), lambda b, h: (b, h, 0, 0))],
        out_shape=[jax.ShapeDtypeStruct((n_sample_batch * steps, heads * RET_DV), BF16),
                   jax.ShapeDtypeStruct(state.shape, F32)],
        compiler_params=_params("arbitrary", "arbitrary"),
        name="ret_sample",
    )(proj, proj, proj, proj, cos, sin, log_gamma, state)


def _hg_prefix_matrix():
    c = HG_CHUNK
    t = np.arange(c)[:, None]
    s = np.arange(c)[None, :]
    le = (s <= t).astype(np.float32)
    mats = [le, le * ((s // HG_LEAF) == (t // HG_LEAF))]
    for half in HG_LEVELS:
        mid = (t // (2 * half)) * (2 * half) + half - 1
        mats.append(le - (s <= mid).astype(np.float32))
    return jnp.asarray(np.concatenate(mats, axis=0), dtype=BF16)


def _hg_gates(q, f, lb):
    forget = lb + (1.0 - lb) * jax.nn.sigmoid(f)
    return _silu(q), 1.0 - forget, jnp.log(forget)


def _hg_guard_kernel(f_ref, lb_ref, o_ref):
    @pl.when(pl.program_id(0) == 0)
    def _():
        o_ref[...] = jnp.zeros_like(o_ref)

    rows, width = f_ref.shape
    lb = lb_ref[...]

    def leaf(li, lowest):
        r = pl.ds(pl.multiple_of(li * HG_LEAF, HG_LEAF), HG_LEAF)
        lf = jnp.log(lb + (1.0 - lb) * jax.nn.sigmoid(f_ref[r, :].astype(F32)))
        return jnp.minimum(lowest, jnp.sum(lf, axis=0, keepdims=True))

    lowest = lax.fori_loop(0, rows // HG_LEAF, leaf, jnp.zeros((1, width), F32), unroll=ROW_CHUNK_UNROLL)
    o_ref[...] = jnp.minimum(o_ref[...], jnp.min(lowest))


def _hg_guard(proj, n_prompt, heads, lb):
    d = heads * HG_DH
    tile = PROJ_TILE
    return pl.pallas_call(
        _hg_guard_kernel,
        grid=(n_prompt // tile,),
        in_specs=[pl.BlockSpec((tile, d), lambda i: (i, 1)),
                  pl.BlockSpec((1, d), lambda i: (0, 0))],
        out_specs=pl.BlockSpec((8, 128), lambda i: (0, 0)),
        out_shape=jax.ShapeDtypeStruct((8, 128), F32),
        compiler_params=_params("arbitrary"),
        name="hg_guard",
    )(proj, lb)


def _hg_prompt_kernel(exact_leaf, q_ref, f_ref, i_ref, g_ref, lb_ref, gn_ref, pm_ref, o_ref, s_out_ref, s_ref):
    c = HG_CHUNK
    seq = q_ref.shape[0]
    n_heads = q_ref.shape[1] // HG_DH
    ti = lax.broadcasted_iota(I32, (c, c), 0)
    si = lax.broadcasted_iota(I32, (c, c), 1)
    leaf_shift = int(math.log2(HG_LEAF))
    mask_leaf = (si <= ti) & ((ti >> leaf_shift) == (si >> leaf_shift))
    level_masks = []
    for half in HG_LEVELS:
        sh = int(math.log2(2 * half))
        level_masks.append(((ti >> sh) == (si >> sh)) & ((ti & (2 * half - 1)) >= half) & ((si & (2 * half - 1)) < half))
    eye = ti == si
    leaf_pos = lax.broadcasted_iota(I32, (c, 1), 0) & (HG_LEAF - 1)
    s_ref[...] = jnp.zeros_like(s_ref)

    width = n_heads * HG_DH
    head_cols = [slice(hh * HG_DH, (hh + 1) * HG_DH) for hh in range(n_heads)]

    def body(ci, carry):
        rows = pl.ds(pl.multiple_of(ci * c, c), c)
        qh, kk, lf = _hg_gates(q_ref[rows, :].astype(F32), f_ref[rows, :].astype(F32), lb_ref[...])
        sums = _dot(pm_ref[...], jnp.concatenate(_split2(lf), axis=1))
        sums = sums[:, :width] + sums[:, width:]
        b = sums[0:c]
        if exact_leaf:
            a = [jnp.zeros((c, c), F32) for _ in head_cols]
            vf = i_ref[rows, :].astype(F32)
            o_leaf = [jnp.zeros((c, HG_DH), F32) for _ in head_cols]
            for dist in range(HG_LEAF):
                if dist == 0:
                    k_s, b_s, v_s = kk, b, vf
                else:
                    k_s, b_s, v_s = pltpu.roll(kk, dist, 0), pltpu.roll(b, dist, 0), pltpu.roll(vf, dist, 0)
                pair = qh * k_s * jnp.exp(jnp.minimum(b - b_s, 0.0))
                for hh, hc in enumerate(head_cols):
                    w = jnp.sum(pair[:, hc], axis=1, keepdims=True)
                    o_leaf[hh] = o_leaf[hh] + jnp.where(leaf_pos >= dist, w, 0.0) * v_s[:, hc]
        else:
            d_leaf = sums[c:2 * c]
            q_f = (qh * jnp.exp(d_leaf)).astype(BF16)
            k_f = (kk * jnp.exp(-d_leaf)).astype(BF16)
            a = [jnp.where(mask_leaf, _dot_nt(q_f[:, hc], k_f[:, hc]), 0.0) for hc in head_cols]
            o_leaf = [0.0 for _ in head_cols]
        for lvl, mask in enumerate(level_masks):
            w = jnp.exp(-jnp.abs(sums[(2 + lvl) * c:(3 + lvl) * c]))
            q_f = (qh * w).astype(BF16)
            k_f = (kk * w).astype(BF16)
            a = [a_h + jnp.where(mask, _dot_nt(q_f[:, hc], k_f[:, hc]), 0.0) for a_h, hc in zip(a, head_cols)]
        b_last = b[c - 1:c, :]
        q_b = (qh * jnp.exp(b)).astype(BF16)
        k_b = (kk * jnp.exp(b_last - b)).astype(BF16)
        e_last = jnp.exp(b_last)
        gate = gn_ref[...] * _silu(g_ref[rows, :].astype(F32))
        for hh, hc in enumerate(head_cols):
            v = i_ref[rows, hc]
            s = s_ref[hh]
            o = _dot(a[hh].astype(BF16), v) + _dot(q_b[:, hc], s.astype(BF16)) + o_leaf[hh]
            col = jnp.sum(jnp.where(eye, e_last[:, hc], 0.0), axis=1, keepdims=True)
            s_ref[hh] = col * s + _dot_tn(k_b[:, hc], v)
            o_ref[rows, hc] = (_rms(o) * gate[:, hc]).astype(BF16)
        return carry

    lax.fori_loop(0, seq // c, body, 0)
    s_out_ref[...] = s_ref[...]


def _hg_prompt(exact_leaf, proj, lb, out_norm, *, n_batch, seq, heads):
    pm = _hg_prefix_matrix()
    hp = HG_HEADS_PER_STEP
    width = hp * HG_DH
    groups = heads // hp
    col = lambda part: (lambda b, h: (b, part * groups + h))
    return pl.pallas_call(
        functools.partial(_hg_prompt_kernel, exact_leaf),
        grid=(n_batch, groups),
        in_specs=[pl.BlockSpec((seq, width), col(0)),
                  pl.BlockSpec((seq, width), col(1)),
                  pl.BlockSpec((seq, width), col(2)),
                  pl.BlockSpec((seq, width), col(3)),
                  pl.BlockSpec((1, width), lambda b, h: (0, h)),
                  pl.BlockSpec((1, width), lambda b, h: (0, h)),
                  pl.BlockSpec(pm.shape, lambda b, h: (0, 0))],
        out_specs=[pl.BlockSpec((seq, width), lambda b, h: (b, h)),
                   pl.BlockSpec((None, hp, HG_DH, HG_DH), lambda b, h: (b, h, 0, 0))],
        out_shape=[jax.ShapeDtypeStruct((n_batch * seq, heads * HG_DH), BF16),
                   jax.ShapeDtypeStruct((n_batch, heads, HG_DH, HG_DH), F32)],
        scratch_shapes=[pltpu.VMEM((hp, HG_DH, HG_DH), F32)],
        compiler_params=_params("arbitrary", "arbitrary"),
        name="hg_prompt",
    )(proj, proj, proj, proj, lb, out_norm, pm)


def _hg_sample_kernel(steps, heads, q_ref, f_ref, i_ref, g_ref, lb_ref, gn_ref, s_in_ref, o_ref, s_out_ref):
    rows = q_ref.shape[0]
    shift = int(math.log2(steps))
    ri = lax.broadcasted_iota(I32, (rows, rows), 0)
    ci = lax.broadcasted_iota(I32, (rows, rows), 1)
    prefix = (((ri >> shift) == (ci >> shift)) & (ci <= ri)).astype(BF16)
    rid = lax.broadcasted_iota(I32, (rows, 1), 0)
    step = rid & (steps - 1)
    row_batch = rid >> shift
    ki = lax.broadcasted_iota(I32, (HG_DH, HG_DH), 0)
    vi = lax.broadcasted_iota(I32, (HG_DH, HG_DH), 1)
    eye = ki == vi

    def head(h):
        cols = pl.ds(pl.multiple_of(h * HG_DH, HG_DH), HG_DH)
        qh, kk, lf = _hg_gates(q_ref[:, cols].astype(F32), f_ref[:, cols].astype(F32), lb_ref[:, cols])
        v = i_ref[:, cols]
        vf = v.astype(F32)
        hi, lo = _split2(lf)
        b = _dot(prefix, jnp.concatenate([hi, lo], axis=1))
        b = b[:, :HG_DH] + b[:, HG_DH:]
        o = jnp.zeros((rows, HG_DH), F32)
        for dist in range(steps):
            if dist == 0:
                k_s, b_s, v_s = kk, b, vf
            else:
                k_s, b_s, v_s = (pltpu.roll(kk, dist, 0), pltpu.roll(b, dist, 0), pltpu.roll(vf, dist, 0))
            w = jnp.sum(qh * k_s * jnp.exp(jnp.minimum(b - b_s, 0.0)), axis=1, keepdims=True)
            o = o + jnp.where(step >= dist, w, 0.0) * v_s
        qd = (qh * jnp.exp(b)).astype(BF16)
        for j in range(rows // steps):
            s = s_in_ref[j, h]
            o = o + jnp.where(row_batch == j, _dot(qd, s.astype(BF16)), 0.0)
            b_last = b[(j + 1) * steps - 1:(j + 1) * steps, :]
            col = jnp.sum(jnp.where(eye, jnp.exp(b_last), 0.0), axis=1, keepdims=True)
            kj = jnp.where(row_batch == j, kk * jnp.exp(jnp.minimum(b_last - b, 0.0)), 0.0).astype(BF16)
            s_out_ref[j, h] = col * s + _dot_tn(kj, v)
        o_ref[:, cols] = (_rms(o) * gn_ref[:, cols] * _silu(g_ref[:, cols].astype(F32))).astype(BF16)

    def group(gi, carry):
        for hh in range(HG_HEADS_PER_STEP):
            head(gi * HG_HEADS_PER_STEP + hh)
        return carry

    lax.fori_loop(0, heads // HG_HEADS_PER_STEP, group, 0)


def _hg_sample(proj, state, n_prompt, steps, heads, lb, out_norm):
    n_sample_batch = state.shape[0]
    d = heads * HG_DH
    per = SAMPLE_ROWS // steps
    base = n_prompt // SAMPLE_ROWS
    col = lambda part: (lambda b: (base + b, part))
    return pl.pallas_call(
        functools.partial(_hg_sample_kernel, steps, heads),
        grid=(n_sample_batch // per,),
        in_specs=[pl.BlockSpec((SAMPLE_ROWS, d), col(0)),
                  pl.BlockSpec((SAMPLE_ROWS, d), col(1)),
                  pl.BlockSpec((SAMPLE_ROWS, d), col(2)),
                  pl.BlockSpec((SAMPLE_ROWS, d), col(3)),
                  pl.BlockSpec((1, d), lambda b: (0, 0)),
                  pl.BlockSpec((1, d), lambda b: (0, 0)),
                  pl.BlockSpec((per, heads, HG_DH, HG_DH), lambda b: (b, 0, 0, 0))],
        out_specs=[pl.BlockSpec((SAMPLE_ROWS, d), lambda b: (b, 0)),
                   pl.BlockSpec((per, heads, HG_DH, HG_DH), lambda b: (b, 0, 0, 0))],
        out_shape=[jax.ShapeDtypeStruct((n_sample_batch * steps, d), BF16),
                   jax.ShapeDtypeStruct(state.shape, F32)],
        compiler_params=_params("arbitrary"),
        name="hg_sample",
    )(proj, proj, proj, proj, lb, out_norm, state)


def _first_max(vals, lane, width):
    m = jnp.max(vals, axis=1, keepdims=True)
    idx = jnp.min(jnp.where(vals == m, lane, width), axis=1, keepdims=True)
    return m, idx


def _route(logits, bias):
    neg = -jnp.inf
    lane = lax.broadcasted_iota(I32, logits.shape, 1)
    group = lane >> GROUP_SHIFT
    scores = jax.nn.sigmoid(logits)
    biased = scores + bias
    best = sel = None
    for gi in range(N_GROUPS):
        vals = jnp.where(group == gi, biased, neg)
        m1, i1 = _first_max(vals, lane, N_EXPERTS)
        m2 = jnp.max(jnp.where(lane == i1, neg, vals), axis=1, keepdims=True)
        total = m1 + m2
        if gi == 0:
            best, sel = total, jnp.zeros_like(i1)
        else:
            better = total > best
            sel = jnp.where(better, gi, sel)
            best = jnp.where(better, total, best)
    vals = jnp.where(group == sel, biased, neg)
    _, e1 = _first_max(vals, lane, N_EXPERTS)
    _, e2 = _first_max(jnp.where(lane == e1, neg, vals), lane, N_EXPERTS)
    w1 = jnp.sum(jnp.where(lane == e1, scores, 0.0), axis=1, keepdims=True)
    w2 = jnp.sum(jnp.where(lane == e2, scores, 0.0), axis=1, keepdims=True)
    tot = w1 + w2
    return e1, e2, w1 / tot, w2 / tot


def _mixout_kernel(prompt_tiles, o_p, o_s, w_ref, x_p, x_s, g1_ref, g2_ref, gm_p, gm_s, sh_p, sh_s, sc_p, sc_s,
                   rw_ref, rb_ref, xo_p, xo_s, h_ref, ridx_ref, rwt_ref, cnt_ref, carry_ref, y_ref):
    i = pl.program_id(0)
    tm = x_p.shape[0]

    @pl.when(i == 0)
    def _():
        carry_ref[...] = jnp.zeros_like(carry_ref)

    def tile(sample):
        o_ref, x_ref, xo_ref = (o_s, x_s, xo_s) if sample else (o_p, x_p, xo_p)
        y_ref[...] = _dot(o_ref[...], w_ref[...])

        def chunk(rows, grp):
            x = x_ref[rows, :] + _mod_rows(sample, gm_p, gm_s, rows) * (_rms(y_ref[rows, :]) * g1_ref[...])
            xo_ref[rows, :] = x
            h = (_rms(x) * g2_ref[...] * (1.0 + _mod_rows(sample, sc_p, sc_s, rows))
                 + _mod_rows(sample, sh_p, sh_s, rows))
            h_ref[grp] = h.reshape(ROW_CHUNK // SUBLANES, SUBLANES, h.shape[1])

        _row_chunks(tm, chunk)

    _by_group(i >= prompt_tiles, tile)

    h1, h2 = _split2(h_ref[...].reshape(tm, h_ref.shape[2]))
    w1, w2 = _split2(rw_ref[...])
    first = _dot(h1, jnp.concatenate([w1, w2], axis=1))
    logits = first[:, :N_EXPERTS] + (first[:, N_EXPERTS:] + _dot(h2, w1))
    e1, e2, p1, p2 = _route(logits, rb_ref[...])

    lane = lax.broadcasted_iota(I32, (tm, N_EXPERTS), 1)
    hot1, hot2 = lane == e1, lane == e2
    onehot = (hot1 | hot2).astype(BF16)
    ti = lax.broadcasted_iota(I32, (tm, tm), 0)
    si = lax.broadcasted_iota(I32, (tm, tm), 1)
    before = _dot((si < ti).astype(BF16), onehot) + carry_ref[...]
    r1 = jnp.sum(jnp.where(hot1, before, 0.0), axis=1, keepdims=True).astype(I32)
    r2 = jnp.sum(jnp.where(hot2, before, 0.0), axis=1, keepdims=True).astype(I32)
    carry = carry_ref[...] + jnp.sum(onehot.astype(F32), axis=0, keepdims=True)
    carry_ref[...] = carry

    wide = lax.broadcasted_iota(I32, (tm, 128), 1)
    ridx_ref[...] = jnp.where(wide == 0, e1, jnp.where(wide == 1, e2, jnp.where(wide == 2, r1, r2)))
    rwt_ref[...] = jnp.where(wide == 0, p1, p2)
    cnt_ref[...] = jnp.zeros_like(cnt_ref)
    cnt_ref[0:1, 0:N_EXPERTS] = carry


def _mixout(tok, o_prompt, o_sample, w_out, x_p, x_s, g1, g2, mod_p, mod_s, router_w, router_b):
    d = tok.d
    v = o_prompt.shape[1]
    return pl.pallas_call(
        functools.partial(_mixout_kernel, tok.prompt_tiles),
        grid=(tok.tiles,),
        in_specs=tok.split_specs(v)
                 + [pl.BlockSpec((v, d), lambda i: (0, 0), pipeline_mode=pl.Buffered(1))]
                 + tok.split_specs(d) + [_const_spec((1, d)), _const_spec((1, d))]
                 + tok.mod_specs(2) + tok.mod_specs(3) + tok.mod_specs(4)
                 + [_const_spec((d, N_EXPERTS)), _const_spec((1, N_EXPERTS))],
        out_specs=tok.split_specs(d, out=True) + [
            pl.BlockSpec((TOKEN_TILE // SUBLANES, SUBLANES, d), lambda i: (i, 0, 0)),
            tok.row_spec(128), tok.row_spec(128), _const_spec((8, 128))],
        out_shape=tok.split_shapes(d, F32) + [jax.ShapeDtypeStruct((tok.n // SUBLANES, SUBLANES, d), F32),
                                              jax.ShapeDtypeStruct((tok.n, 128), I32),
                                              jax.ShapeDtypeStruct((tok.n, 128), F32),
                                              jax.ShapeDtypeStruct((8, 128), F32)],
        scratch_shapes=[pltpu.VMEM((1, N_EXPERTS), F32), pltpu.VMEM((TOKEN_TILE, d), F32)],
        compiler_params=_params("arbitrary"),
        name="mix_out",
    )(o_prompt, o_sample, w_out, x_p, x_s, g1, g2, mod_p, mod_s, mod_p, mod_s, mod_p, mod_s, router_w, router_b)


def _expert_plan(ridx, counts, n_tiles):
    n = ridx.shape[0]
    expert = ridx[:, 0:2]
    rank = ridx[:, 2:4]
    cnt = counts[0, :N_EXPERTS].astype(I32)
    padded = ((cnt + EXPERT_TILE - 1) // EXPERT_TILE) * EXPERT_TILE
    ends = jnp.cumsum(padded)
    starts = ends - padded
    pos = starts[expert] + rank
    tile = jnp.arange(n_tiles, dtype=I32)
    tile_start = tile * EXPERT_TILE
    used = tile_start < ends[-1]
    last_used = jnp.maximum(ends[-1] // EXPERT_TILE - 1, 0)
    tile_expert = jnp.sum((tile_start[:, None] >= ends[None, :]).astype(I32), axis=1)
    tile_expert = jnp.minimum(jnp.where(used, tile_expert, tile_expert[last_used]), N_EXPERTS - 1)
    valid = jnp.clip(cnt[tile_expert] - (tile_start - starts[tile_expert]), 0, EXPERT_TILE)
    valid = jnp.where(used, valid, 0)
    pos = pos.reshape(n // TOKEN_TILE, TOKEN_TILE, 2).transpose(0, 2, 1)
    sub = jnp.arange(EXPERT_F_SPLIT, dtype=I32)[None, :]
    snake = jnp.where((tile[:, None] & 1) == 0, sub, EXPERT_F_SPLIT - 1 - sub)
    part = jnp.where(used[:, None], snake, snake[last_used, EXPERT_F_SPLIT - 1]).reshape(-1)
    return tile_expert, valid, jnp.minimum(tile, last_used), part, pos


def _start_rows(groups, copy):
    def body(g, carry):
        for j in range(SUBLANES):
            copy(g, j).start()
        return carry
    lax.fori_loop(0, groups, body, 0)


def _dispatch_kernel(pos_ref, h_ref, xs_in, xs_hbm, sem):
    del xs_in
    groups = h_ref.shape[0]
    for choice in range(2):
        def row(g, j, choice=choice):
            p = pos_ref[0, choice, g * SUBLANES + j]
            return pltpu.make_async_copy(h_ref.at[g, pl.ds(j, 1)],
                                         xs_hbm.at[p >> 3, pl.ds(p & (SUBLANES - 1), 1)], sem.at[choice])
        _start_rows(groups, row)
    for choice in range(2):
        pltpu.make_async_copy(h_ref, xs_hbm.at[pl.ds(0, groups)], sem.at[choice]).wait()


def _dispatch(h, pos, n_rows, buf=None):
    d = h.shape[2]
    n = h.shape[0] * SUBLANES
    if buf is None:
        buf = jnp.zeros((n_rows // SUBLANES, SUBLANES, d), F32)
    tile_groups = TOKEN_TILE // SUBLANES
    return pl.pallas_call(
        _dispatch_kernel,
        grid=(n // TOKEN_TILE,),
        in_specs=[pl.BlockSpec((1, 2, TOKEN_TILE), lambda i: (i, 0, 0), memory_space=pltpu.SMEM),
                  pl.BlockSpec((tile_groups, SUBLANES, d), lambda i: (i, 0, 0)),
                  pl.BlockSpec(memory_space=pl.ANY)],
        out_specs=pl.BlockSpec(memory_space=pl.ANY),
        out_shape=jax.ShapeDtypeStruct((n_rows // SUBLANES, SUBLANES, d), F32),
        scratch_shapes=[pltpu.SemaphoreType.DMA((2,))],
        input_output_aliases={2: 0},
        compiler_params=_params("arbitrary"),
        name="dispatch",
    )(pos, h, buf)


def _expert_kernel(te_ref, nv_ref, tin_ref, part_ref, x_ref, wg_ref, wu_ref, wd_ref, y_ref):
    del te_ref, tin_ref, part_ref
    n = nv_ref[pl.program_id(0)]
    step = pl.program_id(1)
    rows = x_ref.shape[0] * SUBLANES

    @pl.when(n > 0)
    def _():
        x = x_ref[...].reshape(rows, x_ref.shape[2]).astype(BF16)
        a = _dot(x, wg_ref[...].astype(BF16))
        u = _dot(x, wu_ref[...].astype(BF16))
        y = _dot((_silu(a) * u).astype(BF16), wd_ref[...].astype(BF16)).reshape(y_ref.shape)

        @pl.when(step == 0)
        def _():
            y_ref[...] = y

        @pl.when(step > 0)
        def _():
            y_ref[...] += y

    @pl.when((n == 0) & (step == 0))
    def _():
        y_ref[...] = jnp.zeros_like(y_ref)


def _experts(xs, plan, layer, wg, wu, wd):
    d, f = wg.shape[2], wg.shape[3]
    fs = f // EXPERT_F_SPLIT
    tile_expert, valid, tile_in, part = plan[:4]
    n_tiles = tile_expert.shape[0]
    tile_groups = EXPERT_TILE // SUBLANES
    which = lambda i, s, part: part[i * EXPERT_F_SPLIT + s]
    grid_spec = pltpu.PrefetchScalarGridSpec(
        num_scalar_prefetch=4,
        grid=(n_tiles, EXPERT_F_SPLIT),
        in_specs=[pl.BlockSpec((tile_groups, SUBLANES, d), lambda i, s, te, nv, tin, part: (tin[i], 0, 0)),
                  pl.BlockSpec((None, None, d, fs),
                               lambda i, s, te, nv, tin, part: (layer, te[i], 0, which(i, s, part))),
                  pl.BlockSpec((None, None, d, fs),
                               lambda i, s, te, nv, tin, part: (layer, te[i], 0, which(i, s, part))),
                  pl.BlockSpec((None, None, fs, d),
                               lambda i, s, te, nv, tin, part: (layer, te[i], which(i, s, part), 0))],
        out_specs=pl.BlockSpec((tile_groups, SUBLANES, d), lambda i, s, te, nv, tin, part: (i, 0, 0)),
    )
    return pl.pallas_call(
        _expert_kernel,
        grid_spec=grid_spec,
        out_shape=jax.ShapeDtypeStruct((n_tiles * tile_groups, SUBLANES, d), F32),
        compiler_params=_params("arbitrary", "arbitrary"),
        name="experts",
    )(tile_expert, valid, tile_in, part, xs, wg, wu, wd)


def _by_group(is_sample, body):
    @pl.when(is_sample)
    def _():
        body(True)

    @pl.when(jnp.logical_not(is_sample))
    def _():
        body(False)


def _row_chunks(n_rows, fn):
    per = ROW_CHUNK // SUBLANES

    def body(ci, carry):
        fn(pl.ds(pl.multiple_of(ci * ROW_CHUNK, ROW_CHUNK), ROW_CHUNK), pl.ds(pl.multiple_of(ci * per, per), per))
        return carry
    lax.fori_loop(0, n_rows // ROW_CHUNK, body, 0, unroll=ROW_CHUNK_UNROLL)


def _mod_rows(sample, p_ref, s_ref, rows):
    return s_ref[rows, :] if sample else p_ref[...]


def _moeout_kernel(prompt_tiles, with_next, pos_ref, pos_next_ref, ys_hbm, x_p, x_s, rwt_ref, g3_ref, gf_p, gf_s,
                   *rest):
    if with_next:
        gn_ref, sh_p, sh_s, sc_p, sc_s, xo_p, xo_s, h_ref, ybuf, sem = rest
    else:
        xo_p, xo_s, ybuf, sem = rest
    groups = ybuf.shape[2]
    n_rows = groups * SUBLANES
    d = ybuf.shape[4]
    i = pl.program_id(0)
    slot = i & 1

    def fetch(p_ref, to_slot):
        for choice in range(2):
            def row(g, j, choice=choice):
                p = p_ref[0, choice, g * SUBLANES + j]
                return pltpu.make_async_copy(ys_hbm.at[p >> 3, pl.ds(p & (SUBLANES - 1), 1)],
                                             ybuf.at[to_slot, choice, g, pl.ds(j, 1)], sem.at[to_slot, choice])
            _start_rows(groups, row)

    @pl.when(i == 0)
    def _():
        fetch(pos_ref, 0)

    @pl.when(i + 1 < pl.num_programs(0))
    def _():
        fetch(pos_next_ref, 1 - slot)

    for choice in range(2):
        pltpu.make_async_copy(ys_hbm.at[pl.ds(0, groups)], ybuf.at[slot, choice], sem.at[slot, choice]).wait()

    def tile(sample):
        x_ref, xo_ref = (x_s, xo_s) if sample else (x_p, xo_p)

        def chunk(rows, grp):
            rw = rwt_ref[rows, :]
            y = (rw[:, 0:1] * ybuf[slot, 0, grp].reshape(ROW_CHUNK, d)
                 + rw[:, 1:2] * ybuf[slot, 1, grp].reshape(ROW_CHUNK, d))
            x = x_ref[rows, :] + _mod_rows(sample, gf_p, gf_s, rows) * (_rms(y) * g3_ref[...])
            xo_ref[rows, :] = x
            if with_next:
                h = (_rms(x) * gn_ref[...] * (1.0 + _mod_rows(sample, sc_p, sc_s, rows))
                     + _mod_rows(sample, sh_p, sh_s, rows))
                h_ref[rows, :] = h.astype(BF16)

        _row_chunks(n_rows, chunk)

    _by_group(pl.program_id(0) >= prompt_tiles, tile)


def _moeout(tok, pos, ys, x_p, x_s, rwt, g3, mod_p, mod_s, nxt=None):
    d = tok.d
    last = tok.tiles - 1
    ins = ([pl.BlockSpec((1, 2, TOKEN_TILE), lambda i: (i, 0, 0), memory_space=pltpu.SMEM),
            pl.BlockSpec((1, 2, TOKEN_TILE), lambda i: (jnp.minimum(i + 1, last), 0, 0), memory_space=pltpu.SMEM),
            pl.BlockSpec(memory_space=pl.ANY)]
           + tok.split_specs(d) + [tok.row_spec(128), _const_spec((1, d))] + tok.mod_specs(5))
    args = [pos, pos, ys, x_p, x_s, rwt, g3, mod_p, mod_s]
    scratch = [pltpu.VMEM((2, 2, TOKEN_TILE // SUBLANES, SUBLANES, d), F32), pltpu.SemaphoreType.DMA((2, 2))]
    if nxt is None:
        return pl.pallas_call(
            functools.partial(_moeout_kernel, tok.prompt_tiles, False),
            grid=(tok.tiles,), in_specs=ins, out_specs=tok.split_specs(d, out=True),
            out_shape=tok.split_shapes(d, F32), scratch_shapes=scratch,
            compiler_params=_params("arbitrary"), name="moe_out_last",
        )(*args)
    gain_n, mod_pn, mod_sn = nxt
    ins = ins + [_const_spec((1, d))] + tok.mod_specs(0) + tok.mod_specs(1)
    args = args + [gain_n, mod_pn, mod_sn, mod_pn, mod_sn]
    return pl.pallas_call(
        functools.partial(_moeout_kernel, tok.prompt_tiles, True),
        grid=(tok.tiles,), in_specs=ins, out_specs=tok.split_specs(d, out=True) + [tok.row_spec(d)],
        out_shape=tok.split_shapes(d, F32) + [jax.ShapeDtypeStruct((tok.n, d), BF16)],
        scratch_shapes=scratch,
        compiler_params=_params("arbitrary"), name="moe_out_next",
    )(*args)


def kernel(x_prompt, x_sample, c_prompt, c_sample, state_ret, state_hgrn, ada_w, ada_b, norm_gains,
           ret_w_in, ret_w_out, hg_w_in, hg_w_out, hg_lower_bound, hg_out_norm, router_w, router_b,
           exp_w_gate, exp_w_up, exp_w_down):
    n_batch, seq, d = x_prompt.shape
    n_dec, steps, _ = x_sample.shape
    depth = ada_w.shape[0]
    assert depth == 2 and d % RET_DK == 0 and d % HG_DH == 0
    assert seq % RET_CHUNK == 0 and (n_dec * steps) % TOKEN_TILE == 0 and SAMPLE_ROWS % steps == 0
    ret_heads = d // RET_DK
    hg_heads = d // HG_DH
    n_prompt = n_batch * seq
    n_sample = n_dec * steps
    tok = _Tokens(n_batch, seq, n_sample, d)
    n = tok.n

    n_cond = n_batch + n_dec
    pad = (-n_cond) % 8
    c_all = jnp.concatenate([c_prompt, c_sample, jnp.zeros((pad, d), F32)], axis=0)
    mod = _ada(c_all, ada_w, ada_b)
    mods = []
    for l in range(depth):
        mod_p = mod[l, :n_batch].reshape(n_batch, 6, 1, d)
        mod_s = jnp.take(mod[l], n_batch + jnp.arange(n_sample, dtype=I32) // steps, axis=0)
        mods.append((mod_p, mod_s))
    gain = lambda l, k: norm_gains[l, k].reshape(1, d)

    x_p = x_prompt.reshape(n_prompt, d)
    x_s = x_sample.reshape(n_sample, d)
    n_tiles = (2 * n + N_EXPERTS * (EXPERT_TILE - 1)) // EXPERT_TILE + 1
    rw = router_w.astype(F32)
    rb = router_b.astype(F32).reshape(1, N_EXPERTS)

    def channel_mixer(l, x_p, x_s, h, ridx, rwt, counts, nxt, buf):
        plan = _expert_plan(ridx, counts, n_tiles)
        pos = plan[4]
        xs = _dispatch(h, pos, n_tiles * EXPERT_TILE, buf)
        ys = _experts(xs, plan, l, exp_w_gate, exp_w_up, exp_w_down)
        return _moeout(tok, pos, ys, x_p, x_s, rwt, gain(l, 3), *mods[l], nxt=nxt), xs

    h = _prenorm(tok, x_p, x_s, gain(0, 0), *mods[0])
    proj = _proj(h, ret_w_in[0])
    log_gamma = jnp.log(1.0 - jnp.exp2(-5.0 - jnp.arange(ret_heads, dtype=F32)))
    log_gamma = jnp.broadcast_to(log_gamma[:, None, None], (ret_heads, 1, 128))
    cos_p, sin_p = _rope_tables(jnp.arange(seq))
    cos_s, sin_s = _rope_tables(PAST_LEN + jnp.arange(steps))
    reps = SAMPLE_ROWS // steps
    cos_s, sin_s = jnp.tile(cos_s, (reps, 1)), jnp.tile(sin_s, (reps, 1))
    o_p, ret_prompt = _ret_prompt(proj, n_batch, seq, ret_heads, cos_p, sin_p, log_gamma)
    o_s, ret_sample = _ret_sample(proj, state_ret[0], n_prompt, steps, ret_heads, cos_s, sin_s, log_gamma)
    x_p, x_s, hp, ridx, rwt, counts = _mixout(tok, o_p, o_s, ret_w_out[0].astype(BF16), x_p, x_s,
                                              gain(0, 1), gain(0, 2), *mods[0], rw, rb)
    (x_p, x_s, h), row_buf = channel_mixer(0, x_p, x_s, hp, ridx, rwt, counts, (gain(1, 0),) + mods[1], None)

    sm = jax.nn.softmax(hg_lower_bound.astype(F32), axis=0)
    lb = (jnp.cumsum(sm, axis=0) - sm[0])[1].reshape(1, d)
    proj = _proj(h, hg_w_in[0])
    out_norm = hg_out_norm[0].reshape(1, d)
    leaf_min = _hg_guard(proj, n_prompt, hg_heads, lb)[0, 0]
    o_p, hg_prompt = lax.cond(
        leaf_min > HG_LEAF_LOG_LIMIT,
        functools.partial(_hg_prompt, False, n_batch=n_batch, seq=seq, heads=hg_heads),
        functools.partial(_hg_prompt, True, n_batch=n_batch, seq=seq, heads=hg_heads),
        proj, lb, out_norm)
    o_s, hg_sample = _hg_sample(proj, state_hgrn[0], n_prompt, steps, hg_heads, lb, out_norm)
    x_p, x_s, hp, ridx, rwt, counts = _mixout(tok, o_p, o_s, hg_w_out[0].astype(BF16), x_p, x_s,
                                              gain(1, 1), gain(1, 2), *mods[1], rw, rb)
    (x_p, x_s), _ = channel_mixer(1, x_p, x_s, hp, ridx, rwt, counts, None, row_buf)

    return (x_p.reshape(n_batch, seq, d), x_s.reshape(n_dec, steps, d),
            ret_prompt[None], hg_prompt[None], ret_sample[None], hg_sample[None])
```

```python
import functools
import math

import numpy as np
import jax
import jax.numpy as jnp
from jax import lax
from jax.experimental import pallas as pl
from jax.experimental.pallas import tpu as pltpu

F32, BF16, I32 = jnp.float32, jnp.bfloat16, jnp.int32

EPS = 1e-6
ROPE_BASE = 10000.0
PAST_LEN = 16384
RET_DK = 256
RET_DV = 512
HG_DH = 128
N_EXPERTS = 16
GROUP_SHIFT = 2
N_GROUPS = N_EXPERTS >> GROUP_SHIFT

TOKEN_TILE = 256
PROJ_TILE = 512
PROJ_COLS = 2048
EXPERT_TILE = 384
RET_CHUNK = 256
HG_CHUNK = 128
HG_LEAF = 16
HG_LEVELS = (64, 32, 16)
HG_HEADS_PER_STEP = 4
HG_LEAF_LOG_LIMIT = -80.0
SAMPLE_ROWS = 16
SUBLANES = 8
EXPERT_F_SPLIT = 2
RET_SAMPLE_HEADS = 4
ROW_CHUNK = 16
ROW_CHUNK_UNROLL = 4
VMEM_LIMIT = 56 * 1024 * 1024


def _params(*sem):
    return pltpu.CompilerParams(dimension_semantics=sem, vmem_limit_bytes=VMEM_LIMIT)


def _rms(x):
    return x * lax.rsqrt(jnp.mean(x * x, axis=-1, keepdims=True) + EPS)


def _silu(x):
    return x * jax.nn.sigmoid(x)


def _dot(a, b):
    return jnp.dot(a, b, preferred_element_type=F32)


def _dot_nt(a, b):
    return lax.dot_general(a, b, (((1,), (1,)), ((), ())), preferred_element_type=F32)


def _dot_tn(a, b):
    return lax.dot_general(a, b, (((0,), (0,)), ((), ())), preferred_element_type=F32)


def _split2(x):
    hi = x.astype(BF16)
    lo = (x - hi.astype(F32)).astype(BF16)
    return hi, lo


def _split3(x):
    hi = x.astype(BF16)
    r = x - hi.astype(F32)
    mid = r.astype(BF16)
    lo = (r - mid.astype(F32)).astype(BF16)
    return hi, mid, lo


def _ada_kernel(c_ref, w_ref, b_ref, o_ref):
    s = _silu(c_ref[...]).astype(BF16)
    o_ref[...] = _dot(s, w_ref[...].astype(BF16)) + b_ref[...]


def _ada(c_all, ada_w, ada_b):
    depth, d, d6 = ada_w.shape
    m = c_all.shape[0]
    tn = min(1024, d6)
    return pl.pallas_call(
        _ada_kernel,
        grid=(depth, d6 // tn),
        in_specs=[pl.BlockSpec((m, d), lambda l, j: (0, 0)),
                  pl.BlockSpec((None, d, tn), lambda l, j: (l, 0, j)),
                  pl.BlockSpec((None, 1, tn), lambda l, j: (l, 0, j))],
        out_specs=pl.BlockSpec((None, m, tn), lambda l, j: (l, 0, j)),
        out_shape=jax.ShapeDtypeStruct((depth, m, d6), F32),
        compiler_params=_params("arbitrary", "arbitrary"),
        name="ada_mod",
    )(c_all, ada_w, ada_b.reshape(depth, 1, d6))


class _Tokens:
    def __init__(self, n_prompt_batch, seq, n_sample, d):
        self.d = d
        self.n_prompt = n_prompt_batch * seq
        self.n = self.n_prompt + n_sample
        self.prompt_tiles = self.n_prompt // TOKEN_TILE
        self.tiles_per_batch = seq // TOKEN_TILE
        self.n_batch = n_prompt_batch
        self.tiles = self.n // TOKEN_TILE

    def mod_specs(self, comp):
        d, tpb, nb, npt = self.d, self.tiles_per_batch, self.n_batch, self.prompt_tiles
        return [pl.BlockSpec((None, None, 1, d), lambda i: (jnp.minimum(i // tpb, nb - 1), comp, 0, 0)),
                pl.BlockSpec((TOKEN_TILE, d), lambda i: (jnp.maximum(i - npt, 0), comp),
                             pipeline_mode=pl.Buffered(1))]

    def row_spec(self, width):
        return pl.BlockSpec((TOKEN_TILE, width), lambda i: (i, 0))

    def split_specs(self, width, out=False):
        npt = self.prompt_tiles
        mode = {} if out else dict(pipeline_mode=pl.Buffered(1))
        return [pl.BlockSpec((TOKEN_TILE, width), lambda i: (jnp.minimum(i, npt - 1), 0)),
                pl.BlockSpec((TOKEN_TILE, width), lambda i: (jnp.maximum(i - npt, 0), 0), **mode)]

    def split_shapes(self, width, dtype):
        return [jax.ShapeDtypeStruct((self.n_prompt, width), dtype),
                jax.ShapeDtypeStruct((self.n - self.n_prompt, width), dtype)]


def _const_spec(shape):
    return pl.BlockSpec(shape, lambda i: (0,) * len(shape))


def _pick(is_sample, p_ref, s_ref):
    return jnp.where(is_sample, s_ref[...], p_ref[...])


def _store_split(is_sample, p_ref, s_ref, val):
    @pl.when(is_sample)
    def _():
        s_ref[...] = val

    @pl.when(jnp.logical_not(is_sample))
    def _():
        p_ref[...] = val


def _prenorm_kernel(prompt_tiles, x_p, x_s, g_ref, sh_p, sh_s, sc_p, sc_s, h_ref):
    def tile(sample):
        x_ref = x_s if sample else x_p

        def chunk(rows, _):
            h = (_rms(x_ref[rows, :]) * g_ref[...] * (1.0 + _mod_rows(sample, sc_p, sc_s, rows))
                 + _mod_rows(sample, sh_p, sh_s, rows))
            h_ref[rows, :] = h.astype(BF16)

        _row_chunks(x_p.shape[0], chunk)

    _by_group(pl.program_id(0) >= prompt_tiles, tile)


def _prenorm(tok, x_p, x_s, gain, mod_p, mod_s):
    d = tok.d
    return pl.pallas_call(
        functools.partial(_prenorm_kernel, tok.prompt_tiles),
        grid=(tok.tiles,),
        in_specs=tok.split_specs(d) + [_const_spec((1, d))] + tok.mod_specs(0) + tok.mod_specs(1),
        out_specs=tok.row_spec(d),
        out_shape=jax.ShapeDtypeStruct((tok.n, d), BF16),
        compiler_params=_params("arbitrary"),
        name="prenorm",
    )(x_p, x_s, gain, mod_p, mod_s, mod_p, mod_s)


def _proj_kernel(h_ref, w_ref, o_ref, wb_ref):
    @pl.when(pl.program_id(1) == 0)
    def _():
        wb_ref[...] = w_ref[...].astype(BF16)

    o_ref[...] = _dot(h_ref[...], wb_ref[...]).astype(BF16)


def _proj(h, w):
    n, d = h.shape
    p = w.shape[1]
    tm = PROJ_TILE if n % PROJ_TILE == 0 else TOKEN_TILE
    tn = PROJ_COLS if p % PROJ_COLS == 0 else PROJ_COLS // 2
    assert n % tm == 0 and p % tn == 0
    return pl.pallas_call(
        _proj_kernel,
        grid=(p // tn, n // tm),
        in_specs=[pl.BlockSpec((tm, d), lambda j, i: (i, 0)),
                  pl.BlockSpec((d, tn), lambda j, i: (0, j))],
        out_specs=pl.BlockSpec((tm, tn), lambda j, i: (i, j)),
        out_shape=jax.ShapeDtypeStruct((n, p), BF16),
        scratch_shapes=[pltpu.VMEM((d, tn), BF16)],
        compiler_params=_params("arbitrary", "arbitrary"),
        name="in_proj",
    )(h, w)


def _rope_tables(pos):
    half = RET_DK // 2
    theta = 1.0 / (ROPE_BASE ** jnp.linspace(0.0, 1.0, half, dtype=F32))
    ang = pos.astype(F32)[:, None] * theta[None, :]
    cos, sin = jnp.cos(ang), jnp.sin(ang)
    return (jnp.repeat(cos, 2, axis=1),
            jnp.stack([-sin, sin], axis=-1).reshape(pos.shape[0], RET_DK))


def _rot(x, cos, sin_signed):
    lane = lax.broadcasted_iota(I32, x.shape, 1)
    width = x.shape[1]
    nbr = jnp.where((lane & 1) == 0, pltpu.roll(x, width - 1, 1), pltpu.roll(x, 1, 1))
    return x * cos + nbr * sin_signed


def _gate_out(o, g_ref_val):
    return (_rms(o) * _silu(g_ref_val.astype(F32))).astype(BF16)


def _ret_prompt_kernel(q_ref, k_ref, v_ref, g_ref, cos_ref, sin_ref, lg_ref, o_ref, s_out_ref, s_ref):
    c = RET_CHUNK
    seq = q_ref.shape[0]
    lg = lg_ref[0:1, 0:1]
    ti = lax.broadcasted_iota(I32, (c, c), 0)
    si = lax.broadcasted_iota(I32, (c, c), 1)
    decay = jnp.where(ti >= si, jnp.exp((ti - si).astype(F32) * lg), 0.0)
    tcol = lax.broadcasted_iota(I32, (c, 1), 0).astype(F32)
    dq = jnp.exp((tcol + 1.0) * lg)
    dk = jnp.exp((float(c - 1) - tcol) * lg)
    dchunk = jnp.exp(float(c) * lg)
    s_ref[...] = jnp.zeros_like(s_ref)

    def body(ci, carry):
        r0 = pl.multiple_of(ci * c, c)
        rows = pl.ds(r0, c)
        cos, sin = cos_ref[rows, :], sin_ref[rows, :]
        q = _rot(q_ref[rows, :].astype(F32), cos, sin)
        k = _rot(k_ref[rows, :].astype(F32), cos, sin) * (RET_DK ** -0.5)
        v = v_ref[rows, :]
        s = s_ref[...]
        scores = _dot_nt(q.astype(BF16), k.astype(BF16)) * decay
        o = _dot(scores.astype(BF16), v) + _dot((q * dq).astype(BF16), s.astype(BF16))
        s_ref[...] = dchunk * s + _dot_tn((k * dk).astype(BF16), v)
        o_ref[rows, :] = _gate_out(o, g_ref[rows, :])
        return carry

    lax.fori_loop(0, seq // c, body, 0)
    s_out_ref[...] = s_ref[...]


def _ret_prompt(proj, n_batch, seq, heads, cos, sin, log_gamma):
    qk_blocks = heads
    return pl.pallas_call(
        _ret_prompt_kernel,
        grid=(n_batch, heads),
        in_specs=[pl.BlockSpec((seq, RET_DK), lambda b, h: (b, h)),
                  pl.BlockSpec((seq, RET_DK), lambda b, h: (b, qk_blocks + h)),
                  pl.BlockSpec((seq, RET_DV), lambda b, h: (b, heads + h)),
                  pl.BlockSpec((seq, RET_DV), lambda b, h: (b, 2 * heads + h)),
                  pl.BlockSpec((seq, RET_DK), lambda b, h: (0, 0)),
                  pl.BlockSpec((seq, RET_DK), lambda b, h: (0, 0)),
                  pl.BlockSpec((None, 1, 128), lambda b, h: (h, 0, 0))],
        out_specs=[pl.BlockSpec((seq, RET_DV), lambda b, h: (b, h)),
                   pl.BlockSpec((None, None, RET_DK, RET_DV), lambda b, h: (b, h, 0, 0))],
        out_shape=[jax.ShapeDtypeStruct((n_batch * seq, heads * RET_DV), BF16),
                   jax.ShapeDtypeStruct((n_batch, heads, RET_DK, RET_DV), F32)],
        scratch_shapes=[pltpu.VMEM((RET_DK, RET_DV), F32)],
        compiler_params=_params("arbitrary", "arbitrary"),
        name="ret_prompt",
    )(proj, proj, proj, proj, cos, sin, log_gamma)


def _ret_sample_kernel(steps, q_ref, k_ref, v_ref, g_ref, cos_ref, sin_ref, lg_ref, s_in_ref, o_ref, s_out_ref):
    rows = q_ref.shape[0]
    shift = int(math.log2(steps))
    cos, sin = cos_ref[...], sin_ref[...]
    ri = lax.broadcasted_iota(I32, (rows, rows), 0)
    ci = lax.broadcasted_iota(I32, (rows, rows), 1)
    pair = ((ri >> shift) == (ci >> shift)) & (ri >= ci)
    rid = lax.broadcasted_iota(I32, (rows, 1), 0)
    step = (rid & (steps - 1)).astype(F32)
    row_batch = rid >> shift
    for hh in range(q_ref.shape[1] // RET_DK):
        kc = slice(hh * RET_DK, (hh + 1) * RET_DK)
        vc = slice(hh * RET_DV, (hh + 1) * RET_DV)
        lg = lg_ref[hh, 0:1, 0:1]
        q = _rot(q_ref[:, kc].astype(F32), cos, sin)
        k = _rot(k_ref[:, kc].astype(F32), cos, sin) * (RET_DK ** -0.5)
        v = v_ref[:, vc]
        decay = jnp.where(pair, jnp.exp((ri - ci).astype(F32) * lg), 0.0)
        o = _dot((_dot_nt(q.astype(BF16), k.astype(BF16)) * decay).astype(BF16), v)
        qd = (q * jnp.exp((step + 1.0) * lg)).astype(BF16)
        kd = k * jnp.exp((float(steps - 1) - step) * lg)
        dall = jnp.exp(float(steps) * lg)
        for j in range(rows // steps):
            s = s_in_ref[j, hh]
            o = o + jnp.where(row_batch == j, _dot(qd, s.astype(BF16)), 0.0)
            kj = jnp.where(row_batch == j, kd, 0.0).astype(BF16)
            s_out_ref[j, hh] = dall * s + _dot_tn(kj, v)
        o_ref[:, vc] = _gate_out(o, g_ref[:, vc])


def _ret_sample(proj, state, n_prompt, steps, heads, cos, sin, log_gamma):
    n_sample_batch = state.shape[0]
    per = SAMPLE_ROWS // steps
    base = n_prompt // SAMPLE_ROWS
    hs = min(RET_SAMPLE_HEADS, heads)
    groups = heads // hs
    kw, vw = hs * RET_DK, hs * RET_DV
    return pl.pallas_call(
        functools.partial(_ret_sample_kernel, steps),
        grid=(n_sample_batch // per, groups),
        in_specs=[pl.BlockSpec((SAMPLE_ROWS, kw), lambda b, h: (base + b, h)),
                  pl.BlockSpec((SAMPLE_ROWS, kw), lambda b, h: (base + b, groups + h)),
                  pl.BlockSpec((SAMPLE_ROWS, vw), lambda b, h: (base + b, groups + h)),
                  pl.BlockSpec((SAMPLE_ROWS, vw), lambda b, h: (base + b, 2 * groups + h)),
                  pl.BlockSpec((SAMPLE_ROWS, RET_DK), lambda b, h: (0, 0)),
                  pl.BlockSpec((SAMPLE_ROWS, RET_DK), lambda b, h: (0, 0)),
                  pl.BlockSpec((hs, 1, 128), lambda b, h: (h, 0, 0)),
                  pl.BlockSpec((per, hs, RET_DK, RET_DV), lambda b, h: (b, h, 0, 0))],
        out_specs=[pl.BlockSpec((SAMPLE_ROWS, vw), lambda b, h: (b, h)),
                   pl.BlockSpec((per, hs, RET_DK, RET_DV), lambda b, h: (b, h, 0, 0))],
        out_shape=[jax.ShapeDtypeStruct((n_sample_batch * steps, heads * RET_DV), BF16),
                   jax.ShapeDtypeStruct(state.shape, F32)],
        compiler_params=_params("arbitrary", "arbitrary"),
        name="ret_sample",
    )(proj, proj, proj, proj, cos, sin, log_gamma, state)


def _hg_prefix_matrix():
    c = HG_CHUNK
    t = np.arange(c)[:, None]
    s = np.arange(c)[None, :]
    le = (s <= t).astype(np.float32)
    mats = [le, le * ((s // HG_LEAF) == (t // HG_LEAF))]
    for half in HG_LEVELS:
        mid = (t // (2 * half)) * (2 * half) + half - 1
        mats.append(le - (s <= mid).astype(np.float32))
    return jnp.asarray(np.concatenate(mats, axis=0), dtype=BF16)


def _hg_gates(q, f, lb):
    forget = lb + (1.0 - lb) * jax.nn.sigmoid(f)
    return _silu(q), 1.0 - forget, jnp.log(forget)


def _hg_guard_kernel(f_ref, lb_ref, o_ref):
    @pl.when(pl.program_id(0) == 0)
    def _():
        o_ref[...] = jnp.zeros_like(o_ref)

    rows, width = f_ref.shape
    lb = lb_ref[...]

    def leaf(li, lowest):
        r = pl.ds(pl.multiple_of(li * HG_LEAF, HG_LEAF), HG_LEAF)
        lf = jnp.log(lb + (1.0 - lb) * jax.nn.sigmoid(f_ref[r, :].astype(F32)))
        return jnp.minimum(lowest, jnp.sum(lf, axis=0, keepdims=True))

    lowest = lax.fori_loop(0, rows // HG_LEAF, leaf, jnp.zeros((1, width), F32), unroll=ROW_CHUNK_UNROLL)
    o_ref[...] = jnp.minimum(o_ref[...], jnp.min(lowest))


def _hg_guard(proj, n_prompt, heads, lb):
    d = heads * HG_DH
    tile = PROJ_TILE
    return pl.pallas_call(
        _hg_guard_kernel,
        grid=(n_prompt // tile,),
        in_specs=[pl.BlockSpec((tile, d), lambda i: (i, 1)),
                  pl.BlockSpec((1, d), lambda i: (0, 0))],
        out_specs=pl.BlockSpec((8, 128), lambda i: (0, 0)),
        out_shape=jax.ShapeDtypeStruct((8, 128), F32),
        compiler_params=_params("arbitrary"),
        name="hg_guard",
    )(proj, lb)


def _hg_prompt_kernel(exact_leaf, q_ref, f_ref, i_ref, g_ref, lb_ref, gn_ref, pm_ref, o_ref, s_out_ref, s_ref):
    c = HG_CHUNK
    seq = q_ref.shape[0]
    n_heads = q_ref.shape[1] // HG_DH
    ti = lax.broadcasted_iota(I32, (c, c), 0)
    si = lax.broadcasted_iota(I32, (c, c), 1)
    leaf_shift = int(math.log2(HG_LEAF))
    mask_leaf = (si <= ti) & ((ti >> leaf_shift) == (si >> leaf_shift))
    level_masks = []
    for half in HG_LEVELS:
        sh = int(math.log2(2 * half))
        level_masks.append(((ti >> sh) == (si >> sh)) & ((ti & (2 * half - 1)) >= half) & ((si & (2 * half - 1)) < half))
    eye = ti == si
    leaf_pos = lax.broadcasted_iota(I32, (c, 1), 0) & (HG_LEAF - 1)
    s_ref[...] = jnp.zeros_like(s_ref)

    width = n_heads * HG_DH
    head_cols = [slice(hh * HG_DH, (hh + 1) * HG_DH) for hh in range(n_heads)]

    def body(ci, carry):
        rows = pl.ds(pl.multiple_of(ci * c, c), c)
        qh, kk, lf = _hg_gates(q_ref[rows, :].astype(F32), f_ref[rows, :].astype(F32), lb_ref[...])
        sums = _dot(pm_ref[...], jnp.concatenate(_split2(lf), axis=1))
        sums = sums[:, :width] + sums[:, width:]
        b = sums[0:c]
        if exact_leaf:
            a = [jnp.zeros((c, c), F32) for _ in head_cols]
            vf = i_ref[rows, :].astype(F32)
            o_leaf = [jnp.zeros((c, HG_DH), F32) for _ in head_cols]
            for dist in range(HG_LEAF):
                if dist == 0:
                    k_s, b_s, v_s = kk, b, vf
                else:
                    k_s, b_s, v_s = pltpu.roll(kk, dist, 0), pltpu.roll(b, dist, 0), pltpu.roll(vf, dist, 0)
                pair = qh * k_s * jnp.exp(jnp.minimum(b - b_s, 0.0))
                for hh, hc in enumerate(head_cols):
                    w = jnp.sum(pair[:, hc], axis=1, keepdims=True)
                    o_leaf[hh] = o_leaf[hh] + jnp.where(leaf_pos >= dist, w, 0.0) * v_s[:, hc]
        else:
            d_leaf = sums[c:2 * c]
            q_f = (qh * jnp.exp(d_leaf)).astype(BF16)
            k_f = (kk * jnp.exp(-d_leaf)).astype(BF16)
            a = [jnp.where(mask_leaf, _dot_nt(q_f[:, hc], k_f[:, hc]), 0.0) for hc in head_cols]
            o_leaf = [0.0 for _ in head_cols]
        for lvl, mask in enumerate(level_masks):
            w = jnp.exp(-jnp.abs(sums[(2 + lvl) * c:(3 + lvl) * c]))
            q_f = (qh * w).astype(BF16)
            k_f = (kk * w).astype(BF16)
            a = [a_h + jnp.where(mask, _dot_nt(q_f[:, hc], k_f[:, hc]), 0.0) for a_h, hc in zip(a, head_cols)]
        b_last = b[c - 1:c, :]
        q_b = (qh * jnp.exp(b)).astype(BF16)
        k_b = (kk * jnp.exp(b_last - b)).astype(BF16)
        e_last = jnp.exp(b_last)
        gate = gn_ref[...] * _silu(g_ref[rows, :].astype(F32))
        for hh, hc in enumerate(head_cols):
            v = i_ref[rows, hc]
            s = s_ref[hh]
            o = _dot(a[hh].astype(BF16), v) + _dot(q_b[:, hc], s.astype(BF16)) + o_leaf[hh]
            col = jnp.sum(jnp.where(eye, e_last[:, hc], 0.0), axis=1, keepdims=True)
            s_ref[hh] = col * s + _dot_tn(k_b[:, hc], v)
            o_ref[rows, hc] = (_rms(o) * gate[:, hc]).astype(BF16)
        return carry

    lax.fori_loop(0, seq // c, body, 0)
    s_out_ref[...] = s_ref[...]


def _hg_prompt(exact_leaf, proj, lb, out_norm, *, n_batch, seq, heads):
    pm = _hg_prefix_matrix()
    hp = HG_HEADS_PER_STEP
    width = hp * HG_DH
    groups = heads // hp
    col = lambda part: (lambda b, h: (b, part * groups + h))
    return pl.pallas_call(
        functools.partial(_hg_prompt_kernel, exact_leaf),
        grid=(n_batch, groups),
        in_specs=[pl.BlockSpec((seq, width), col(0)),
                  pl.BlockSpec((seq, width), col(1)),
                  pl.BlockSpec((seq, width), col(2)),
                  pl.BlockSpec((seq, width), col(3)),
                  pl.BlockSpec((1, width), lambda b, h: (0, h)),
                  pl.BlockSpec((1, width), lambda b, h: (0, h)),
                  pl.BlockSpec(pm.shape, lambda b, h: (0, 0))],
        out_specs=[pl.BlockSpec((seq, width), lambda b, h: (b, h)),
                   pl.BlockSpec((None, hp, HG_DH, HG_DH), lambda b, h: (b, h, 0, 0))],
        out_shape=[jax.ShapeDtypeStruct((n_batch * seq, heads * HG_DH), BF16),
                   jax.ShapeDtypeStruct((n_batch, heads, HG_DH, HG_DH), F32)],
        scratch_shapes=[pltpu.VMEM((hp, HG_DH, HG_DH), F32)],
        compiler_params=_params("arbitrary", "arbitrary"),
        name="hg_prompt",
    )(proj, proj, proj, proj, lb, out_norm, pm)


def _hg_sample_kernel(steps, heads, q_ref, f_ref, i_ref, g_ref, lb_ref, gn_ref, s_in_ref, o_ref, s_out_ref):
    rows = q_ref.shape[0]
    shift = int(math.log2(steps))
    ri = lax.broadcasted_iota(I32, (rows, rows), 0)
    ci = lax.broadcasted_iota(I32, (rows, rows), 1)
    prefix = (((ri >> shift) == (ci >> shift)) & (ci <= ri)).astype(BF16)
    rid = lax.broadcasted_iota(I32, (rows, 1), 0)
    step = rid & (steps - 1)
    row_batch = rid >> shift
    ki = lax.broadcasted_iota(I32, (HG_DH, HG_DH), 0)
    vi = lax.broadcasted_iota(I32, (HG_DH, HG_DH), 1)
    eye = ki == vi

    def head(h):
        cols = pl.ds(pl.multiple_of(h * HG_DH, HG_DH), HG_DH)
        qh, kk, lf = _hg_gates(q_ref[:, cols].astype(F32), f_ref[:, cols].astype(F32), lb_ref[:, cols])
        v = i_ref[:, cols]
        vf = v.astype(F32)
        hi, lo = _split2(lf)
        b = _dot(prefix, jnp.concatenate([hi, lo], axis=1))
        b = b[:, :HG_DH] + b[:, HG_DH:]
        o = jnp.zeros((rows, HG_DH), F32)
        for dist in range(steps):
            if dist == 0:
                k_s, b_s, v_s = kk, b, vf
            else:
                k_s, b_s, v_s = (pltpu.roll(kk, dist, 0), pltpu.roll(b, dist, 0), pltpu.roll(vf, dist, 0))
            w = jnp.sum(qh * k_s * jnp.exp(jnp.minimum(b - b_s, 0.0)), axis=1, keepdims=True)
            o = o + jnp.where(step >= dist, w, 0.0) * v_s
        qd = (qh * jnp.exp(b)).astype(BF16)
        for j in range(rows // steps):
            s = s_in_ref[j, h]
            o = o + jnp.where(row_batch == j, _dot(qd, s.astype(BF16)), 0.0)
            b_last = b[(j + 1) * steps - 1:(j + 1) * steps, :]
            col = jnp.sum(jnp.where(eye, jnp.exp(b_last), 0.0), axis=1, keepdims=True)
            kj = jnp.where(row_batch == j, kk * jnp.exp(jnp.minimum(b_last - b, 0.0)), 0.0).astype(BF16)
            s_out_ref[j, h] = col * s + _dot_tn(kj, v)
        o_ref[:, cols] = (_rms(o) * gn_ref[:, cols] * _silu(g_ref[:, cols].astype(F32))).astype(BF16)

    def group(gi, carry):
        for hh in range(HG_HEADS_PER_STEP):
            head(gi * HG_HEADS_PER_STEP + hh)
        return carry

    lax.fori_loop(0, heads // HG_HEADS_PER_STEP, group, 0)


def _hg_sample(proj, state, n_prompt, steps, heads, lb, out_norm):
    n_sample_batch = state.shape[0]
    d = heads * HG_DH
    per = SAMPLE_ROWS // steps
    base = n_prompt // SAMPLE_ROWS
    col = lambda part: (lambda b: (base + b, part))
    return pl.pallas_call(
        functools.partial(_hg_sample_kernel, steps, heads),
        grid=(n_sample_batch // per,),
        in_specs=[pl.BlockSpec((SAMPLE_ROWS, d), col(0)),
                  pl.BlockSpec((SAMPLE_ROWS, d), col(1)),
                  pl.BlockSpec((SAMPLE_ROWS, d), col(2)),
                  pl.BlockSpec((SAMPLE_ROWS, d), col(3)),
                  pl.BlockSpec((1, d), lambda b: (0, 0)),
                  pl.BlockSpec((1, d), lambda b: (0, 0)),
                  pl.BlockSpec((per, heads, HG_DH, HG_DH), lambda b: (b, 0, 0, 0))],
        out_specs=[pl.BlockSpec((SAMPLE_ROWS, d), lambda b: (b, 0)),
                   pl.BlockSpec((per, heads, HG_DH, HG_DH), lambda b: (b, 0, 0, 0))],
        out_shape=[jax.ShapeDtypeStruct((n_sample_batch * steps, d), BF16),
                   jax.ShapeDtypeStruct(state.shape, F32)],
        compiler_params=_params("arbitrary"),
        name="hg_sample",
    )(proj, proj, proj, proj, lb, out_norm, state)


def _first_max(vals, lane, width):
    m = jnp.max(vals, axis=1, keepdims=True)
    idx = jnp.min(jnp.where(vals == m, lane, width), axis=1, keepdims=True)
    return m, idx


def _route(logits, bias):
    neg = -jnp.inf
    lane = lax.broadcasted_iota(I32, logits.shape, 1)
    group = lane >> GROUP_SHIFT
    scores = jax.nn.sigmoid(logits)
    biased = scores + bias
    best = sel = None
    for gi in range(N_GROUPS):
        vals = jnp.where(group == gi, biased, neg)
        m1, i1 = _first_max(vals, lane, N_EXPERTS)
        m2 = jnp.max(jnp.where(lane == i1, neg, vals), axis=1, keepdims=True)
        total = m1 + m2
        if gi == 0:
            best, sel = total, jnp.zeros_like(i1)
        else:
            better = total > best
            sel = jnp.where(better, gi, sel)
            best = jnp.where(better, total, best)
    vals = jnp.where(group == sel, biased, neg)
    _, e1 = _first_max(vals, lane, N_EXPERTS)
    _, e2 = _first_max(jnp.where(lane == e1, neg, vals), lane, N_EXPERTS)
    w1 = jnp.sum(jnp.where(lane == e1, scores, 0.0), axis=1, keepdims=True)
    w2 = jnp.sum(jnp.where(lane == e2, scores, 0.0), axis=1, keepdims=True)
    tot = w1 + w2
    return e1, e2, w1 / tot, w2 / tot


def _mixout_kernel(prompt_tiles, o_p, o_s, w_ref, x_p, x_s, g1_ref, g2_ref, gm_p, gm_s, sh_p, sh_s, sc_p, sc_s,
                   rw_ref, rb_ref, xo_p, xo_s, h_ref, ridx_ref, rwt_ref, cnt_ref, carry_ref, y_ref):
    i = pl.program_id(0)
    tm = x_p.shape[0]

    @pl.when(i == 0)
    def _():
        carry_ref[...] = jnp.zeros_like(carry_ref)

    def tile(sample):
        o_ref, x_ref, xo_ref = (o_s, x_s, xo_s) if sample else (o_p, x_p, xo_p)
        y_ref[...] = _dot(o_ref[...], w_ref[...])

        def chunk(rows, grp):
            x = x_ref[rows, :] + _mod_rows(sample, gm_p, gm_s, rows) * (_rms(y_ref[rows, :]) * g1_ref[...])
            xo_ref[rows, :] = x
            h = (_rms(x) * g2_ref[...] * (1.0 + _mod_rows(sample, sc_p, sc_s, rows))
                 + _mod_rows(sample, sh_p, sh_s, rows))
            h_ref[grp] = h.reshape(ROW_CHUNK // SUBLANES, SUBLANES, h.shape[1])

        _row_chunks(tm, chunk)

    _by_group(i >= prompt_tiles, tile)

    h1, h2 = _split2(h_ref[...].reshape(tm, h_ref.shape[2]))
    w1, w2 = _split2(rw_ref[...])
    first = _dot(h1, jnp.concatenate([w1, w2], axis=1))
    logits = first[:, :N_EXPERTS] + (first[:, N_EXPERTS:] + _dot(h2, w1))
    e1, e2, p1, p2 = _route(logits, rb_ref[...])

    lane = lax.broadcasted_iota(I32, (tm, N_EXPERTS), 1)
    hot1, hot2 = lane == e1, lane == e2
    onehot = (hot1 | hot2).astype(BF16)
    ti = lax.broadcasted_iota(I32, (tm, tm), 0)
    si = lax.broadcasted_iota(I32, (tm, tm), 1)
    before = _dot((si < ti).astype(BF16), onehot) + carry_ref[...]
    r1 = jnp.sum(jnp.where(hot1, before, 0.0), axis=1, keepdims=True).astype(I32)
    r2 = jnp.sum(jnp.where(hot2, before, 0.0), axis=1, keepdims=True).astype(I32)
    carry = carry_ref[...] + jnp.sum(onehot.astype(F32), axis=0, keepdims=True)
    carry_ref[...] = carry

    wide = lax.broadcasted_iota(I32, (tm, 128), 1)
    ridx_ref[...] = jnp.where(wide == 0, e1, jnp.where(wide == 1, e2, jnp.where(wide == 2, r1, r2)))
    rwt_ref[...] = jnp.where(wide == 0, p1, p2)
    cnt_ref[...] = jnp.zeros_like(cnt_ref)
    cnt_ref[0:1, 0:N_EXPERTS] = carry


def _mixout(tok, o_prompt, o_sample, w_out, x_p, x_s, g1, g2, mod_p, mod_s, router_w, router_b):
    d = tok.d
    v = o_prompt.shape[1]
    return pl.pallas_call(
        functools.partial(_mixout_kernel, tok.prompt_tiles),
        grid=(tok.tiles,),
        in_specs=tok.split_specs(v)
                 + [pl.BlockSpec((v, d), lambda i: (0, 0), pipeline_mode=pl.Buffered(1))]
                 + tok.split_specs(d) + [_const_spec((1, d)), _const_spec((1, d))]
                 + tok.mod_specs(2) + tok.mod_specs(3) + tok.mod_specs(4)
                 + [_const_spec((d, N_EXPERTS)), _const_spec((1, N_EXPERTS))],
        out_specs=tok.split_specs(d, out=True) + [
            pl.BlockSpec((TOKEN_TILE // SUBLANES, SUBLANES, d), lambda i: (i, 0, 0)),
            tok.row_spec(128), tok.row_spec(128), _const_spec((8, 128))],
        out_shape=tok.split_shapes(d, F32) + [jax.ShapeDtypeStruct((tok.n // SUBLANES, SUBLANES, d), F32),
                                              jax.ShapeDtypeStruct((tok.n, 128), I32),
                                              jax.ShapeDtypeStruct((tok.n, 128), F32),
                                              jax.ShapeDtypeStruct((8, 128), F32)],
        scratch_shapes=[pltpu.VMEM((1, N_EXPERTS), F32), pltpu.VMEM((TOKEN_TILE, d), F32)],
        compiler_params=_params("arbitrary"),
        name="mix_out",
    )(o_prompt, o_sample, w_out, x_p, x_s, g1, g2, mod_p, mod_s, mod_p, mod_s, mod_p, mod_s, router_w, router_b)


def _expert_plan(ridx, counts, n_tiles):
    n = ridx.shape[0]
    expert = ridx[:, 0:2]
    rank = ridx[:, 2:4]
    cnt = counts[0, :N_EXPERTS].astype(I32)
    padded = ((cnt + EXPERT_TILE - 1) // EXPERT_TILE) * EXPERT_TILE
    ends = jnp.cumsum(padded)
    starts = ends - padded
    pos = starts[expert] + rank
    tile = jnp.arange(n_tiles, dtype=I32)
    tile_start = tile * EXPERT_TILE
    used = tile_start < ends[-1]
    last_used = jnp.maximum(ends[-1] // EXPERT_TILE - 1, 0)
    tile_expert = jnp.sum((tile_start[:, None] >= ends[None, :]).astype(I32), axis=1)
    tile_expert = jnp.minimum(jnp.where(used, tile_expert, tile_expert[last_used]), N_EXPERTS - 1)
    valid = jnp.clip(cnt[tile_expert] - (tile_start - starts[tile_expert]), 0, EXPERT_TILE)
    valid = jnp.where(used, valid, 0)
    pos = pos.reshape(n // TOKEN_TILE, TOKEN_TILE, 2).transpose(0, 2, 1)
    sub = jnp.arange(EXPERT_F_SPLIT, dtype=I32)[None, :]
    snake = jnp.where((tile[:, None] & 1) == 0, sub, EXPERT_F_SPLIT - 1 - sub)
    part = jnp.where(used[:, None], snake, snake[last_used, EXPERT_F_SPLIT - 1]).reshape(-1)
    return tile_expert, valid, jnp.minimum(tile, last_used), part, pos


def _start_rows(groups, copy):
    def body(g, carry):
        for j in range(SUBLANES):
            copy(g, j).start(priority=j % 2)
        return carry
    lax.fori_loop(0, groups, body, 0)


def _dispatch_kernel(pos_ref, h_ref, xs_in, xs_hbm, sem):
    del xs_in
    groups = h_ref.shape[0]
    for choice in range(2):
        def row(g, j, choice=choice):
            p = pos_ref[0, choice, g * SUBLANES + j]
            return pltpu.make_async_copy(h_ref.at[g, pl.ds(j, 1)],
                                         xs_hbm.at[p >> 3, pl.ds(p & (SUBLANES - 1), 1)], sem.at[choice])
        _start_rows(groups, row)
    for choice in range(2):
        pltpu.make_async_copy(h_ref, xs_hbm.at[pl.ds(0, groups)], sem.at[choice]).wait()


def _dispatch(h, pos, n_rows, buf=None):
    d = h.shape[2]
    n = h.shape[0] * SUBLANES
    if buf is None:
        buf = jnp.zeros((n_rows // SUBLANES, SUBLANES, d), F32)
    tile_groups = TOKEN_TILE // SUBLANES
    return pl.pallas_call(
        _dispatch_kernel,
        grid=(n // TOKEN_TILE,),
        in_specs=[pl.BlockSpec((1, 2, TOKEN_TILE), lambda i: (i, 0, 0), memory_space=pltpu.SMEM),
                  pl.BlockSpec((tile_groups, SUBLANES, d), lambda i: (i, 0, 0)),
                  pl.BlockSpec(memory_space=pl.ANY)],
        out_specs=pl.BlockSpec(memory_space=pl.ANY),
        out_shape=jax.ShapeDtypeStruct((n_rows // SUBLANES, SUBLANES, d), F32),
        scratch_shapes=[pltpu.SemaphoreType.DMA((2,))],
        input_output_aliases={2: 0},
        compiler_params=_params("arbitrary"),
        name="dispatch",
    )(pos, h, buf)


def _expert_kernel(te_ref, nv_ref, tin_ref, part_ref, x_ref, wg_ref, wu_ref, wd_ref, y_ref):
    del te_ref, tin_ref, part_ref
    n = nv_ref[pl.program_id(0)]
    step = pl.program_id(1)
    rows = x_ref.shape[0] * SUBLANES

    @pl.when(n > 0)
    def _():
        x = x_ref[...].reshape(rows, x_ref.shape[2]).astype(BF16)
        a = _dot(x, wg_ref[...].astype(BF16))
        u = _dot(x, wu_ref[...].astype(BF16))
        y = _dot((_silu(a) * u).astype(BF16), wd_ref[...].astype(BF16)).reshape(y_ref.shape)

        @pl.when(step == 0)
        def _():
            y_ref[...] = y

        @pl.when(step > 0)
        def _():
            y_ref[...] += y

    @pl.when((n == 0) & (step == 0))
    def _():
        y_ref[...] = jnp.zeros_like(y_ref)


def _experts(xs, plan, layer, wg, wu, wd):
    d, f = wg.shape[2], wg.shape[3]
    fs = f // EXPERT_F_SPLIT
    tile_expert, valid, tile_in, part = plan[:4]
    n_tiles = tile_expert.shape[0]
    tile_groups = EXPERT_TILE // SUBLANES
    which = lambda i, s, part: part[i * EXPERT_F_SPLIT + s]
    grid_spec = pltpu.PrefetchScalarGridSpec(
        num_scalar_prefetch=4,
        grid=(n_tiles, EXPERT_F_SPLIT),
        in_specs=[pl.BlockSpec((tile_groups, SUBLANES, d), lambda i, s, te, nv, tin, part: (tin[i], 0, 0)),
                  pl.BlockSpec((None, None, d, fs),
                               lambda i, s, te, nv, tin, part: (layer, te[i], 0, which(i, s, part))),
                  pl.BlockSpec((None, None, d, fs),
                               lambda i, s, te, nv, tin, part: (layer, te[i], 0, which(i, s, part))),
                  pl.BlockSpec((None, None, fs, d),
                               lambda i, s, te, nv, tin, part: (layer, te[i], which(i, s, part), 0))],
        out_specs=pl.BlockSpec((tile_groups, SUBLANES, d), lambda i, s, te, nv, tin, part: (i, 0, 0)),
    )
    return pl.pallas_call(
        _expert_kernel,
        grid_spec=grid_spec,
        out_shape=jax.ShapeDtypeStruct((n_tiles * tile_groups, SUBLANES, d), F32),
        compiler_params=_params("arbitrary", "arbitrary"),
        name="experts",
    )(tile_expert, valid, tile_in, part, xs, wg, wu, wd)


def _by_group(is_sample, body):
    @pl.when(is_sample)
    def _():
        body(True)

    @pl.when(jnp.logical_not(is_sample))
    def _():
        body(False)


def _row_chunks(n_rows, fn):
    per = ROW_CHUNK // SUBLANES

    def body(ci, carry):
        fn(pl.ds(pl.multiple_of(ci * ROW_CHUNK, ROW_CHUNK), ROW_CHUNK), pl.ds(pl.multiple_of(ci * per, per), per))
        return carry
    lax.fori_loop(0, n_rows // ROW_CHUNK, body, 0, unroll=ROW_CHUNK_UNROLL)


def _mod_rows(sample, p_ref, s_ref, rows):
    return s_ref[rows, :] if sample else p_ref[...]


def _moeout_kernel(prompt_tiles, with_next, pos_ref, pos_next_ref, ys_hbm, x_p, x_s, rwt_ref, g3_ref, gf_p, gf_s,
                   *rest):
    if with_next:
        gn_ref, sh_p, sh_s, sc_p, sc_s, xo_p, xo_s, h_ref, ybuf, sem = rest
    else:
        xo_p, xo_s, ybuf, sem = rest
    groups = ybuf.shape[2]
    n_rows = groups * SUBLANES
    d = ybuf.shape[4]
    i = pl.program_id(0)
    slot = i & 1

    def fetch(p_ref, to_slot):
        for choice in range(2):
            def row(g, j, choice=choice):
                p = p_ref[0, choice, g * SUBLANES + j]
                return pltpu.make_async_copy(ys_hbm.at[p >> 3, pl.ds(p & (SUBLANES - 1), 1)],
                                             ybuf.at[to_slot, choice, g, pl.ds(j, 1)], sem.at[to_slot, choice])
            _start_rows(groups, row)

    @pl.when(i == 0)
    def _():
        fetch(pos_ref, 0)

    @pl.when(i + 1 < pl.num_programs(0))
    def _():
        fetch(pos_next_ref, 1 - slot)

    for choice in range(2):
        pltpu.make_async_copy(ys_hbm.at[pl.ds(0, groups)], ybuf.at[slot, choice], sem.at[slot, choice]).wait()

    def tile(sample):
        x_ref, xo_ref = (x_s, xo_s) if sample else (x_p, xo_p)

        def chunk(rows, grp):
            rw = rwt_ref[rows, :]
            y = (rw[:, 0:1] * ybuf[slot, 0, grp].reshape(ROW_CHUNK, d)
                 + rw[:, 1:2] * ybuf[slot, 1, grp].reshape(ROW_CHUNK, d))
            x = x_ref[rows, :] + _mod_rows(sample, gf_p, gf_s, rows) * (_rms(y) * g3_ref[...])
            xo_ref[rows, :] = x
            if with_next:
                h = (_rms(x) * gn_ref[...] * (1.0 + _mod_rows(sample, sc_p, sc_s, rows))
                     + _mod_rows(sample, sh_p, sh_s, rows))
                h_ref[rows, :] = h.astype(BF16)

        _row_chunks(n_rows, chunk)

    _by_group(pl.program_id(0) >= prompt_tiles, tile)


def _moeout(tok, pos, ys, x_p, x_s, rwt, g3, mod_p, mod_s, nxt=None):
    d = tok.d
    last = tok.tiles - 1
    ins = ([pl.BlockSpec((1, 2, TOKEN_TILE), lambda i: (i, 0, 0), memory_space=pltpu.SMEM),
            pl.BlockSpec((1, 2, TOKEN_TILE), lambda i: (jnp.minimum(i + 1, last), 0, 0), memory_space=pltpu.SMEM),
            pl.BlockSpec(memory_space=pl.ANY)]
           + tok.split_specs(d) + [tok.row_spec(128), _const_spec((1, d))] + tok.mod_specs(5))
    args = [pos, pos, ys, x_p, x_s, rwt, g3, mod_p, mod_s]
    scratch = [pltpu.VMEM((2, 2, TOKEN_TILE // SUBLANES, SUBLANES, d), F32), pltpu.SemaphoreType.DMA((2, 2))]
    if nxt is None:
        return pl.pallas_call(
            functools.partial(_moeout_kernel, tok.prompt_tiles, False),
            grid=(tok.tiles,), in_specs=ins, out_specs=tok.split_specs(d, out=True),
            out_shape=tok.split_shapes(d, F32), scratch_shapes=scratch,
            compiler_params=_params("arbitrary"), name="moe_out_last",
        )(*args)
    gain_n, mod_pn, mod_sn = nxt
    ins = ins + [_const_spec((1, d))] + tok.mod_specs(0) + tok.mod_specs(1)
    args = args + [gain_n, mod_pn, mod_sn, mod_pn, mod_sn]
    return pl.pallas_call(
        functools.partial(_moeout_kernel, tok.prompt_tiles, True),
        grid=(tok.tiles,), in_specs=ins, out_specs=tok.split_specs(d, out=True) + [tok.row_spec(d)],
        out_shape=tok.split_shapes(d, F32) + [jax.ShapeDtypeStruct((tok.n, d), BF16)],
        scratch_shapes=scratch,
        compiler_params=_params("arbitrary"), name="moe_out_next",
    )(*args)


def kernel(x_prompt, x_sample, c_prompt, c_sample, state_ret, state_hgrn, ada_w, ada_b, norm_gains,
           ret_w_in, ret_w_out, hg_w_in, hg_w_out, hg_lower_bound, hg_out_norm, router_w, router_b,
           exp_w_gate, exp_w_up, exp_w_down):
    n_batch, seq, d = x_prompt.shape
    n_dec, steps, _ = x_sample.shape
    depth = ada_w.shape[0]
    assert depth == 2 and d % RET_DK == 0 and d % HG_DH == 0
    assert seq % RET_CHUNK == 0 and (n_dec * steps) % TOKEN_TILE == 0 and SAMPLE_ROWS % steps == 0
    ret_heads = d // RET_DK
    hg_heads = d // HG_DH
    n_prompt = n_batch * seq
    n_sample = n_dec * steps
    tok = _Tokens(n_batch, seq, n_sample, d)
    n = tok.n

    n_cond = n_batch + n_dec
    pad = (-n_cond) % 8
    c_all = jnp.concatenate([c_prompt, c_sample, jnp.zeros((pad, d), F32)], axis=0)
    mod = _ada(c_all, ada_w, ada_b)
    mods = []
    for l in range(depth):
        mod_p = mod[l, :n_batch].reshape(n_batch, 6, 1, d)
        mod_s = jnp.take(mod[l], n_batch + jnp.arange(n_sample, dtype=I32) // steps, axis=0)
        mods.append((mod_p, mod_s))
    gain = lambda l, k: norm_gains[l, k].reshape(1, d)

    x_p = x_prompt.reshape(n_prompt, d)
    x_s = x_sample.reshape(n_sample, d)
    n_tiles = (2 * n + N_EXPERTS * (EXPERT_TILE - 1)) // EXPERT_TILE + 1
    rw = router_w.astype(F32)
    rb = router_b.astype(F32).reshape(1, N_EXPERTS)

    def channel_mixer(l, x_p, x_s, h, ridx, rwt, counts, nxt, buf):
        plan = _expert_plan(ridx, counts, n_tiles)
        pos = plan[4]
        xs = _dispatch(h, pos, n_tiles * EXPERT_TILE, buf)
        ys = _experts(xs, plan, l, exp_w_gate, exp_w_up, exp_w_down)
        return _moeout(tok, pos, ys, x_p, x_s, rwt, gain(l, 3), *mods[l], nxt=nxt), xs

    h = _prenorm(tok, x_p, x_s, gain(0, 0), *mods[0])
    proj = _proj(h, ret_w_in[0])
    log_gamma = jnp.log(1.0 - jnp.exp2(-5.0 - jnp.arange(ret_heads, dtype=F32)))
    log_gamma = jnp.broadcast_to(log_gamma[:, None, None], (ret_heads, 1, 128))
    cos_p, sin_p = _rope_tables(jnp.arange(seq))
    cos_s, sin_s = _rope_tables(PAST_LEN + jnp.arange(steps))
    reps = SAMPLE_ROWS // steps
    cos_s, sin_s = jnp.tile(cos_s, (reps, 1)), jnp.tile(sin_s, (reps, 1))
    o_p, ret_prompt = _ret_prompt(proj, n_batch, seq, ret_heads, cos_p, sin_p, log_gamma)
    o_s, ret_sample = _ret_sample(proj, state_ret[0], n_prompt, steps, ret_heads, cos_s, sin_s, log_gamma)
    x_p, x_s, hp, ridx, rwt, counts = _mixout(tok, o_p, o_s, ret_w_out[0].astype(BF16), x_p, x_s,
                                              gain(0, 1), gain(0, 2), *mods[0], rw, rb)
    (x_p, x_s, h), row_buf = channel_mixer(0, x_p, x_s, hp, ridx, rwt, counts, (gain(1, 0),) + mods[1], None)

    sm = jax.nn.softmax(hg_lower_bound.astype(F32), axis=0)
    lb = (jnp.cumsum(sm, axis=0) - sm[0])[1].reshape(1, d)
    proj = _proj(h, hg_w_in[0])
    out_norm = hg_out_norm[0].reshape(1, d)
    leaf_min = _hg_guard(proj, n_prompt, hg_heads, lb)[0, 0]
    o_p, hg_prompt = lax.cond(
        leaf_min > HG_LEAF_LOG_LIMIT,
        functools.partial(_hg_prompt, False, n_batch=n_batch, seq=seq, heads=hg_heads),
        functools.partial(_hg_prompt, True, n_batch=n_batch, seq=seq, heads=hg_heads),
        proj, lb, out_norm)
    o_s, hg_sample = _hg_sample(proj, state_hgrn[0], n_prompt, steps, hg_heads, lb, out_norm)
    x_p, x_s, hp, ridx, rwt, counts = _mixout(tok, o_p, o_s, hg_w_out[0].astype(BF16), x_p, x_s,
                                              gain(1, 1), gain(1, 2), *mods[1], rw, rb)
    (x_p, x_s), _ = channel_mixer(1, x_p, x_s, hp, ridx, rwt, counts, None, row_buf)

    return (x_p.reshape(n_batch, seq, d), x_s.reshape(n_dec, steps, d),
            ret_prompt[None], hg_prompt[None], ret_sample[None], hg_sample[None])
```
